```python
import math
import jax, jax.numpy as jnp
from jax import lax
import numpy as np

D_MODEL = 1024
BATCH = 8
SEQ = 2048
DEPTH = 1

CHUNK = 64
Q_BLOCK = 128
D_MIX = D_MODEL
D_ATTN = D_MIX // 2
D_RWKV = D_MIX - D_ATTN
DH_A = 64
DV_A = 2 * DH_A
H_A = D_ATTN // DV_A
ROT_DIM = DH_A // 4
ROPE_THETA = 500000.0
QK_NORM_EPS = 1e-6
SUBLN_EPS = 1e-5
RWKV_HEAD = 64
H_R = D_RWKV // RWKV_HEAD
DECAY_LORA = 64
AAA_LORA = 64
GATE_LORA = 128
LN_X_EPS = 64e-5
RWKV_COLS = 3 * D_RWKV + DECAY_LORA + AAA_LORA + GATE_LORA
D_IN = 3 * D_ATTN + RWKV_COLS
N_GROUPS = 4
EXPERTS_PER_GROUP = 8
N_EXPERTS = N_GROUPS * EXPERTS_PER_GROUP
TOP_K = 2
D_EXPERT = 512
MOE_BLOCK = 128
NORM_EPS = 1e-6

kernel_name = 'hymba_diffattn_rwkv7_hmoe_block'


def rms_norm(x, g, eps):
    xf = x.astype(jnp.float32)
    y = xf * lax.rsqrt(jnp.mean(xf * xf, axis=-1, keepdims=True) + eps)
    return (y * g.astype(jnp.float32)).astype(x.dtype)


def rope_tables(seq):
    inv = ROPE_THETA ** (-jnp.arange(0, ROT_DIM, 2, dtype=jnp.float32) / ROT_DIM)
    ang = jnp.arange(seq, dtype=jnp.float32)[:, None] * inv[None, :]
    return jnp.cos(ang), jnp.sin(ang)


def partial_rope(t, cos, sin):
    half = ROT_DIM // 2
    c = cos[None, :, None, None, :].astype(t.dtype)
    s = sin[None, :, None, None, :].astype(t.dtype)
    x1 = t[..., :half]
    x2 = t[..., half:ROT_DIM]
    return jnp.concatenate([x1 * c - x2 * s, x2 * c + x1 * s, t[..., ROT_DIM:]], axis=-1)


def diff_attention(q, k, v, lam):
    b, s = q.shape[0], q.shape[1]
    nb = s // Q_BLOCK
    scale = DH_A ** -0.5
    qb = q.reshape(b, nb, Q_BLOCK, H_A, 2, DH_A).transpose(1, 0, 2, 3, 4, 5)
    key_chunk = jnp.arange(s) // CHUNK

    def one_block(args):
        q_blk, i = args
        sc = jnp.einsum('bqhcd,bkhcd->bhcqk', q_blk, k).astype(jnp.float32) * scale
        q_chunk = (i * Q_BLOCK + jnp.arange(Q_BLOCK)) // CHUNK
        allowed = key_chunk[None, :] <= q_chunk[:, None]
        sc = jnp.where(allowed, sc, jnp.finfo(jnp.float32).min)
        p = jax.nn.softmax(sc, axis=-1)
        p_diff = p[:, :, 0] - lam * p[:, :, 1]
        return jnp.einsum('bhqk,bkhe->bqhe', p_diff.astype(v.dtype), v)

    o = lax.map(one_block, (qb, jnp.arange(nb)))
    return o.transpose(1, 0, 2, 3, 4).reshape(b, s, H_A, DV_A)


def diff_attn_mixer(u, cos, sin, q_norm_g, k_norm_g, lq1, lk1, lq2, lk2, subln_g, lambda_init):
    f32 = jnp.float32
    b, s, _ = u.shape
    q = u[..., :D_ATTN].reshape(b, s, H_A, 2, DH_A)
    k = u[..., D_ATTN:2 * D_ATTN].reshape(b, s, H_A, 2, DH_A)
    v = u[..., 2 * D_ATTN:].reshape(b, s, H_A, DV_A)
    q = partial_rope(rms_norm(q, q_norm_g, QK_NORM_EPS), cos, sin)
    k = partial_rope(rms_norm(k, k_norm_g, QK_NORM_EPS), cos, sin)
    lam = (jnp.exp(jnp.sum(lq1.astype(f32) * lk1.astype(f32)))
           - jnp.exp(jnp.sum(lq2.astype(f32) * lk2.astype(f32))) + lambda_init)
    o = diff_attention(q, k, v, lam)
    o = rms_norm(o, subln_g, SUBLN_EPS) * (1.0 - lambda_init)
    return o.reshape(b, s, D_ATTN)


def rwkv7_mixer(u, shift_mu, w0, w_lora_up, a0, a_lora_up, g_lora_up, k_k, k_a, r_k, ln_x_g, ln_x_b):
    f32 = jnp.float32
    b, s, _ = u.shape
    u_prev = jnp.concatenate([jnp.zeros_like(u[:, :1]), u[:, :-1]], axis=1)
    u = u + (u_prev - u) * shift_mu
    r = u[..., :D_RWKV]
    k = u[..., D_RWKV:2 * D_RWKV]
    v = u[..., 2 * D_RWKV:3 * D_RWKV]
    o = 3 * D_RWKV
    w_lo = u[..., o:o + DECAY_LORA]
    a_lo = u[..., o + DECAY_LORA:o + DECAY_LORA + AAA_LORA]
    g_lo = u[..., o + DECAY_LORA + AAA_LORA:]
    w = -jax.nn.softplus(-(w0.astype(f32) + (jnp.tanh(w_lo) @ w_lora_up).astype(f32))) - 0.5
    decay = jnp.exp(-jnp.exp(w))
    a = jax.nn.sigmoid(a0.astype(f32) + (a_lo @ a_lora_up).astype(f32))
    g = jax.nn.sigmoid(g_lo) @ g_lora_up
    kk = (k.astype(f32) * k_k.astype(f32)).reshape(b, s, H_R, RWKV_HEAD)
    kk = kk / jnp.maximum(jnp.linalg.norm(kk, axis=-1, keepdims=True), 1e-12)
    k_mod = k.astype(f32) * (1.0 + (a - 1.0) * k_a.astype(f32))
    heads = lambda t: t.astype(f32).reshape(b, s, H_R, RWKV_HEAD)
    r_h, k_h, v_h, w_h, a_h = heads(r), heads(k_mod), heads(v), heads(decay), heads(a)
    b_h = kk * a_h

    def step(state, inp):
        r_t, w_t, k_t, v_t, kk_t, b_t = inp
        sa = jnp.einsum('bhvk,bhk->bhv', state, -kk_t)
        state = (state * w_t[:, :, None, :] + sa[..., None] * b_t[:, :, None, :]
                 + v_t[..., None] * k_t[:, :, None, :])
        return state, jnp.einsum('bhvk,bhk->bhv', state, r_t)

    tm = lambda t: jnp.moveaxis(t, 1, 0)
    state0 = jnp.zeros((b, H_R, RWKV_HEAD, RWKV_HEAD), f32)
    _, y = lax.scan(step, state0, (tm(r_h), tm(w_h), tm(k_h), tm(v_h), tm(kk), tm(b_h)))
    y = jnp.moveaxis(y, 0, 1)
    mu = jnp.mean(y, axis=-1, keepdims=True)
    var = jnp.mean(jnp.square(y - mu), axis=-1, keepdims=True)
    y = ((y - mu) * lax.rsqrt(var + LN_X_EPS)).reshape(b, s, D_RWKV)
    y = y * ln_x_g.astype(f32) + ln_x_b.astype(f32)
    bonus = jnp.sum(r_h * k_h * r_k.astype(f32).reshape(H_R, RWKV_HEAD), axis=-1, keepdims=True) * v_h
    y = (y + bonus.reshape(b, s, D_RWKV)) * g.astype(f32)
    return y.astype(u.dtype)


def hier_moe(h, w_rg, b_rg, w_re, b_re, w_gate, w_up, w_down):
    f32 = jnp.float32
    b, s, d = h.shape
    n = b * s
    t = h.reshape(n, d)
    gprob = jax.nn.softmax((t @ w_rg).astype(f32) + b_rg.astype(f32), axis=-1)
    p_group, gsel = lax.top_k(gprob, 1)
    elog = ((t @ w_re).astype(f32) + b_re.astype(f32)).reshape(n, N_GROUPS, EXPERTS_PER_GROUP)
    elog_sel = elog[jnp.arange(n), gsel[:, 0]]
    top_p, top_i = lax.top_k(jax.nn.softmax(elog_sel, axis=-1), TOP_K)
    gate = p_group * top_p / jnp.sum(top_p, axis=-1, keepdims=True)
    eid = gsel * EXPERTS_PER_GROUP + top_i

    flat_e = eid.reshape(-1)
    order = jnp.argsort(flat_e)
    sorted_e = flat_e[order]
    tok = order // TOP_K
    counts = jnp.bincount(flat_e, length=N_EXPERTS)
    padded = (counts + MOE_BLOCK - 1) // MOE_BLOCK * MOE_BLOCK
    pad_ends = jnp.cumsum(padded)
    pad_starts = pad_ends - padded
    starts = jnp.cumsum(counts) - counts
    dest = pad_starts[sorted_e] + (jnp.arange(n * TOP_K) - starts[sorted_e])
    m_pad = n * TOP_K + N_EXPERTS * MOE_BLOCK
    n_blocks = m_pad // MOE_BLOCK
    xs = jnp.zeros((m_pad, d), t.dtype).at[dest].set(t[tok])
    block_e = jnp.minimum(jnp.searchsorted(pad_ends, jnp.arange(n_blocks) * MOE_BLOCK, side='right'),
                          N_EXPERTS - 1)

    def expert_block(args):
        xb, e = args
        return (jax.nn.silu(xb @ w_gate[e]) * (xb @ w_up[e])) @ w_down[e]

    y_pad = lax.map(expert_block, (xs.reshape(n_blocks, MOE_BLOCK, d), block_e)).reshape(m_pad, d)
    dest_by_slot = jnp.zeros((n * TOP_K,), dest.dtype).at[order].set(dest)
    y_slots = y_pad[dest_by_slot].reshape(n, TOP_K, d)
    y = jnp.einsum('nkd,nk->nd', y_slots, gate.astype(y_slots.dtype))
    return y.reshape(b, s, d)


def setup_inputs(seed: int = 0) -> dict:
    key = jax.random.key(seed)
    ks = iter(jax.random.split(key, 32))
    f32 = jnp.float32
    L = DEPTH
    nrm = lambda shape, sc: jax.random.normal(next(ks), shape, f32) * sc
    uni = lambda shape, lo, hi: jax.random.uniform(next(ks), shape, f32, lo, hi)
    return {
        'x': nrm((BATCH, SEQ, D_MODEL), 1.0),
        'ln1_g': 1.0 + nrm((L, D_MODEL), 0.02),
        'w_in': nrm((L, D_MODEL, D_IN), D_MODEL ** -0.5),
        'q_norm_g': 1.0 + nrm((L, DH_A), 0.02),
        'k_norm_g': 1.0 + nrm((L, DH_A), 0.02),
        'lambda_q1': nrm((L, DH_A), 0.1),
        'lambda_k1': nrm((L, DH_A), 0.1),
        'lambda_q2': nrm((L, DH_A), 0.1),
        'lambda_k2': nrm((L, DH_A), 0.1),
        'subln_g': 1.0 + nrm((L, DV_A), 0.02),
        'shift_mu': uni((L, RWKV_COLS), 0.0, 1.0),
        'w0': uni((L, D_RWKV), -6.5, -1.5),
        'w_lora_up': nrm((L, DECAY_LORA, D_RWKV), 0.1),
        'a0': nrm((L, D_RWKV), 0.1),
        'a_lora_up': nrm((L, AAA_LORA, D_RWKV), AAA_LORA ** -0.5),
        'g_lora_up': nrm((L, GATE_LORA, D_RWKV), GATE_LORA ** -0.5),
        'k_k': 0.85 + nrm((L, D_RWKV), 0.02),
        'k_a': 1.0 + nrm((L, D_RWKV), 0.02),
        'r_k': nrm((L, D_RWKV), 0.1),
        'ln_x_g': 1.0 + nrm((L, D_RWKV), 0.02),
        'ln_x_b': nrm((L, D_RWKV), 0.02),
        'w_out': nrm((L, D_MIX, D_MODEL), D_MIX ** -0.5),
        'ln2_g': 1.0 + nrm((L, D_MODEL), 0.02),
        'w_router_group': nrm((L, D_MODEL, N_GROUPS), D_MODEL ** -0.5),
        'b_router_group': nrm((L, N_GROUPS), 0.01),
        'w_router_expert': nrm((L, D_MODEL, N_EXPERTS), D_MODEL ** -0.5),
        'b_router_expert': nrm((L, N_EXPERTS), 0.01),
        'w_exp_gate': nrm((L, N_EXPERTS, D_MODEL, D_EXPERT), D_MODEL ** -0.5),
        'w_exp_up': nrm((L, N_EXPERTS, D_MODEL, D_EXPERT), D_MODEL ** -0.5),
        'w_exp_down': nrm((L, N_EXPERTS, D_EXPERT, D_MODEL), D_EXPERT ** -0.5),
    }


def reference(x, ln1_g, w_in, q_norm_g, k_norm_g, lambda_q1, lambda_k1, lambda_q2, lambda_k2, subln_g,
              shift_mu, w0, w_lora_up, a0, a_lora_up, g_lora_up, k_k, k_a, r_k, ln_x_g, ln_x_b,
              w_out, ln2_g, w_router_group, b_router_group, w_router_expert, b_router_expert,
              w_exp_gate, w_exp_up, w_exp_down):
    cos, sin = rope_tables(x.shape[1])
    h = x
    for l in range(DEPTH):
        lambda_init = 0.8 - 0.6 * math.exp(-0.3 * l)
        u = rms_norm(h, ln1_g[l], NORM_EPS) @ w_in[l]
        o_a = diff_attn_mixer(u[..., :3 * D_ATTN], cos, sin, q_norm_g[l], k_norm_g[l],
                              lambda_q1[l], lambda_k1[l], lambda_q2[l], lambda_k2[l],
                              subln_g[l], lambda_init)
        o_b = rwkv7_mixer(u[..., 3 * D_ATTN:], shift_mu[l], w0[l], w_lora_up[l], a0[l],
                          a_lora_up[l], g_lora_up[l], k_k[l], k_a[l], r_k[l], ln_x_g[l], ln_x_b[l])
        h = h + jnp.concatenate([o_a, o_b], axis=-1) @ w_out[l]
        h = h + hier_moe(rms_norm(h, ln2_g[l], NORM_EPS), w_router_group[l], b_router_group[l],
                         w_router_expert[l], b_router_expert[l], w_exp_gate[l], w_exp_up[l],
                         w_exp_down[l])
    return h
```

```python
import functools
import math

import jax
import jax.numpy as jnp
from jax import lax
from jax.experimental import pallas as pl
from jax.experimental.pallas import tpu as pltpu

F32 = jnp.float32
BF16 = jnp.bfloat16
I32 = jnp.int32

CHUNK = 64
DH_A = 64
DV_A = 128
ROT_DIM = 16
ROPE_THETA = 500000.0
QK_NORM_EPS = 1e-6
SUBLN_EPS = 1e-5
RWKV_HEAD = 64
LN_X_EPS = 64e-5
N_GROUPS = 4
EXPERTS_PER_GROUP = 8
N_EXPERTS = 32
NORM_EPS = 1e-6

LANES = 128
VMEM_LIMIT = 56 * 1024 * 1024

_NEG = -1e30


def _dot(a, b):
    return jnp.dot(a, b, preferred_element_type=F32)


def _dot_nt(a, b):
    return lax.dot_general(a, b, (((1,), (1,)), ((), ())), preferred_element_type=F32)


def _bdot(a, b):
    return _dot(a.astype(BF16), b.astype(BF16))


def _split(x):
    hi = x.astype(BF16)
    lo = (x - hi.astype(F32)).astype(BF16)
    return hi, lo


def _in_proj_body(x_ref, g_ref, wa_ref, wb_ref, ua_ref, ub_ref):
    x = x_ref[...]
    ms = jnp.mean(x * x, axis=-1, keepdims=True)
    xn = (x * lax.rsqrt(ms + NORM_EPS) * g_ref[...]).astype(BF16)
    ua_ref[...] = _dot(xn, wa_ref[...])
    ub_ref[...] = _dot(xn, wb_ref[...])


def _in_proj(x2, g, wa, wb):
    n, d = x2.shape
    tm = min(512, n)
    na, nb = wa.shape[1], wb.shape[1]
    return pl.pallas_call(
        _in_proj_body,
        grid=(n // tm,),
        in_specs=[
            pl.BlockSpec((tm, d), lambda i: (i, 0)),
            pl.BlockSpec((1, d), lambda i: (0, 0)),
            pl.BlockSpec((d, na), lambda i: (0, 0)),
            pl.BlockSpec((d, nb), lambda i: (0, 0)),
        ],
        out_specs=[
            pl.BlockSpec((tm, na), lambda i: (i, 0)),
            pl.BlockSpec((tm, nb), lambda i: (i, 0)),
        ],
        out_shape=[jax.ShapeDtypeStruct((n, na), F32), jax.ShapeDtypeStruct((n, nb), F32)],
        compiler_params=pltpu.CompilerParams(
            dimension_semantics=("arbitrary",), vmem_limit_bytes=VMEM_LIMIT),
        name="in_proj",
    )(x2, g, wa, wb)


def _attn_body(q_ref, k_ref, v_ref, cos_ref, sa_ref, sb_ref, gq_ref, gk_ref, l_ref, sg_ref,
               o_ref, q0_s, q1_s, k_s, v_s, *, lambda_init, bq):
    s_len = q_ref.shape[0]
    lane = lax.broadcasted_iota(I32, (1, LANES), 1)
    lo = lane < DH_A

    def prep(t, g):
        t2 = t * t
        s_lo = jnp.sum(jnp.where(lo, t2, 0.0), axis=-1, keepdims=True)
        s_hi = jnp.sum(jnp.where(lo, 0.0, t2), axis=-1, keepdims=True)
        ms = jnp.where(lo, s_lo, s_hi) * (1.0 / DH_A)
        tn = t * lax.rsqrt(ms + QK_NORM_EPS) * g
        return (tn * cos_ref[...] + pltpu.roll(tn, LANES - ROT_DIM // 2, 1) * sa_ref[...]
                + pltpu.roll(tn, ROT_DIM // 2, 1) * sb_ref[...])

    q = prep(q_ref[...], gq_ref[...]) * (DH_A ** -0.5)
    q0_s[...] = jnp.where(lo, q, 0.0).astype(BF16)
    q1_s[...] = jnp.where(lo, 0.0, q).astype(BF16)
    k_s[...] = prep(k_ref[...], gk_ref[...]).astype(BF16)
    v_s[...] = v_ref[...].astype(BF16)

    l = l_ref[...]
    lam = (jnp.exp(jnp.sum(l[0:1] * l[1:2], axis=-1, keepdims=True))
           - jnp.exp(jnp.sum(l[2:3] * l[3:4], axis=-1, keepdims=True)) + lambda_init)

    chunk_shift = int(math.log2(CHUNK))
    rr = lax.broadcasted_iota(I32, (bq, bq), 0) >> chunk_shift
    cc = lax.broadcasted_iota(I32, (bq, bq), 1) >> chunk_shift
    diag_ok = cc <= rr

    for i in range(s_len // bq):
        r0 = i * bq
        e_diag, e_off, l_sum = [], [], []
        for q_s in (q0_s, q1_s):
            qb = q_s[r0:r0 + bq, :]
            sd = jnp.where(diag_ok, _dot_nt(qb, k_s[r0:r0 + bq, :]), _NEG)
            m = jnp.max(sd, axis=-1, keepdims=True)
            if r0 > 0:
                so = _dot_nt(qb, k_s[0:r0, :])
                m = jnp.maximum(m, jnp.max(so, axis=-1, keepdims=True))
            ed = jnp.exp(sd - m)
            ls = jnp.sum(ed, axis=-1, keepdims=True)
            if r0 > 0:
                eo = jnp.exp(so - m)
                ls = ls + jnp.sum(eo, axis=-1, keepdims=True)
                e_off.append(eo)
            e_diag.append(ed)
            l_sum.append(ls)
        w0 = 1.0 / l_sum[0]
        w1 = lam / l_sum[1]
        pd = (e_diag[0] * w0 - e_diag[1] * w1).astype(BF16)
        o = _dot(pd, v_s[r0:r0 + bq, :])
        if r0 > 0:
            po = (e_off[0] * w0 - e_off[1] * w1).astype(BF16)
            o = o + _dot(po, v_s[0:r0, :])
        o = o * lax.rsqrt(jnp.mean(o * o, axis=-1, keepdims=True) + SUBLN_EPS)
        o_ref[r0:r0 + bq, :] = o * (sg_ref[...] * (1.0 - lambda_init))


def _diff_attn(ua3, cosf, sina, sinb, gq, gk, lvec, sg, lambda_init):
    b, s, _ = ua3.shape
    h_a = ua3.shape[2] // (3 * DV_A)
    bq = min(256, s)
    blk = lambda off: pl.BlockSpec((None, s, DV_A), lambda bi, hi: (bi, 0, off + hi))
    const = lambda shape: pl.BlockSpec(shape, lambda bi, hi: (0,) * len(shape))
    return pl.pallas_call(
        functools.partial(_attn_body, lambda_init=lambda_init, bq=bq),
        grid=(b, h_a),
        in_specs=[blk(0), blk(h_a), blk(2 * h_a),
                  const((s, LANES)), const((s, LANES)), const((s, LANES)),
                  const((1, LANES)), const((1, LANES)), const((4, DH_A)), const((1, LANES))],
        out_specs=pl.BlockSpec((None, s, DV_A), lambda bi, hi: (bi, 0, hi)),
        out_shape=jax.ShapeDtypeStruct((b, s, h_a * DV_A), F32),
        scratch_shapes=[pltpu.VMEM((s, LANES), BF16)] * 4,
        compiler_params=pltpu.CompilerParams(
            dimension_semantics=("arbitrary", "arbitrary"), vmem_limit_bytes=VMEM_LIMIT),
        name="diff_attn",
    )(ua3, ua3, ua3, cosf, sina, sinb, gq, gk, lvec, sg)


def _rwkv_body(u_ref, mu_ref, w0_ref, wlu_ref, a0_ref, alu_ref, glu_ref, kkw_ref, ka_ref, rk_ref,
               lg_ref, lb_ref, bd_ref, tri_ref, o_ref, prev_s, st_s):
    c_len = u_ref.shape[0]
    d_r = o_ref.shape[1]
    n_pair = d_r // LANES

    @pl.when(pl.program_id(1) == 0)
    def _():
        prev_s[...] = jnp.zeros_like(prev_s)
        st_s[...] = jnp.zeros_like(st_s)

    u = u_ref[...]
    row = lax.broadcasted_iota(I32, (c_len, 1), 0)
    u_prev = jnp.where(row == 0, prev_s[...], pltpu.roll(u, 1, 0))
    prev_s[...] = u[c_len - 1:c_len, :]
    us = u + (u_prev - u) * mu_ref[...]

    r = us[:, 0:d_r]
    k = us[:, d_r:2 * d_r]
    v = us[:, 2 * d_r:3 * d_r]
    lo_in = us[:, 3 * d_r:3 * d_r + LANES]
    g_in = us[:, 3 * d_r + LANES:3 * d_r + 2 * LANES]

    bd = bd_ref[...]

    def head_sum(x):
        hi, lo = _split(x)
        return _dot(hi, bd) + _dot(lo, bd)

    z = -(w0_ref[...] + _bdot(jnp.tanh(lo_in), wlu_ref[...]))
    softplus = jnp.maximum(z, 0.0) + jnp.log(1.0 + jnp.exp(-jnp.abs(z)))
    logw = -jnp.exp(-softplus - 0.5)
    a = 1.0 / (1.0 + jnp.exp(-(a0_ref[...] + _bdot(lo_in, alu_ref[...]))))
    g = _bdot(1.0 / (1.0 + jnp.exp(-g_in)), glu_ref[...])
    kk_raw = k * kkw_ref[...]
    kk = kk_raw / jnp.maximum(jnp.sqrt(head_sum(kk_raw * kk_raw)), 1e-12)
    k_mod = k * (1.0 + (a - 1.0) * ka_ref[...])

    tri = tri_ref[...]
    w_hi, w_lo = _split(logw)
    cum = _dot(tri, w_hi) + _dot(tri, w_lo)
    cum_last = cum[c_len - 1:c_len, :]
    e_neg = jnp.exp(-cum)
    e_rem = jnp.exp(cum_last - cum)
    a_t = -kk * jnp.exp(cum - logw)
    r_t = r * jnp.exp(cum)
    b_raw = kk * a
    b_t = b_raw * e_neg
    k_t = k_mod * e_neg
    b_2 = b_raw * e_rem
    k_2 = k_mod * e_rem
    p_c = jnp.exp(cum_last)

    lane = lax.broadcasted_iota(I32, (1, LANES), 1)
    head0 = lane < RWKV_HEAD
    two_c = 2 * c_len
    rr = lax.broadcasted_iota(I32, (two_c, two_c), 0)
    cc = lax.broadcasted_iota(I32, (two_c, two_c), 1)
    t_i = rr & (c_len - 1)
    s_i = cc & (c_len - 1)
    strict = t_i > s_i
    incl = t_i >= s_i
    eye = rr == cc

    def stack(x, j):
        xp = x[:, j * LANES:(j + 1) * LANES]
        return jnp.concatenate([jnp.where(head0, xp, 0.0), jnp.where(head0, 0.0, xp)], axis=0)

    n_double = int(math.log2(c_len))
    y_pairs = []
    for j in range(n_pair):
        a_s, r_s, b_s, k_s, v_s = (stack(t, j) for t in (a_t, r_t, b_t, k_t, v))
        gram = _dot_nt(jnp.concatenate([a_s, r_s], axis=0).astype(BF16),
                       jnp.concatenate([b_s, k_s], axis=0).astype(BF16))
        n_ab = jnp.where(strict, gram[:two_c, :two_c], 0.0)
        a_ak = jnp.where(strict, gram[:two_c, two_c:], 0.0)
        m_rb = jnp.where(incl, gram[two_c:, :two_c], 0.0)
        m_rk = jnp.where(incl, gram[two_c:, two_c:], 0.0)
        vv = _bdot(jnp.concatenate([a_ak, m_rk], axis=0), v_s)
        x = jnp.concatenate([a_s, vv[:two_c]], axis=1)
        p = n_ab
        for it in range(n_double):
            x = x + _bdot(p, x)
            if it + 1 < n_double:
                p = _bdot(p, p)
        mx = _bdot(m_rb, x)
        r_h = r_s + mx[:, :two_c]
        y_h = mx[:, two_c:] + vv[two_c:]
        bx = _bdot(stack(b_2, j).T, x)
        g_m = jnp.where(eye, p_c[:, j * LANES:(j + 1) * LANES], 0.0) + bx[:, :two_c]
        h_m = bx[:, two_c:] + _bdot(stack(k_2, j).T, v_s)
        st = st_s[j].astype(BF16)
        y_s = _dot(r_h.astype(BF16), st) + y_h
        st_s[j] = _dot(g_m.astype(BF16), st) + h_m
        y_pairs.append(y_s[:c_len] + y_s[c_len:])
    y = jnp.concatenate(y_pairs, axis=1)

    inv_n = 1.0 / RWKV_HEAD
    mu = head_sum(y) * inv_n
    dlt = y - mu
    var = head_sum(dlt * dlt) * inv_n
    yn = dlt * lax.rsqrt(var + LN_X_EPS) * lg_ref[...] + lb_ref[...]
    bonus = head_sum(r * k_mod * rk_ref[...]) * v
    o_ref[...] = (yn + bonus) * g


def _rwkv7(ub3, mu, w0, wlu, a0, alu, glu, kkw, ka, rk, lg, lb, bd, tri):
    b, s, cols = ub3.shape
    d_r = w0.shape[1]
    c_len = tri.shape[0]
    const = lambda a: pl.BlockSpec(a.shape, lambda bi, ci: (0,) * a.ndim)
    params = (mu, w0, wlu, a0, alu, glu, kkw, ka, rk, lg, lb, bd, tri)
    return pl.pallas_call(
        _rwkv_body,
        grid=(b, s // c_len),
        in_specs=[pl.BlockSpec((None, c_len, cols), lambda bi, ci: (bi, ci, 0))]
                 + [const(a) for a in params],
        out_specs=pl.BlockSpec((None, c_len, d_r), lambda bi, ci: (bi, ci, 0)),
        out_shape=jax.ShapeDtypeStruct((b, s, d_r), F32),
        scratch_shapes=[pltpu.VMEM((1, cols), F32),
                        pltpu.VMEM((d_r // LANES, LANES, LANES), F32)],
        compiler_params=pltpu.CompilerParams(
            dimension_semantics=("arbitrary", "arbitrary"), vmem_limit_bytes=VMEM_LIMIT),
        name="rwkv7",
    )(ub3, *params)


def _out_body(x_ref, oa_ref, ob_ref, wa_ref, wb_ref, g2_ref, wrh_ref, wrl_ref, br_ref, tril_ref,
              h_ref, t_ref, eid_ref, gate_ref, rank_ref, cnt_ref, run_s):
    tm = x_ref.shape[0]

    @pl.when(pl.program_id(0) == 0)
    def _():
        run_s[...] = jnp.zeros_like(run_s)

    h = x_ref[...] + _bdot(oa_ref[...], wa_ref[...]) + _bdot(ob_ref[...], wb_ref[...])
    h_ref[...] = h
    t = h * lax.rsqrt(jnp.mean(h * h, axis=-1, keepdims=True) + NORM_EPS) * g2_ref[...]
    t_ref[...] = t

    t_hi, t_lo = _split(t)
    wrh = wrh_ref[...]
    lg = _dot(t_hi, wrh) + _dot(t_lo, wrh) + _dot(t_hi, wrl_ref[...]) + br_ref[...]

    lane_i = lax.broadcasted_iota(I32, (1, LANES), 1)
    lane = lane_i.astype(F32)
    big = float(LANES)

    def first_index(mask):
        return jnp.min(jnp.where(mask, lane, big), axis=-1, keepdims=True)

    def masked_softmax(mask):
        m = jnp.max(jnp.where(mask, lg, _NEG), axis=-1, keepdims=True)
        e = jnp.where(mask, jnp.exp(lg - m), 0.0)
        return e / jnp.sum(e, axis=-1, keepdims=True)

    is_g = lane_i < N_GROUPS
    gprob = masked_softmax(is_g)
    p_group = jnp.max(gprob, axis=-1, keepdims=True)
    gsel = first_index(is_g & (gprob == p_group))
    base = N_GROUPS + gsel * EXPERTS_PER_GROUP
    in_e = (lane >= base) & (lane < base + EXPERTS_PER_GROUP)
    eprob = masked_softmax(in_e)
    p1 = jnp.max(eprob, axis=-1, keepdims=True)
    i1 = first_index(in_e & (eprob == p1))
    rest = in_e & (lane != i1)
    p2 = jnp.max(jnp.where(rest, eprob, -1.0), axis=-1, keepdims=True)
    i2 = first_index(rest & (eprob == p2))
    den = p1 + p2
    g1 = p_group * p1 / den
    g2 = p_group * p2 / den
    e1 = i1 - N_GROUPS
    e2 = i2 - N_GROUPS

    oh1 = (lane == e1).astype(F32)
    oh2 = (lane == e2).astype(F32)
    tril = tril_ref[...]
    before1 = _dot(tril, oh1.astype(BF16))
    tot1 = jnp.sum(oh1, axis=0, keepdims=True)
    before2 = _dot(tril, oh2.astype(BF16)) + tot1
    run = run_s[...]
    rk1 = jnp.sum(oh1 * (before1 + run), axis=-1, keepdims=True)
    rk2 = jnp.sum(oh2 * (before2 + run), axis=-1, keepdims=True)
    run = run + tot1 + jnp.sum(oh2, axis=0, keepdims=True)
    run_s[...] = run
    cnt_ref[...] = run

    sel0 = lane_i == 0
    sel1 = lane_i == 1
    eid_ref[...] = jnp.where(sel0, e1, jnp.where(sel1, e2, 0.0)).astype(I32)
    rank_ref[...] = jnp.where(sel0, rk1, jnp.where(sel1, rk2, 0.0)).astype(I32)
    gate_ref[...] = jnp.where(sel0, g1, jnp.where(sel1, g2, 0.0))


def _out_proj_route(x2, oa, ob, wa, wb, g2, wrh, wrl, br, tril):
    n, d = x2.shape
    tm = tril.shape[0]
    da, db = oa.shape[1], ob.shape[1]
    row = lambda w: pl.BlockSpec((tm, w), lambda i: (i, 0))
    const = lambda a: pl.BlockSpec(a.shape, lambda i: (0,) * a.ndim)
    return pl.pallas_call(
        _out_body,
        grid=(n // tm,),
        in_specs=[row(d), row(da), row(db), const(wa), const(wb), const(g2), const(wrh), const(wrl),
                  const(br), const(tril)],
        out_specs=[row(d), row(d), row(LANES), row(LANES), row(LANES),
                   pl.BlockSpec((1, LANES), lambda i: (0, 0))],
        out_shape=[jax.ShapeDtypeStruct((n, d), F32), jax.ShapeDtypeStruct((n, d), F32),
                   jax.ShapeDtypeStruct((n, LANES), I32), jax.ShapeDtypeStruct((n, LANES), F32),
                   jax.ShapeDtypeStruct((n, LANES), I32), jax.ShapeDtypeStruct((1, LANES), F32)],
        scratch_shapes=[pltpu.VMEM((1, LANES), F32)],
        compiler_params=pltpu.CompilerParams(
            dimension_semantics=("arbitrary",), vmem_limit_bytes=VMEM_LIMIT),
        name="out_proj_route",
    )(x2, oa, ob, wa, wb, g2, wrh, wrl, br, tril)


def _dispatch_body(d0_ref, d1_ref, t_ref, xs_ref, sem):
    tm = d0_ref.shape[0]
    base = pl.program_id(0) * tm

    def copy(i, d_ref):
        return pltpu.make_async_copy(t_ref.at[pl.ds(base + i, 1)], xs_ref.at[pl.ds(d_ref[i], 1)], sem)

    def start(i, carry):
        copy(i, d0_ref).start()
        copy(i, d1_ref).start()
        return carry

    def wait(i, carry):
        copy(i, d0_ref).wait()
        copy(i, d1_ref).wait()
        return carry

    lax.fori_loop(0, tm, start, 0)
    lax.fori_loop(0, tm, wait, 0)


def _dispatch(d0, d1, t):
    n, d = t.shape
    tm = min(512, n)
    smem = pl.BlockSpec((tm,), lambda i: (i,), memory_space=pltpu.SMEM)
    return pl.pallas_call(
        _dispatch_body,
        grid=(n // tm,),
        in_specs=[smem, smem, pl.BlockSpec(memory_space=pl.ANY)],
        out_specs=pl.BlockSpec(memory_space=pl.ANY),
        out_shape=jax.ShapeDtypeStruct((2 * n, d), t.dtype),
        scratch_shapes=[pltpu.SemaphoreType.DMA],
        compiler_params=pltpu.CompilerParams(dimension_semantics=("arbitrary",)),
        name="moe_dispatch",
    )(d0, d1, t)


def _experts_body(ub_ref, ue_ref, gs_ref, nu_ref, x_ref, wg_ref, wu_ref, wd_ref, y_ref):
    w = pl.program_id(0)
    bm = x_ref.shape[0]

    @pl.when(w < nu_ref[0])
    def _():
        e = ue_ref[w]
        blk = ub_ref[w]
        rows = blk * bm + lax.broadcasted_iota(I32, (bm, 1), 0)
        valid = (rows >= gs_ref[e]) & (rows < gs_ref[e + 1])
        xb = x_ref[...].astype(BF16)
        gt = _dot(xb, wg_ref[...])
        up = _dot(xb, wu_ref[...])
        act = (gt / (1.0 + jnp.exp(-gt))) * up
        y = jnp.where(valid, _dot(act.astype(BF16), wd_ref[...]), 0.0)
        first = jnp.logical_or(w == 0, ub_ref[jnp.maximum(w - 1, 0)] != blk)

        @pl.when(first)
        def _():
            y_ref[...] = y

        @pl.when(jnp.logical_not(first))
        def _():
            y_ref[...] += y


def _experts(unit_blk, unit_exp, gstart, n_units, xs, wg, wu, wd, bm):
    m, d = xs.shape
    de = wg.shape[2]
    max_units = unit_blk.shape[0]
    grid_spec = pltpu.PrefetchScalarGridSpec(
        num_scalar_prefetch=4,
        grid=(max_units,),
        in_specs=[
            pl.BlockSpec((bm, d), lambda w, ub, ue, gs, nu: (ub[w], 0)),
            pl.BlockSpec((None, d, de), lambda w, ub, ue, gs, nu: (ue[w], 0, 0)),
            pl.BlockSpec((None, d, de), lambda w, ub, ue, gs, nu: (ue[w], 0, 0)),
            pl.BlockSpec((None, de, d), lambda w, ub, ue, gs, nu: (ue[w], 0, 0)),
        ],
        out_specs=pl.BlockSpec((bm, d), lambda w, ub, ue, gs, nu: (ub[w], 0)),
    )
    return pl.pallas_call(
        _experts_body,
        grid_spec=grid_spec,
        out_shape=jax.ShapeDtypeStruct((m, d), F32),
        compiler_params=pltpu.CompilerParams(
            dimension_semantics=("arbitrary",), vmem_limit_bytes=VMEM_LIMIT),
        name="moe_experts",
    )(unit_blk, unit_exp, gstart, n_units, xs, wg, wu, wd)


def _combine_body(d0_ref, d1_ref, y_ref, h_ref, gate_ref, o_ref, b0, b1, sem):
    tm = d0_ref.shape[0]

    def copy(i, d_ref, buf):
        return pltpu.make_async_copy(y_ref.at[pl.ds(d_ref[i], 1)], buf.at[pl.ds(i, 1)], sem)

    def start(i, carry):
        copy(i, d0_ref, b0).start()
        copy(i, d1_ref, b1).start()
        return carry

    def wait(i, carry):
        copy(i, d0_ref, b0).wait()
        copy(i, d1_ref, b1).wait()
        return carry

    lax.fori_loop(0, tm, start, 0)
    lax.fori_loop(0, tm, wait, 0)
    gate = gate_ref[...]
    o_ref[...] = h_ref[...] + b0[...] * gate[:, 0:1] + b1[...] * gate[:, 1:2]


def _combine(d0, d1, y, h, gate):
    n, d = h.shape
    tm = min(256, n)
    smem = pl.BlockSpec((tm,), lambda i: (i,), memory_space=pltpu.SMEM)
    return pl.pallas_call(
        _combine_body,
        grid=(n // tm,),
        in_specs=[smem, smem, pl.BlockSpec(memory_space=pl.ANY),
                  pl.BlockSpec((tm, d), lambda i: (i, 0)),
                  pl.BlockSpec((tm, LANES), lambda i: (i, 0))],
        out_specs=pl.BlockSpec((tm, d), lambda i: (i, 0)),
        out_shape=jax.ShapeDtypeStruct((n, d), F32),
        scratch_shapes=[pltpu.VMEM((tm, d), F32), pltpu.VMEM((tm, d), F32), pltpu.SemaphoreType.DMA],
        compiler_params=pltpu.CompilerParams(
            dimension_semantics=("arbitrary",), vmem_limit_bytes=VMEM_LIMIT),
        name="moe_combine",
    )(d0, d1, y, h, gate)


def _rope_tables(seq):
    half = ROT_DIM // 2
    inv = ROPE_THETA ** (-jnp.arange(0, ROT_DIM, 2, dtype=F32) / ROT_DIM)
    ang = jnp.arange(seq, dtype=F32)[:, None] * inv[None, :]
    cos, sin = jnp.cos(ang), jnp.sin(ang)
    ones = jnp.ones((seq, DH_A - ROT_DIM), F32)
    zeros = jnp.zeros((seq, DH_A - half), F32)
    zeros_h = jnp.zeros((seq, half), F32)
    zeros_r = jnp.zeros((seq, DH_A - ROT_DIM), F32)
    cos_c = jnp.concatenate([cos, cos, ones], axis=1)
    sa_c = jnp.concatenate([-sin, zeros], axis=1)
    sb_c = jnp.concatenate([zeros_h, sin, zeros_r], axis=1)
    tile = lambda t: jnp.concatenate([t, t], axis=1)
    return tile(cos_c), tile(sa_c), tile(sb_c)


def _layer(x, l, lambda_init, ln1_g, w_in, q_norm_g, k_norm_g, lambda_q1, lambda_k1, lambda_q2,
           lambda_k2, subln_g, shift_mu, w0, w_lora_up, a0, a_lora_up, g_lora_up, k_k, k_a, r_k,
           ln_x_g, ln_x_b, w_out, ln2_g, w_router_group, b_router_group, w_router_expert,
           b_router_expert, w_exp_gate, w_exp_up, w_exp_down):
    b, s, d = x.shape
    n = b * s
    d_attn3 = 3 * (d // 2)
    d_attn = d // 2
    d_r = w0.shape[1]
    x2 = x.reshape(n, d)
    row = lambda a: a[l].reshape(1, -1)

    w_in_b = w_in[l].astype(BF16)
    ua, ub = _in_proj(x2, row(ln1_g), w_in_b[:, :d_attn3], w_in_b[:, d_attn3:])

    cosf, sina, sinb = _rope_tables(s)
    tile2 = lambda a: jnp.concatenate([a[l], a[l]]).reshape(1, -1)
    lvec = jnp.stack([lambda_q1[l], lambda_k1[l], lambda_q2[l], lambda_k2[l]])
    o_a = _diff_attn(ua.reshape(b, s, d_attn3), cosf, sina, sinb, tile2(q_norm_g), tile2(k_norm_g),
                     lvec, row(subln_g), lambda_init)

    dl = w_lora_up.shape[1]
    zeros_l = jnp.zeros((dl, d_r), F32)
    wlu = jnp.concatenate([w_lora_up[l], zeros_l], axis=0).astype(BF16)
    alu = jnp.concatenate([zeros_l, a_lora_up[l]], axis=0).astype(BF16)
    hid = jnp.arange(d_r) // RWKV_HEAD
    bd = (hid[:, None] == hid[None, :]).astype(BF16)
    c_len = min(CHUNK, s)
    tri = (jnp.arange(c_len)[:, None] >= jnp.arange(c_len)[None, :]).astype(BF16)
    o_b = _rwkv7(ub.reshape(b, s, -1), row(shift_mu), row(w0), wlu, row(a0), alu,
                 g_lora_up[l].astype(BF16), row(k_k), row(k_a), row(r_k), row(ln_x_g), row(ln_x_b),
                 bd, tri)

    w_out_b = w_out[l].astype(BF16)
    wr = jnp.concatenate([w_router_group[l], w_router_expert[l],
                          jnp.zeros((d, LANES - N_GROUPS - N_EXPERTS), F32)], axis=1)
    wr_hi = wr.astype(BF16)
    wr_lo = (wr - wr_hi.astype(F32)).astype(BF16)
    br = jnp.concatenate([b_router_group[l], b_router_expert[l],
                          jnp.zeros((LANES - N_GROUPS - N_EXPERTS,), F32)]).reshape(1, LANES)
    tm_r = min(512, n)
    tril = (jnp.arange(tm_r)[:, None] > jnp.arange(tm_r)[None, :]).astype(BF16)
    h, t, eid, gate, rank, cnt = _out_proj_route(
        x2, o_a.reshape(n, d_attn), o_b.reshape(n, d_r), w_out_b[:d_attn], w_out_b[d_attn:],
        row(ln2_g), wr_hi, wr_lo, br, tril)

    counts = cnt[0, :N_EXPERTS].astype(I32)
    gend = jnp.cumsum(counts)
    gstart = jnp.concatenate([jnp.zeros((1,), I32), gend]).astype(I32)
    onehot = (eid[:, :2, None] == jnp.arange(N_EXPERTS)[None, None, :])
    dest = jnp.sum(jnp.where(onehot, gstart[:N_EXPERTS], 0), axis=-1) + rank[:, :2]
    d0 = dest[:, 0].astype(I32)
    d1 = dest[:, 1].astype(I32)

    m = 2 * n
    bm = min(256, m)
    n_blk = m // bm
    max_units = n_blk + N_EXPERTS - 1
    first_blk = gstart[:N_EXPERTS] // bm
    last_blk = jnp.where(counts > 0, (gend - 1) // bm, first_blk)
    units_e = jnp.where(counts > 0, last_blk - first_blk + 1, 0)
    unit_end = jnp.cumsum(units_e)
    unit_start = unit_end - units_e
    n_units = unit_end[-1]
    w_ids = jnp.arange(max_units)
    u_exp = jnp.minimum(jnp.searchsorted(unit_end, w_ids, side="right"), N_EXPERTS - 1)
    u_blk = first_blk[u_exp] + (w_ids - unit_start[u_exp])
    last = jnp.maximum(n_units - 1, 0)
    pad = w_ids >= n_units
    u_exp = jnp.where(pad, u_exp[last], u_exp).astype(I32)
    u_blk = jnp.where(pad, u_blk[last], u_blk).astype(I32)

    xs = _dispatch(d0, d1, t)
    y = _experts(u_blk, u_exp, gstart, n_units.reshape(1).astype(I32), xs,
                 w_exp_gate[l].astype(BF16), w_exp_up[l].astype(BF16), w_exp_down[l].astype(BF16), bm)
    out = _combine(d0, d1, y, h, gate)
    return out.reshape(b, s, d)


def kernel(x, ln1_g, w_in, q_norm_g, k_norm_g, lambda_q1, lambda_k1, lambda_q2, lambda_k2, subln_g, shift_mu, w0, w_lora_up, a0, a_lora_up, g_lora_up, k_k, k_a, r_k, ln_x_g, ln_x_b, w_out, ln2_g, w_router_group, b_router_group, w_router_expert, b_router_expert, w_exp_gate, w_exp_up, w_exp_down):
    h = x
    for l in range(ln1_g.shape[0]):
        lambda_init = 0.8 - 0.6 * math.exp(-0.3 * l)
        h = _layer(h, l, lambda_init, ln1_g, w_in, q_norm_g, k_norm_g, lambda_q1, lambda_k1,
                   lambda_q2, lambda_k2, subln_g, shift_mu, w0, w_lora_up, a0, a_lora_up, g_lora_up,
                   k_k, k_a, r_k, ln_x_g, ln_x_b, w_out, ln2_g, w_router_group, b_router_group,
                   w_router_expert, b_router_expert, w_exp_gate, w_exp_up, w_exp_down)
    return h
```

```python
import functools
import math

import jax
import jax.numpy as jnp
from jax import lax
from jax.experimental import pallas as pl
from jax.experimental.pallas import tpu as pltpu

F32 = jnp.float32
BF16 = jnp.bfloat16
I32 = jnp.int32

CHUNK = 64
DH_A = 64
DV_A = 128
ROT_DIM = 16
ROPE_THETA = 500000.0
QK_NORM_EPS = 1e-6
SUBLN_EPS = 1e-5
RWKV_HEAD = 64
LN_X_EPS = 64e-5
N_GROUPS = 4
EXPERTS_PER_GROUP = 8
N_EXPERTS = 32
NORM_EPS = 1e-6

LANES = 128
VMEM_LIMIT = 56 * 1024 * 1024

_NEG = -1e30


def _dot(a, b):
    return jnp.dot(a, b, preferred_element_type=F32)


def _dot_nt(a, b):
    return lax.dot_general(a, b, (((1,), (1,)), ((), ())), preferred_element_type=F32)


def _bdot(a, b):
    return _dot(a.astype(BF16), b.astype(BF16))


def _split(x):
    hi = x.astype(BF16)
    lo = (x - hi.astype(F32)).astype(BF16)
    return hi, lo


def _in_proj_body(x_ref, g_ref, wa_ref, wb_ref, ua_ref, ub_ref):
    x = x_ref[...]
    ms = jnp.mean(x * x, axis=-1, keepdims=True)
    xn = (x * lax.rsqrt(ms + NORM_EPS) * g_ref[...]).astype(BF16)
    ua_ref[...] = _dot(xn, wa_ref[...])
    ub_ref[...] = _dot(xn, wb_ref[...])


def _in_proj(x2, g, wa, wb):
    n, d = x2.shape
    tm = min(512, n)
    na, nb = wa.shape[1], wb.shape[1]
    return pl.pallas_call(
        _in_proj_body,
        grid=(n // tm,),
        in_specs=[
            pl.BlockSpec((tm, d), lambda i: (i, 0)),
            pl.BlockSpec((1, d), lambda i: (0, 0)),
            pl.BlockSpec((d, na), lambda i: (0, 0)),
            pl.BlockSpec((d, nb), lambda i: (0, 0)),
        ],
        out_specs=[
            pl.BlockSpec((tm, na), lambda i: (i, 0)),
            pl.BlockSpec((tm, nb), lambda i: (i, 0)),
        ],
        out_shape=[jax.ShapeDtypeStruct((n, na), F32), jax.ShapeDtypeStruct((n, nb), F32)],
        compiler_params=pltpu.CompilerParams(
            dimension_semantics=("arbitrary",), vmem_limit_bytes=VMEM_LIMIT),
        name="in_proj",
    )(x2, g, wa, wb)


def _attn_body(q_ref, k_ref, v_ref, cos_ref, sa_ref, sb_ref, gq_ref, gk_ref, l_ref, sg_ref,
               o_ref, q0_s, q1_s, k_s, v_s, *, lambda_init, bq):
    s_len = q_ref.shape[0]
    lane = lax.broadcasted_iota(I32, (1, LANES), 1)
    lo = lane < DH_A

    def prep(t, g):
        t2 = t * t
        s_lo = jnp.sum(jnp.where(lo, t2, 0.0), axis=-1, keepdims=True)
        s_hi = jnp.sum(jnp.where(lo, 0.0, t2), axis=-1, keepdims=True)
        ms = jnp.where(lo, s_lo, s_hi) * (1.0 / DH_A)
        tn = t * lax.rsqrt(ms + QK_NORM_EPS) * g
        return (tn * cos_ref[...] + pltpu.roll(tn, LANES - ROT_DIM // 2, 1) * sa_ref[...]
                + pltpu.roll(tn, ROT_DIM // 2, 1) * sb_ref[...])

    q = prep(q_ref[...], gq_ref[...]) * (DH_A ** -0.5)
    q0_s[...] = jnp.where(lo, q, 0.0).astype(BF16)
    q1_s[...] = jnp.where(lo, 0.0, q).astype(BF16)
    k_s[...] = prep(k_ref[...], gk_ref[...]).astype(BF16)
    v_s[...] = v_ref[...].astype(BF16)

    l = l_ref[...]
    lam = (jnp.exp(jnp.sum(l[0:1] * l[1:2], axis=-1, keepdims=True))
           - jnp.exp(jnp.sum(l[2:3] * l[3:4], axis=-1, keepdims=True)) + lambda_init)

    chunk_shift = int(math.log2(CHUNK))
    rr = lax.broadcasted_iota(I32, (bq, bq), 0) >> chunk_shift
    cc = lax.broadcasted_iota(I32, (bq, bq), 1) >> chunk_shift
    diag_ok = cc <= rr

    for i in range(s_len // bq):
        r0 = i * bq
        e_diag, e_off, l_sum = [], [], []
        for q_s in (q0_s, q1_s):
            qb = q_s[r0:r0 + bq, :]
            sd = jnp.where(diag_ok, _dot_nt(qb, k_s[r0:r0 + bq, :]), _NEG)
            m = jnp.max(sd, axis=-1, keepdims=True)
            if r0 > 0:
                so = _dot_nt(qb, k_s[0:r0, :])
                m = jnp.maximum(m, jnp.max(so, axis=-1, keepdims=True))
            ed = jnp.exp(sd - m)
            ls = jnp.sum(ed, axis=-1, keepdims=True)
            if r0 > 0:
                eo = jnp.exp(so - m)
                ls = ls + jnp.sum(eo, axis=-1, keepdims=True)
                e_off.append(eo)
            e_diag.append(ed)
            l_sum.append(ls)
        w0 = 1.0 / l_sum[0]
        w1 = lam / l_sum[1]
        pd = (e_diag[0] * w0 - e_diag[1] * w1).astype(BF16)
        o = _dot(pd, v_s[r0:r0 + bq, :])
        if r0 > 0:
            po = (e_off[0] * w0 - e_off[1] * w1).astype(BF16)
            o = o + _dot(po, v_s[0:r0, :])
        o = o * lax.rsqrt(jnp.mean(o * o, axis=-1, keepdims=True) + SUBLN_EPS)
        o_ref[r0:r0 + bq, :] = o * (sg_ref[...] * (1.0 - lambda_init))


def _diff_attn(ua3, cosf, sina, sinb, gq, gk, lvec, sg, lambda_init):
    b, s, _ = ua3.shape
    h_a = ua3.shape[2] // (3 * DV_A)
    bq = min(256, s)
    blk = lambda off: pl.BlockSpec((None, s, DV_A), lambda bi, hi: (bi, 0, off + hi))
    const = lambda shape: pl.BlockSpec(shape, lambda bi, hi: (0,) * len(shape))
    return pl.pallas_call(
        functools.partial(_attn_body, lambda_init=lambda_init, bq=bq),
        grid=(b, h_a),
        in_specs=[blk(0), blk(h_a), blk(2 * h_a),
                  const((s, LANES)), const((s, LANES)), const((s, LANES)),
                  const((1, LANES)), const((1, LANES)), const((4, DH_A)), const((1, LANES))],
        out_specs=pl.BlockSpec((None, s, DV_A), lambda bi, hi: (bi, 0, hi)),
        out_shape=jax.ShapeDtypeStruct((b, s, h_a * DV_A), F32),
        scratch_shapes=[pltpu.VMEM((s, LANES), BF16)] * 4,
        compiler_params=pltpu.CompilerParams(
            dimension_semantics=("arbitrary", "arbitrary"), vmem_limit_bytes=VMEM_LIMIT),
        name="diff_attn",
    )(ua3, ua3, ua3, cosf, sina, sinb, gq, gk, lvec, sg)


def _rwkv_body(u_ref, mu_ref, w0_ref, wlu_ref, a0_ref, alu_ref, glu_ref, kkw_ref, ka_ref, rk_ref,
               lg_ref, lb_ref, bd_ref, tri_ref, o_ref, prev_s, st_s):
    c_len = u_ref.shape[0]
    d_r = o_ref.shape[1]
    n_pair = d_r // LANES

    @pl.when(pl.program_id(1) == 0)
    def _():
        prev_s[...] = jnp.zeros_like(prev_s)
        st_s[...] = jnp.zeros_like(st_s)

    u = u_ref[...]
    row = lax.broadcasted_iota(I32, (c_len, 1), 0)
    u_prev = jnp.where(row == 0, prev_s[...], pltpu.roll(u, 1, 0))
    prev_s[...] = u[c_len - 1:c_len, :]
    us = u + (u_prev - u) * mu_ref[...]

    r = us[:, 0:d_r]
    k = us[:, d_r:2 * d_r]
    v = us[:, 2 * d_r:3 * d_r]
    lo_in = us[:, 3 * d_r:3 * d_r + LANES]
    g_in = us[:, 3 * d_r + LANES:3 * d_r + 2 * LANES]

    bd = bd_ref[...]

    def head_sum(x):
        hi, lo = _split(x)
        return _dot(hi, bd) + _dot(lo, bd)

    z = -(w0_ref[...] + _bdot(jnp.tanh(lo_in), wlu_ref[...]))
    softplus = jnp.maximum(z, 0.0) + jnp.log(1.0 + jnp.exp(-jnp.abs(z)))
    logw = -jnp.exp(-softplus - 0.5)
    a = 1.0 / (1.0 + jnp.exp(-(a0_ref[...] + _bdot(lo_in, alu_ref[...]))))
    g = _bdot(1.0 / (1.0 + jnp.exp(-g_in)), glu_ref[...])
    kk_raw = k * kkw_ref[...]
    kk = kk_raw / jnp.maximum(jnp.sqrt(head_sum(kk_raw * kk_raw)), 1e-12)
    k_mod = k * (1.0 + (a - 1.0) * ka_ref[...])

    tri = tri_ref[...]
    w_hi, w_lo = _split(logw)
    cum = _dot(tri, w_hi) + _dot(tri, w_lo)
    cum_last = cum[c_len - 1:c_len, :]
    e_neg = jnp.exp(-cum)
    e_rem = jnp.exp(cum_last - cum)
    a_t = -kk * jnp.exp(cum - logw)
    r_t = r * jnp.exp(cum)
    b_raw = kk * a
    b_t = b_raw * e_neg
    k_t = k_mod * e_neg
    b_2 = b_raw * e_rem
    k_2 = k_mod * e_rem
    p_c = jnp.exp(cum_last)

    lane = lax.broadcasted_iota(I32, (1, LANES), 1)
    head0 = lane < RWKV_HEAD
    two_c = 2 * c_len
    rr = lax.broadcasted_iota(I32, (two_c, two_c), 0)
    cc = lax.broadcasted_iota(I32, (two_c, two_c), 1)
    t_i = rr & (c_len - 1)
    s_i = cc & (c_len - 1)
    strict = t_i > s_i
    incl = t_i >= s_i
    eye = rr == cc

    def stack(x, j):
        xp = x[:, j * LANES:(j + 1) * LANES]
        return jnp.concatenate([jnp.where(head0, xp, 0.0), jnp.where(head0, 0.0, xp)], axis=0)

    n_double = int(math.log2(c_len))
    y_pairs = []
    for j in range(n_pair):
        a_s, r_s, b_s, k_s, v_s = (stack(t, j) for t in (a_t, r_t, b_t, k_t, v))
        gram = _dot_nt(jnp.concatenate([a_s, r_s], axis=0).astype(BF16),
                       jnp.concatenate([b_s, k_s], axis=0).astype(BF16))
        n_ab = jnp.where(strict, gram[:two_c, :two_c], 0.0)
        a_ak = jnp.where(strict, gram[:two_c, two_c:], 0.0)
        m_rb = jnp.where(incl, gram[two_c:, :two_c], 0.0)
        m_rk = jnp.where(incl, gram[two_c:, two_c:], 0.0)
        vv = _bdot(jnp.concatenate([a_ak, m_rk], axis=0), v_s)
        x = jnp.concatenate([a_s, vv[:two_c]], axis=1)
        p = n_ab
        for it in range(n_double):
            x = x + _bdot(p, x)
            if it + 1 < n_double:
                p = _bdot(p, p)
        mx = _bdot(m_rb, x)
        r_h = r_s + mx[:, :two_c]
        y_h = mx[:, two_c:] + vv[two_c:]
        bx = _bdot(stack(b_2, j).T, x)
        g_m = jnp.where(eye, p_c[:, j * LANES:(j + 1) * LANES], 0.0) + bx[:, :two_c]
        h_m = bx[:, two_c:] + _bdot(stack(k_2, j).T, v_s)
        st = st_s[j].astype(BF16)
        y_s = _dot(r_h.astype(BF16), st) + y_h
        st_s[j] = _dot(g_m.astype(BF16), st) + h_m
        y_pairs.append(y_s[:c_len] + y_s[c_len:])
    y = jnp.concatenate(y_pairs, axis=1)

    inv_n = 1.0 / RWKV_HEAD
    mu = head_sum(y) * inv_n
    dlt = y - mu
    var = head_sum(dlt * dlt) * inv_n
    yn = dlt * lax.rsqrt(var + LN_X_EPS) * lg_ref[...] + lb_ref[...]
    bonus = head_sum(r * k_mod * rk_ref[...]) * v
    o_ref[...] = (yn + bonus) * g


def _rwkv7(ub3, mu, w0, wlu, a0, alu, glu, kkw, ka, rk, lg, lb, bd, tri):
    b, s, cols = ub3.shape
    d_r = w0.shape[1]
    c_len = tri.shape[0]
    const = lambda a: pl.BlockSpec(a.shape, lambda bi, ci: (0,) * a.ndim)
    params = (mu, w0, wlu, a0, alu, glu, kkw, ka, rk, lg, lb, bd, tri)
    return pl.pallas_call(
        _rwkv_body,
        grid=(b, s // c_len),
        in_specs=[pl.BlockSpec((None, c_len, cols), lambda bi, ci: (bi, ci, 0))]
                 + [const(a) for a in params],
        out_specs=pl.BlockSpec((None, c_len, d_r), lambda bi, ci: (bi, ci, 0)),
        out_shape=jax.ShapeDtypeStruct((b, s, d_r), F32),
        scratch_shapes=[pltpu.VMEM((1, cols), F32),
                        pltpu.VMEM((d_r // LANES, LANES, LANES), F32)],
        compiler_params=pltpu.CompilerParams(
            dimension_semantics=("arbitrary", "arbitrary"), vmem_limit_bytes=VMEM_LIMIT),
        name="rwkv7",
    )(ub3, *params)


def _out_body(x_ref, oa_ref, ob_ref, wa_ref, wb_ref, g2_ref, wrh_ref, wrl_ref, br_ref, tril_ref,
              h_ref, t_ref, eid_ref, gate_ref, rank_ref, cnt_ref, run_s):
    tm = x_ref.shape[0]

    @pl.when(pl.program_id(0) == 0)
    def _():
        run_s[...] = jnp.zeros_like(run_s)

    h = x_ref[...] + _bdot(oa_ref[...], wa_ref[...]) + _bdot(ob_ref[...], wb_ref[...])
    h_ref[...] = h
    t = h * lax.rsqrt(jnp.mean(h * h, axis=-1, keepdims=True) + NORM_EPS) * g2_ref[...]
    t_ref[...] = t

    t_hi, t_lo = _split(t)
    wrh = wrh_ref[...]
    lg = _dot(t_hi, wrh) + _dot(t_lo, wrh) + _dot(t_hi, wrl_ref[...]) + br_ref[...]

    lane_i = lax.broadcasted_iota(I32, (1, LANES), 1)
    lane = lane_i.astype(F32)
    big = float(LANES)

    def first_index(mask):
        return jnp.min(jnp.where(mask, lane, big), axis=-1, keepdims=True)

    def masked_softmax(mask):
        m = jnp.max(jnp.where(mask, lg, _NEG), axis=-1, keepdims=True)
        e = jnp.where(mask, jnp.exp(lg - m), 0.0)
        return e / jnp.sum(e, axis=-1, keepdims=True)

    is_g = lane_i < N_GROUPS
    gprob = masked_softmax(is_g)
    p_group = jnp.max(gprob, axis=-1, keepdims=True)
    gsel = first_index(is_g & (gprob == p_group))
    base = N_GROUPS + gsel * EXPERTS_PER_GROUP
    in_e = (lane >= base) & (lane < base + EXPERTS_PER_GROUP)
    eprob = masked_softmax(in_e)
    p1 = jnp.max(eprob, axis=-1, keepdims=True)
    i1 = first_index(in_e & (eprob == p1))
    rest = in_e & (lane != i1)
    p2 = jnp.max(jnp.where(rest, eprob, -1.0), axis=-1, keepdims=True)
    i2 = first_index(rest & (eprob == p2))
    den = p1 + p2
    g1 = p_group * p1 / den
    g2 = p_group * p2 / den
    e1 = i1 - N_GROUPS
    e2 = i2 - N_GROUPS

    oh1 = (lane == e1).astype(F32)
    oh2 = (lane == e2).astype(F32)
    tril = tril_ref[...]
    before1 = _dot(tril, oh1.astype(BF16))
    tot1 = jnp.sum(oh1, axis=0, keepdims=True)
    before2 = _dot(tril, oh2.astype(BF16)) + tot1
    run = run_s[...]
    rk1 = jnp.sum(oh1 * (before1 + run), axis=-1, keepdims=True)
    rk2 = jnp.sum(oh2 * (before2 + run), axis=-1, keepdims=True)
    run = run + tot1 + jnp.sum(oh2, axis=0, keepdims=True)
    run_s[...] = run
    cnt_ref[...] = run

    sel0 = lane_i == 0
    sel1 = lane_i == 1
    eid_ref[...] = jnp.where(sel0, e1, jnp.where(sel1, e2, 0.0)).astype(I32)
    rank_ref[...] = jnp.where(sel0, rk1, jnp.where(sel1, rk2, 0.0)).astype(I32)
    gate_ref[...] = jnp.where(sel0, g1, jnp.where(sel1, g2, 0.0))


def _out_proj_route(x2, oa, ob, wa, wb, g2, wrh, wrl, br, tril):
    n, d = x2.shape
    tm = tril.shape[0]
    da, db = oa.shape[1], ob.shape[1]
    row = lambda w: pl.BlockSpec((tm, w), lambda i: (i, 0))
    const = lambda a: pl.BlockSpec(a.shape, lambda i: (0,) * a.ndim)
    return pl.pallas_call(
        _out_body,
        grid=(n // tm,),
        in_specs=[row(d), row(da), row(db), const(wa), const(wb), const(g2), const(wrh), const(wrl),
                  const(br), const(tril)],
        out_specs=[row(d), row(d), row(LANES), row(LANES), row(LANES),
                   pl.BlockSpec((1, LANES), lambda i: (0, 0))],
        out_shape=[jax.ShapeDtypeStruct((n, d), F32), jax.ShapeDtypeStruct((n, d), F32),
                   jax.ShapeDtypeStruct((n, LANES), I32), jax.ShapeDtypeStruct((n, LANES), F32),
                   jax.ShapeDtypeStruct((n, LANES), I32), jax.ShapeDtypeStruct((1, LANES), F32)],
        scratch_shapes=[pltpu.VMEM((1, LANES), F32)],
        compiler_params=pltpu.CompilerParams(
            dimension_semantics=("arbitrary",), vmem_limit_bytes=VMEM_LIMIT),
        name="out_proj_route",
    )(x2, oa, ob, wa, wb, g2, wrh, wrl, br, tril)


def _dispatch_body(d0_ref, d1_ref, t_ref, xs_ref, sem):
    tm = d0_ref.shape[0]

    def copy(i, d_ref):
        return pltpu.make_async_copy(t_ref.at[pl.ds(i, 1)], xs_ref.at[pl.ds(d_ref[i], 1)], sem)

    def start(i, carry):
        copy(i, d0_ref).start()
        copy(i, d1_ref).start()
        return carry

    def wait(i, carry):
        copy(i, d0_ref).wait()
        copy(i, d1_ref).wait()
        return carry

    lax.fori_loop(0, tm, start, 0, unroll=8)
    lax.fori_loop(0, tm, wait, 0, unroll=8)


def _dispatch(d0, d1, t):
    n, d = t.shape
    tm = min(512, n)
    smem = pl.BlockSpec((tm,), lambda i: (i,), memory_space=pltpu.SMEM)
    return pl.pallas_call(
        _dispatch_body,
        grid=(n // tm,),
        in_specs=[smem, smem, pl.BlockSpec((tm, d), lambda i: (i, 0))],
        out_specs=pl.BlockSpec(memory_space=pl.ANY),
        out_shape=jax.ShapeDtypeStruct((2 * n, d), t.dtype),
        scratch_shapes=[pltpu.SemaphoreType.DMA],
        compiler_params=pltpu.CompilerParams(dimension_semantics=("arbitrary",)),
        name="moe_dispatch",
    )(d0, d1, t)


def _experts_body(ub_ref, ue_ref, gs_ref, nu_ref, x_ref, wg_ref, wu_ref, wd_ref, y_ref):
    w = pl.program_id(0)
    bm = x_ref.shape[0]

    @pl.when(w < nu_ref[0])
    def _():
        e = ue_ref[w]
        blk = ub_ref[w]
        rows = blk * bm + lax.broadcasted_iota(I32, (bm, 1), 0)
        valid = (rows >= gs_ref[e]) & (rows < gs_ref[e + 1])
        xb = x_ref[...].astype(BF16)
        gt = _dot(xb, wg_ref[...])
        up = _dot(xb, wu_ref[...])
        act = (gt / (1.0 + jnp.exp(-gt))) * up
        y = jnp.where(valid, _dot(act.astype(BF16), wd_ref[...]), 0.0)
        first = jnp.logical_or(w == 0, ub_ref[jnp.maximum(w - 1, 0)] != blk)

        @pl.when(first)
        def _():
            y_ref[...] = y

        @pl.when(jnp.logical_not(first))
        def _():
            y_ref[...] += y


def _experts(unit_blk, unit_exp, gstart, n_units, xs, wg, wu, wd, bm):
    m, d = xs.shape
    de = wg.shape[2]
    max_units = unit_blk.shape[0]
    grid_spec = pltpu.PrefetchScalarGridSpec(
        num_scalar_prefetch=4,
        grid=(max_units,),
        in_specs=[
            pl.BlockSpec((bm, d), lambda w, ub, ue, gs, nu: (ub[w], 0)),
            pl.BlockSpec((None, d, de), lambda w, ub, ue, gs, nu: (ue[w], 0, 0)),
            pl.BlockSpec((None, d, de), lambda w, ub, ue, gs, nu: (ue[w], 0, 0)),
            pl.BlockSpec((None, de, d), lambda w, ub, ue, gs, nu: (ue[w], 0, 0)),
        ],
        out_specs=pl.BlockSpec((bm, d), lambda w, ub, ue, gs, nu: (ub[w], 0)),
    )
    return pl.pallas_call(
        _experts_body,
        grid_spec=grid_spec,
        out_shape=jax.ShapeDtypeStruct((m, d), F32),
        compiler_params=pltpu.CompilerParams(
            dimension_semantics=("arbitrary",), vmem_limit_bytes=VMEM_LIMIT),
        name="moe_experts",
    )(unit_blk, unit_exp, gstart, n_units, xs, wg, wu, wd)


def _combine_body(d0_ref, d1_ref, y_ref, h_ref, gate_ref, o_ref, b0, b1, sem):
    tm = d0_ref.shape[0]

    def copy(i, d_ref, buf):
        return pltpu.make_async_copy(y_ref.at[pl.ds(d_ref[i], 1)], buf.at[pl.ds(i, 1)], sem)

    def start(i, carry):
        copy(i, d0_ref, b0).start()
        copy(i, d1_ref, b1).start()
        return carry

    def wait(i, carry):
        copy(i, d0_ref, b0).wait()
        copy(i, d1_ref, b1).wait()
        return carry

    lax.fori_loop(0, tm, start, 0, unroll=8)
    lax.fori_loop(0, tm, wait, 0, unroll=8)
    gate = gate_ref[...]
    o_ref[...] = h_ref[...] + b0[...] * gate[:, 0:1] + b1[...] * gate[:, 1:2]


def _combine(d0, d1, y, h, gate):
    n, d = h.shape
    tm = min(256, n)
    smem = pl.BlockSpec((tm,), lambda i: (i,), memory_space=pltpu.SMEM)
    return pl.pallas_call(
        _combine_body,
        grid=(n // tm,),
        in_specs=[smem, smem, pl.BlockSpec(memory_space=pl.ANY),
                  pl.BlockSpec((tm, d), lambda i: (i, 0)),
                  pl.BlockSpec((tm, LANES), lambda i: (i, 0))],
        out_specs=pl.BlockSpec((tm, d), lambda i: (i, 0)),
        out_shape=jax.ShapeDtypeStruct((n, d), F32),
        scratch_shapes=[pltpu.VMEM((tm, d), F32), pltpu.VMEM((tm, d), F32), pltpu.SemaphoreType.DMA],
        compiler_params=pltpu.CompilerParams(
            dimension_semantics=("arbitrary",), vmem_limit_bytes=VMEM_LIMIT),
        name="moe_combine",
    )(d0, d1, y, h, gate)


def _rope_tables(seq):
    half = ROT_DIM // 2
    inv = ROPE_THETA ** (-jnp.arange(0, ROT_DIM, 2, dtype=F32) / ROT_DIM)
    ang = jnp.arange(seq, dtype=F32)[:, None] * inv[None, :]
    cos, sin = jnp.cos(ang), jnp.sin(ang)
    ones = jnp.ones((seq, DH_A - ROT_DIM), F32)
    zeros = jnp.zeros((seq, DH_A - half), F32)
    zeros_h = jnp.zeros((seq, half), F32)
    zeros_r = jnp.zeros((seq, DH_A - ROT_DIM), F32)
    cos_c = jnp.concatenate([cos, cos, ones], axis=1)
    sa_c = jnp.concatenate([-sin, zeros], axis=1)
    sb_c = jnp.concatenate([zeros_h, sin, zeros_r], axis=1)
    tile = lambda t: jnp.concatenate([t, t], axis=1)
    return tile(cos_c), tile(sa_c), tile(sb_c)


def _layer(x, l, lambda_init, ln1_g, w_in, q_norm_g, k_norm_g, lambda_q1, lambda_k1, lambda_q2,
           lambda_k2, subln_g, shift_mu, w0, w_lora_up, a0, a_lora_up, g_lora_up, k_k, k_a, r_k,
           ln_x_g, ln_x_b, w_out, ln2_g, w_router_group, b_router_group, w_router_expert,
           b_router_expert, w_exp_gate, w_exp_up, w_exp_down):
    b, s, d = x.shape
    n = b * s
    d_attn3 = 3 * (d // 2)
    d_attn = d // 2
    d_r = w0.shape[1]
    x2 = x.reshape(n, d)
    row = lambda a: a[l].reshape(1, -1)

    w_in_b = w_in[l].astype(BF16)
    ua, ub = _in_proj(x2, row(ln1_g), w_in_b[:, :d_attn3], w_in_b[:, d_attn3:])

    cosf, sina, sinb = _rope_tables(s)
    tile2 = lambda a: jnp.concatenate([a[l], a[l]]).reshape(1, -1)
    lvec = jnp.stack([lambda_q1[l], lambda_k1[l], lambda_q2[l], lambda_k2[l]])
    o_a = _diff_attn(ua.reshape(b, s, d_attn3), cosf, sina, sinb, tile2(q_norm_g), tile2(k_norm_g),
                     lvec, row(subln_g), lambda_init)

    dl = w_lora_up.shape[1]
    zeros_l = jnp.zeros((dl, d_r), F32)
    wlu = jnp.concatenate([w_lora_up[l], zeros_l], axis=0).astype(BF16)
    alu = jnp.concatenate([zeros_l, a_lora_up[l]], axis=0).astype(BF16)
    hid = jnp.arange(d_r) // RWKV_HEAD
    bd = (hid[:, None] == hid[None, :]).astype(BF16)
    c_len = min(CHUNK, s)
    tri = (jnp.arange(c_len)[:, None] >= jnp.arange(c_len)[None, :]).astype(BF16)
    o_b = _rwkv7(ub.reshape(b, s, -1), row(shift_mu), row(w0), wlu, row(a0), alu,
                 g_lora_up[l].astype(BF16), row(k_k), row(k_a), row(r_k), row(ln_x_g), row(ln_x_b),
                 bd, tri)

    w_out_b = w_out[l].astype(BF16)
    wr = jnp.concatenate([w_router_group[l], w_router_expert[l],
                          jnp.zeros((d, LANES - N_GROUPS - N_EXPERTS), F32)], axis=1)
    wr_hi = wr.astype(BF16)
    wr_lo = (wr - wr_hi.astype(F32)).astype(BF16)
    br = jnp.concatenate([b_router_group[l], b_router_expert[l],
                          jnp.zeros((LANES - N_GROUPS - N_EXPERTS,), F32)]).reshape(1, LANES)
    tm_r = min(512, n)
    tril = (jnp.arange(tm_r)[:, None] > jnp.arange(tm_r)[None, :]).astype(BF16)
    h, t, eid, gate, rank, cnt = _out_proj_route(
        x2, o_a.reshape(n, d_attn), o_b.reshape(n, d_r), w_out_b[:d_attn], w_out_b[d_attn:],
        row(ln2_g), wr_hi, wr_lo, br, tril)

    counts = cnt[0, :N_EXPERTS].astype(I32)
    gend = jnp.cumsum(counts)
    gstart = jnp.concatenate([jnp.zeros((1,), I32), gend]).astype(I32)
    onehot = (eid[:, :2, None] == jnp.arange(N_EXPERTS)[None, None, :])
    dest = jnp.sum(jnp.where(onehot, gstart[:N_EXPERTS], 0), axis=-1) + rank[:, :2]
    d0 = dest[:, 0].astype(I32)
    d1 = dest[:, 1].astype(I32)

    m = 2 * n
    bm = min(256, m)
    n_blk = m // bm
    max_units = n_blk + N_EXPERTS - 1
    first_blk = gstart[:N_EXPERTS] // bm
    last_blk = jnp.where(counts > 0, (gend - 1) // bm, first_blk)
    units_e = jnp.where(counts > 0, last_blk - first_blk + 1, 0)
    unit_end = jnp.cumsum(units_e)
    unit_start = unit_end - units_e
    n_units = unit_end[-1]
    w_ids = jnp.arange(max_units)
    u_exp = jnp.minimum(jnp.sum((unit_end[None, :] <= w_ids[:, None]).astype(I32), axis=1),
                        N_EXPERTS - 1)
    u_blk = first_blk[u_exp] + (w_ids - unit_start[u_exp])
    last = jnp.maximum(n_units - 1, 0)
    pad = w_ids >= n_units
    u_exp = jnp.where(pad, u_exp[last], u_exp).astype(I32)
    u_blk = jnp.where(pad, u_blk[last], u_blk).astype(I32)

    xs = _dispatch(d0, d1, t)
    y = _experts(u_blk, u_exp, gstart, n_units.reshape(1).astype(I32), xs,
                 w_exp_gate[l].astype(BF16), w_exp_up[l].astype(BF16), w_exp_down[l].astype(BF16), bm)
    out = _combine(d0, d1, y, h, gate)
    return out.reshape(b, s, d)


def kernel(x, ln1_g, w_in, q_norm_g, k_norm_g, lambda_q1, lambda_k1, lambda_q2, lambda_k2, subln_g, shift_mu, w0, w_lora_up, a0, a_lora_up, g_lora_up, k_k, k_a, r_k, ln_x_g, ln_x_b, w_out, ln2_g, w_router_group, b_router_group, w_router_expert, b_router_expert, w_exp_gate, w_exp_up, w_exp_down):
    h = x
    for l in range(ln1_g.shape[0]):
        lambda_init = 0.8 - 0.6 * math.exp(-0.3 * l)
        h = _layer(h, l, lambda_init, ln1_g, w_in, q_norm_g, k_norm_g, lambda_q1, lambda_k1,
                   lambda_q2, lambda_k2, subln_g, shift_mu, w0, w_lora_up, a0, a_lora_up, g_lora_up,
                   k_k, k_a, r_k, ln_x_g, ln_x_b, w_out, ln2_g, w_router_group, b_router_group,
                   w_router_expert, b_router_expert, w_exp_gate, w_exp_up, w_exp_down)
    return h
```

```python
import functools
import math

import jax
import jax.numpy as jnp
from jax import lax
from jax.experimental import pallas as pl
from jax.experimental.pallas import tpu as pltpu

F32 = jnp.float32
BF16 = jnp.bfloat16
I32 = jnp.int32

CHUNK = 64
DH_A = 64
DV_A = 128
ROT_DIM = 16
ROPE_THETA = 500000.0
QK_NORM_EPS = 1e-6
SUBLN_EPS = 1e-5
RWKV_HEAD = 64
LN_X_EPS = 64e-5
RWKV_ROWS = 256
N_GROUPS = 4
EXPERTS_PER_GROUP = 8
N_EXPERTS = 32
NORM_EPS = 1e-6

LANES = 128
VMEM_LIMIT = 56 * 1024 * 1024

_NEG = -1e30


def _dot(a, b):
    return jnp.dot(a, b, preferred_element_type=F32)


def _dot_nt(a, b):
    return lax.dot_general(a, b, (((1,), (1,)), ((), ())), preferred_element_type=F32)


def _bdot(a, b):
    return _dot(a.astype(BF16), b.astype(BF16))


def _split(x):
    hi = x.astype(BF16)
    lo = (x - hi.astype(F32)).astype(BF16)
    return hi, lo


def _in_proj_body(x_ref, g_ref, wa_ref, wb_ref, ua_ref, ub_ref):
    x = x_ref[...]
    ms = jnp.mean(x * x, axis=-1, keepdims=True)
    xn = (x * lax.rsqrt(ms + NORM_EPS) * g_ref[...]).astype(BF16)
    ua_ref[...] = _dot(xn, wa_ref[...])
    ub_ref[...] = _dot(xn, wb_ref[...])


def _in_proj(x2, g, wa, wb):
    n, d = x2.shape
    tm = min(512, n)
    na, nb = wa.shape[1], wb.shape[1]
    return pl.pallas_call(
        _in_proj_body,
        grid=(n // tm,),
        in_specs=[
            pl.BlockSpec((tm, d), lambda i: (i, 0)),
            pl.BlockSpec((1, d), lambda i: (0, 0)),
            pl.BlockSpec((d, na), lambda i: (0, 0)),
            pl.BlockSpec((d, nb), lambda i: (0, 0)),
        ],
        out_specs=[
            pl.BlockSpec((tm, na), lambda i: (i, 0)),
            pl.BlockSpec((tm, nb), lambda i: (i, 0)),
        ],
        out_shape=[jax.ShapeDtypeStruct((n, na), F32), jax.ShapeDtypeStruct((n, nb), F32)],
        compiler_params=pltpu.CompilerParams(
            dimension_semantics=("arbitrary",), vmem_limit_bytes=VMEM_LIMIT),
        name="in_proj",
    )(x2, g, wa, wb)


def _attn_body(q_ref, k_ref, v_ref, cos_ref, sa_ref, sb_ref, gq_ref, gk_ref, l_ref, sg_ref,
               o_ref, q0_s, q1_s, k_s, v_s, *, lambda_init, bq):
    s_len = q_ref.shape[0]
    lane = lax.broadcasted_iota(I32, (1, LANES), 1)
    lo = lane < DH_A

    def prep(t, g):
        t2 = t * t
        s_lo = jnp.sum(jnp.where(lo, t2, 0.0), axis=-1, keepdims=True)
        s_hi = jnp.sum(jnp.where(lo, 0.0, t2), axis=-1, keepdims=True)
        ms = jnp.where(lo, s_lo, s_hi) * (1.0 / DH_A)
        tn = t * lax.rsqrt(ms + QK_NORM_EPS) * g
        return (tn * cos_ref[...] + pltpu.roll(tn, LANES - ROT_DIM // 2, 1) * sa_ref[...]
                + pltpu.roll(tn, ROT_DIM // 2, 1) * sb_ref[...])

    q = prep(q_ref[...], gq_ref[...]) * (DH_A ** -0.5)
    q0_s[...] = jnp.where(lo, q, 0.0).astype(BF16)
    q1_s[...] = jnp.where(lo, 0.0, q).astype(BF16)
    k_s[...] = prep(k_ref[...], gk_ref[...]).astype(BF16)
    v_s[...] = v_ref[...].astype(BF16)

    l = l_ref[...]
    lam = (jnp.exp(jnp.sum(l[0:1] * l[1:2], axis=-1, keepdims=True))
           - jnp.exp(jnp.sum(l[2:3] * l[3:4], axis=-1, keepdims=True)) + lambda_init)

    chunk_shift = int(math.log2(CHUNK))
    rr = lax.broadcasted_iota(I32, (bq, bq), 0) >> chunk_shift
    cc = lax.broadcasted_iota(I32, (bq, bq), 1) >> chunk_shift
    diag_ok = cc <= rr

    for i in range(s_len // bq):
        r0 = i * bq
        e_diag, e_off, l_sum = [], [], []
        for q_s in (q0_s, q1_s):
            qb = q_s[r0:r0 + bq, :]
            sd = jnp.where(diag_ok, _dot_nt(qb, k_s[r0:r0 + bq, :]), _NEG)
            m = jnp.max(sd, axis=-1, keepdims=True)
            if r0 > 0:
                so = _dot_nt(qb, k_s[0:r0, :])
                m = jnp.maximum(m, jnp.max(so, axis=-1, keepdims=True))
            ed = jnp.exp(sd - m)
            ls = jnp.sum(ed, axis=-1, keepdims=True)
            if r0 > 0:
                eo = jnp.exp(so - m)
                ls = ls + jnp.sum(eo, axis=-1, keepdims=True)
                e_off.append(eo)
            e_diag.append(ed)
            l_sum.append(ls)
        w0 = 1.0 / l_sum[0]
        w1 = lam / l_sum[1]
        pd = (e_diag[0] * w0 - e_diag[1] * w1).astype(BF16)
        o = _dot(pd, v_s[r0:r0 + bq, :])
        if r0 > 0:
            po = (e_off[0] * w0 - e_off[1] * w1).astype(BF16)
            o = o + _dot(po, v_s[0:r0, :])
        o = o * lax.rsqrt(jnp.mean(o * o, axis=-1, keepdims=True) + SUBLN_EPS)
        o_ref[r0:r0 + bq, :] = o * (sg_ref[...] * (1.0 - lambda_init))


def _diff_attn(ua3, cosf, sina, sinb, gq, gk, lvec, sg, lambda_init):
    b, s, _ = ua3.shape
    h_a = ua3.shape[2] // (3 * DV_A)
    bq = min(256, s)
    blk = lambda off: pl.BlockSpec((None, s, DV_A), lambda bi, hi: (bi, 0, off + hi))
    const = lambda shape: pl.BlockSpec(shape, lambda bi, hi: (0,) * len(shape))
    return pl.pallas_call(
        functools.partial(_attn_body, lambda_init=lambda_init, bq=bq),
        grid=(b, h_a),
        in_specs=[blk(0), blk(h_a), blk(2 * h_a),
                  const((s, LANES)), const((s, LANES)), const((s, LANES)),
                  const((1, LANES)), const((1, LANES)), const((4, DH_A)), const((1, LANES))],
        out_specs=pl.BlockSpec((None, s, DV_A), lambda bi, hi: (bi, 0, hi)),
        out_shape=jax.ShapeDtypeStruct((b, s, h_a * DV_A), F32),
        scratch_shapes=[pltpu.VMEM((s, LANES), BF16)] * 4,
        compiler_params=pltpu.CompilerParams(
            dimension_semantics=("arbitrary", "arbitrary"), vmem_limit_bytes=VMEM_LIMIT),
        name="diff_attn",
    )(ua3, ua3, ua3, cosf, sina, sinb, gq, gk, lvec, sg)


def _rwkv_body(u_ref, mu_ref, w0_ref, wlu_ref, a0_ref, alu_ref, glu_ref, kkw_ref, ka_ref, rk_ref,
               lg_ref, lb_ref, bd_ref, tri_ref, o_ref, prev_s, st_s, *, c_len):
    tr = u_ref.shape[0]
    d_r = o_ref.shape[1]
    n_pair = d_r // LANES
    n_ch = tr // c_len

    @pl.when(pl.program_id(1) == 0)
    def _():
        prev_s[...] = jnp.zeros_like(prev_s)
        st_s[...] = jnp.zeros_like(st_s)

    u = u_ref[...]
    row = lax.broadcasted_iota(I32, (tr, 1), 0)
    u_prev = jnp.where(row == 0, prev_s[...], pltpu.roll(u, 1, 0))
    prev_s[...] = u[tr - 1:tr, :]
    us = u + (u_prev - u) * mu_ref[...]

    r = us[:, 0:d_r]
    k = us[:, d_r:2 * d_r]
    v = us[:, 2 * d_r:3 * d_r]
    lo_in = us[:, 3 * d_r:3 * d_r + LANES]
    g_in = us[:, 3 * d_r + LANES:3 * d_r + 2 * LANES]

    bd = bd_ref[...]

    def head_sum(x):
        hi, lo = _split(x)
        return _dot(hi, bd) + _dot(lo, bd)

    z = -(w0_ref[...] + _bdot(jnp.tanh(lo_in), wlu_ref[...]))
    softplus = jnp.maximum(z, 0.0) + jnp.log(1.0 + jnp.exp(-jnp.abs(z)))
    logw = -jnp.exp(-softplus - 0.5)
    a = 1.0 / (1.0 + jnp.exp(-(a0_ref[...] + _bdot(lo_in, alu_ref[...]))))
    g = _bdot(1.0 / (1.0 + jnp.exp(-g_in)), glu_ref[...])
    kk_raw = k * kkw_ref[...]
    kk = kk_raw / jnp.maximum(jnp.sqrt(head_sum(kk_raw * kk_raw)), 1e-12)
    k_mod = k * (1.0 + (a - 1.0) * ka_ref[...])

    tri = tri_ref[...]
    w_hi, w_lo = _split(logw)
    cum = _dot(tri, w_hi) + _dot(tri, w_lo)
    last_rows = [cum[(c + 1) * c_len - 1:(c + 1) * c_len, :] for c in range(n_ch)]
    cum_last = jnp.concatenate([jnp.broadcast_to(lr, (c_len, d_r)) for lr in last_rows], axis=0)
    e_neg = jnp.exp(-cum)
    e_rem = jnp.exp(cum_last - cum)
    a_t = -kk * jnp.exp(cum - logw)
    r_t = r * jnp.exp(cum)
    b_raw = kk * a
    b_t = b_raw * e_neg
    k_t = k_mod * e_neg
    b_2 = b_raw * e_rem
    k_2 = k_mod * e_rem

    lane = lax.broadcasted_iota(I32, (1, LANES), 1)
    head0 = lane < RWKV_HEAD
    two_c = 2 * c_len
    rr = lax.broadcasted_iota(I32, (two_c, two_c), 0)
    cc = lax.broadcasted_iota(I32, (two_c, two_c), 1)
    t_i = rr & (c_len - 1)
    s_i = cc & (c_len - 1)
    strict = t_i > s_i
    incl = t_i >= s_i
    eye = rr == cc

    def stack(x, c, j):
        xp = x[c * c_len:(c + 1) * c_len, j * LANES:(j + 1) * LANES]
        return jnp.concatenate([jnp.where(head0, xp, 0.0), jnp.where(head0, 0.0, xp)], axis=0)

    units = [(c, j) for c in range(n_ch) for j in range(n_pair)]
    n_double = int(math.log2(c_len))

    a_s = [stack(a_t, c, j) for c, j in units]
    r_s = [stack(r_t, c, j) for c, j in units]
    v_b = [stack(v, c, j).astype(BF16) for c, j in units]
    gram = [_dot_nt(jnp.concatenate([a_s[i], r_s[i]], axis=0).astype(BF16),
                    jnp.concatenate([stack(b_t, c, j), stack(k_t, c, j)], axis=0).astype(BF16))
            for i, (c, j) in enumerate(units)]
    p = [jnp.where(strict, gm[:two_c, :two_c], 0.0) for gm in gram]
    m_rb = [jnp.where(incl, gm[two_c:, :two_c], 0.0).astype(BF16) for gm in gram]
    vv = [_dot(jnp.concatenate([jnp.where(strict, gm[:two_c, two_c:], 0.0),
                                jnp.where(incl, gm[two_c:, two_c:], 0.0)], axis=0).astype(BF16), vb)
          for gm, vb in zip(gram, v_b)]
    x = [jnp.concatenate([a_s[i], vv[i][:two_c]], axis=1) for i in range(len(units))]
    for it in range(n_double):
        p_b = [pi.astype(BF16) for pi in p]
        x = [xi + _dot(pb, xi.astype(BF16)) for xi, pb in zip(x, p_b)]
        if it + 1 < n_double:
            p = [_dot(pb, pb) for pb in p_b]
    x_b = [xi.astype(BF16) for xi in x]
    mx = [_dot(mb, xb) for mb, xb in zip(m_rb, x_b)]
    bx = [_dot(stack(b_2, c, j).T.astype(BF16), x_b[i]) for i, (c, j) in enumerate(units)]
    kv = [_dot(stack(k_2, c, j).T.astype(BF16), v_b[i]) for i, (c, j) in enumerate(units)]

    y_rows = []
    st = [st_s[j] for j in range(n_pair)]
    for c in range(n_ch):
        p_c = jnp.exp(last_rows[c])
        y_pairs = []
        for j in range(n_pair):
            i = c * n_pair + j
            r_h = (r_s[i] + mx[i][:, :two_c]).astype(BF16)
            y_h = mx[i][:, two_c:] + vv[i][two_c:]
            g_m = (jnp.where(eye, p_c[:, j * LANES:(j + 1) * LANES], 0.0) + bx[i][:, :two_c]).astype(BF16)
            h_m = bx[i][:, two_c:] + kv[i]
            st_b = st[j].astype(BF16)
            y_s = _dot(r_h, st_b) + y_h
            st[j] = _dot(g_m, st_b) + h_m
            y_pairs.append(y_s[:c_len] + y_s[c_len:])
        y_rows.append(jnp.concatenate(y_pairs, axis=1))
    for j in range(n_pair):
        st_s[j] = st[j]
    y = jnp.concatenate(y_rows, axis=0)

    inv_n = 1.0 / RWKV_HEAD
    mu = head_sum(y) * inv_n
    dlt = y - mu
    var = head_sum(dlt * dlt) * inv_n
    yn = dlt * lax.rsqrt(var + LN_X_EPS) * lg_ref[...] + lb_ref[...]
    bonus = head_sum(r * k_mod * rk_ref[...]) * v
    o_ref[...] = (yn + bonus) * g


def _rwkv7(ub3, mu, w0, wlu, a0, alu, glu, kkw, ka, rk, lg, lb, bd, tri, c_len):
    b, s, cols = ub3.shape
    d_r = w0.shape[1]
    tr = tri.shape[0]
    const = lambda a: pl.BlockSpec(a.shape, lambda bi, ci: (0,) * a.ndim)
    params = (mu, w0, wlu, a0, alu, glu, kkw, ka, rk, lg, lb, bd, tri)
    return pl.pallas_call(
        functools.partial(_rwkv_body, c_len=c_len),
        grid=(b, s // tr),
        in_specs=[pl.BlockSpec((None, tr, cols), lambda bi, ci: (bi, ci, 0))]
                 + [const(a) for a in params],
        out_specs=pl.BlockSpec((None, tr, d_r), lambda bi, ci: (bi, ci, 0)),
        out_shape=jax.ShapeDtypeStruct((b, s, d_r), F32),
        scratch_shapes=[pltpu.VMEM((1, cols), F32),
                        pltpu.VMEM((d_r // LANES, LANES, LANES), F32)],
        compiler_params=pltpu.CompilerParams(
            dimension_semantics=("arbitrary", "arbitrary"), vmem_limit_bytes=VMEM_LIMIT),
        name="rwkv7",
    )(ub3, *params)


def _out_body(x_ref, oa_ref, ob_ref, wa_ref, wb_ref, g2_ref, wrh_ref, wrl_ref, br_ref, tril_ref,
              h_ref, t_ref, eid_ref, gate_ref, rank_ref, cnt_ref, run_s):
    tm = x_ref.shape[0]

    @pl.when(pl.program_id(0) == 0)
    def _():
        run_s[...] = jnp.zeros_like(run_s)

    h = x_ref[...] + _bdot(oa_ref[...], wa_ref[...]) + _bdot(ob_ref[...], wb_ref[...])
    h_ref[...] = h
    t = h * lax.rsqrt(jnp.mean(h * h, axis=-1, keepdims=True) + NORM_EPS) * g2_ref[...]
    t_ref[...] = t

    t_hi, t_lo = _split(t)
    wrh = wrh_ref[...]
    lg = _dot(t_hi, wrh) + _dot(t_lo, wrh) + _dot(t_hi, wrl_ref[...]) + br_ref[...]

    lane_i = lax.broadcasted_iota(I32, (1, LANES), 1)
    lane = lane_i.astype(F32)
    big = float(LANES)

    def first_index(mask):
        return jnp.min(jnp.where(mask, lane, big), axis=-1, keepdims=True)

    def masked_softmax(mask):
        m = jnp.max(jnp.where(mask, lg, _NEG), axis=-1, keepdims=True)
        e = jnp.where(mask, jnp.exp(lg - m), 0.0)
        return e / jnp.sum(e, axis=-1, keepdims=True)

    is_g = lane_i < N_GROUPS
    gprob = masked_softmax(is_g)
    p_group = jnp.max(gprob, axis=-1, keepdims=True)
    gsel = first_index(is_g & (gprob == p_group))
    base = N_GROUPS + gsel * EXPERTS_PER_GROUP
    in_e = (lane >= base) & (lane < base + EXPERTS_PER_GROUP)
    eprob = masked_softmax(in_e)
    p1 = jnp.max(eprob, axis=-1, keepdims=True)
    i1 = first_index(in_e & (eprob == p1))
    rest = in_e & (lane != i1)
    p2 = jnp.max(jnp.where(rest, eprob, -1.0), axis=-1, keepdims=True)
    i2 = first_index(rest & (eprob == p2))
    den = p1 + p2
    g1 = p_group * p1 / den
    g2 = p_group * p2 / den
    e1 = i1 - N_GROUPS
    e2 = i2 - N_GROUPS

    oh1 = (lane == e1).astype(F32)
    oh2 = (lane == e2).astype(F32)
    tril = tril_ref[...]
    before1 = _dot(tril, oh1.astype(BF16))
    tot1 = jnp.sum(oh1, axis=0, keepdims=True)
    before2 = _dot(tril, oh2.astype(BF16)) + tot1
    run = run_s[...]
    rk1 = jnp.sum(oh1 * (before1 + run), axis=-1, keepdims=True)
    rk2 = jnp.sum(oh2 * (before2 + run), axis=-1, keepdims=True)
    run = run + tot1 + jnp.sum(oh2, axis=0, keepdims=True)
    run_s[...] = run
    cnt_ref[...] = run

    sel0 = lane_i == 0
    sel1 = lane_i == 1
    eid_ref[...] = jnp.where(sel0, e1, jnp.where(sel1, e2, 0.0)).astype(I32)
    rank_ref[...] = jnp.where(sel0, rk1, jnp.where(sel1, rk2, 0.0)).astype(I32)
    gate_ref[...] = jnp.where(sel0, g1, jnp.where(sel1, g2, 0.0))


def _out_proj_route(x2, oa, ob, wa, wb, g2, wrh, wrl, br, tril):
    n, d = x2.shape
    tm = tril.shape[0]
    da, db = oa.shape[1], ob.shape[1]
    row = lambda w: pl.BlockSpec((tm, w), lambda i: (i, 0))
    const = lambda a: pl.BlockSpec(a.shape, lambda i: (0,) * a.ndim)
    return pl.pallas_call(
        _out_body,
        grid=(n // tm,),
        in_specs=[row(d), row(da), row(db), const(wa), const(wb), const(g2), const(wrh), const(wrl),
                  const(br), const(tril)],
        out_specs=[row(d), row(d), row(LANES), row(LANES), row(LANES),
                   pl.BlockSpec((1, LANES), lambda i: (0, 0))],
        out_shape=[jax.ShapeDtypeStruct((n, d), F32), jax.ShapeDtypeStruct((n, d), F32),
                   jax.ShapeDtypeStruct((n, LANES), I32), jax.ShapeDtypeStruct((n, LANES), F32),
                   jax.ShapeDtypeStruct((n, LANES), I32), jax.ShapeDtypeStruct((1, LANES), F32)],
        scratch_shapes=[pltpu.VMEM((1, LANES), F32)],
        compiler_params=pltpu.CompilerParams(
            dimension_semantics=("arbitrary",), vmem_limit_bytes=VMEM_LIMIT),
        name="out_proj_route",
    )(x2, oa, ob, wa, wb, g2, wrh, wrl, br, tril)


def _dispatch_body(d0_ref, d1_ref, t_ref, xs_ref, sem):
    tm = d0_ref.shape[0]

    def copy(i, d_ref):
        return pltpu.make_async_copy(t_ref.at[pl.ds(i, 1)], xs_ref.at[pl.ds(d_ref[i], 1)], sem)

    def start(i, carry):
        copy(i, d0_ref).start(priority=0)
        copy(i, d1_ref).start(priority=1)
        return carry

    def wait(i, carry):
        copy(i, d0_ref).wait()
        copy(i, d1_ref).wait()
        return carry

    lax.fori_loop(0, tm, start, 0, unroll=8)
    lax.fori_loop(0, tm, wait, 0, unroll=8)


def _dispatch(d0, d1, t):
    n, d = t.shape
    tm = min(512, n)
    smem = pl.BlockSpec((tm,), lambda i: (i,), memory_space=pltpu.SMEM)
    return pl.pallas_call(
        _dispatch_body,
        grid=(n // tm,),
        in_specs=[smem, smem, pl.BlockSpec((tm, d), lambda i: (i, 0))],
        out_specs=pl.BlockSpec(memory_space=pl.ANY),
        out_shape=jax.ShapeDtypeStruct((2 * n, d), t.dtype),
        scratch_shapes=[pltpu.SemaphoreType.DMA],
        compiler_params=pltpu.CompilerParams(dimension_semantics=("arbitrary",)),
        name="moe_dispatch",
    )(d0, d1, t)


def _experts_body(ub_ref, ue_ref, gs_ref, nu_ref, x_ref, wg_ref, wu_ref, wd_ref, y_ref):
    w = pl.program_id(0)
    bm = x_ref.shape[0]

    @pl.when(w < nu_ref[0])
    def _():
        e = ue_ref[w]
        blk = ub_ref[w]
        rows = blk * bm + lax.broadcasted_iota(I32, (bm, 1), 0)
        valid = (rows >= gs_ref[e]) & (rows < gs_ref[e + 1])
        xb = x_ref[...].astype(BF16)
        gt = _dot(xb, wg_ref[...])
        up = _dot(xb, wu_ref[...])
        act = (gt / (1.0 + jnp.exp(-gt))) * up
        y = jnp.where(valid, _dot(act.astype(BF16), wd_ref[...]), 0.0)
        first = jnp.logical_or(w == 0, ub_ref[jnp.maximum(w - 1, 0)] != blk)

        @pl.when(first)
        def _():
            y_ref[...] = y

        @pl.when(jnp.logical_not(first))
        def _():
            y_ref[...] += y


def _experts(unit_blk, unit_exp, gstart, n_units, xs, wg, wu, wd, bm):
    m, d = xs.shape
    de = wg.shape[2]
    max_units = unit_blk.shape[0]
    grid_spec = pltpu.PrefetchScalarGridSpec(
        num_scalar_prefetch=4,
        grid=(max_units,),
        in_specs=[
            pl.BlockSpec((bm, d), lambda w, ub, ue, gs, nu: (ub[w], 0)),
            pl.BlockSpec((None, d, de), lambda w, ub, ue, gs, nu: (ue[w], 0, 0)),
            pl.BlockSpec((None, d, de), lambda w, ub, ue, gs, nu: (ue[w], 0, 0)),
            pl.BlockSpec((None, de, d), lambda w, ub, ue, gs, nu: (ue[w], 0, 0)),
        ],
        out_specs=pl.BlockSpec((bm, d), lambda w, ub, ue, gs, nu: (ub[w], 0)),
    )
    return pl.pallas_call(
        _experts_body,
        grid_spec=grid_spec,
        out_shape=jax.ShapeDtypeStruct((m, d), F32),
        compiler_params=pltpu.CompilerParams(
            dimension_semantics=("arbitrary",), vmem_limit_bytes=VMEM_LIMIT),
        name="moe_experts",
    )(unit_blk, unit_exp, gstart, n_units, xs, wg, wu, wd)


def _combine_body(d0_ref, d1_ref, y_ref, h_ref, gate_ref, o_ref, b0, b1, sem):
    tm = d0_ref.shape[0]

    def copy(i, d_ref, buf):
        return pltpu.make_async_copy(y_ref.at[pl.ds(d_ref[i], 1)], buf.at[pl.ds(i, 1)], sem)

    def start(i, carry):
        copy(i, d0_ref, b0).start(priority=0)
        copy(i, d1_ref, b1).start(priority=1)
        return carry

    def wait(i, carry):
        copy(i, d0_ref, b0).wait()
        copy(i, d1_ref, b1).wait()
        return carry

    lax.fori_loop(0, tm, start, 0, unroll=8)
    lax.fori_loop(0, tm, wait, 0, unroll=8)
    gate = gate_ref[...]
    o_ref[...] = h_ref[...] + b0[...] * gate[:, 0:1] + b1[...] * gate[:, 1:2]


def _combine(d0, d1, y, h, gate):
    n, d = h.shape
    tm = min(256, n)
    smem = pl.BlockSpec((tm,), lambda i: (i,), memory_space=pltpu.SMEM)
    return pl.pallas_call(
        _combine_body,
        grid=(n // tm,),
        in_specs=[smem, smem, pl.BlockSpec(memory_space=pl.ANY),
                  pl.BlockSpec((tm, d), lambda i: (i, 0)),
                  pl.BlockSpec((tm, LANES), lambda i: (i, 0))],
        out_specs=pl.BlockSpec((tm, d), lambda i: (i, 0)),
        out_shape=jax.ShapeDtypeStruct((n, d), F32),
        scratch_shapes=[pltpu.VMEM((tm, d), F32), pltpu.VMEM((tm, d), F32), pltpu.SemaphoreType.DMA],
        compiler_params=pltpu.CompilerParams(
            dimension_semantics=("arbitrary",), vmem_limit_bytes=VMEM_LIMIT),
        name="moe_combine",
    )(d0, d1, y, h, gate)


def _rope_tables(seq):
    half = ROT_DIM // 2
    inv = ROPE_THETA ** (-jnp.arange(0, ROT_DIM, 2, dtype=F32) / ROT_DIM)
    ang = jnp.arange(seq, dtype=F32)[:, None] * inv[None, :]
    cos, sin = jnp.cos(ang), jnp.sin(ang)
    ones = jnp.ones((seq, DH_A - ROT_DIM), F32)
    zeros = jnp.zeros((seq, DH_A - half), F32)
    zeros_h = jnp.zeros((seq, half), F32)
    zeros_r = jnp.zeros((seq, DH_A - ROT_DIM), F32)
    cos_c = jnp.concatenate([cos, cos, ones], axis=1)
    sa_c = jnp.concatenate([-sin, zeros], axis=1)
    sb_c = jnp.concatenate([zeros_h, sin, zeros_r], axis=1)
    tile = lambda t: jnp.concatenate([t, t], axis=1)
    return tile(cos_c), tile(sa_c), tile(sb_c)


def _layer(x, l, lambda_init, ln1_g, w_in, q_norm_g, k_norm_g, lambda_q1, lambda_k1, lambda_q2,
           lambda_k2, subln_g, shift_mu, w0, w_lora_up, a0, a_lora_up, g_lora_up, k_k, k_a, r_k,
           ln_x_g, ln_x_b, w_out, ln2_g, w_router_group, b_router_group, w_router_expert,
           b_router_expert, w_exp_gate, w_exp_up, w_exp_down):
    b, s, d = x.shape
    n = b * s
    d_attn3 = 3 * (d // 2)
    d_attn = d // 2
    d_r = w0.shape[1]
    x2 = x.reshape(n, d)
    row = lambda a: a[l].reshape(1, -1)

    w_in_b = w_in[l].astype(BF16)
    ua, ub = _in_proj(x2, row(ln1_g), w_in_b[:, :d_attn3], w_in_b[:, d_attn3:])

    cosf, sina, sinb = _rope_tables(s)
    tile2 = lambda a: jnp.concatenate([a[l], a[l]]).reshape(1, -1)
    lvec = jnp.stack([lambda_q1[l], lambda_k1[l], lambda_q2[l], lambda_k2[l]])
    o_a = _diff_attn(ua.reshape(b, s, d_attn3), cosf, sina, sinb, tile2(q_norm_g), tile2(k_norm_g),
                     lvec, row(subln_g), lambda_init)

    dl = w_lora_up.shape[1]
    zeros_l = jnp.zeros((dl, d_r), F32)
    wlu = jnp.concatenate([w_lora_up[l], zeros_l], axis=0).astype(BF16)
    alu = jnp.concatenate([zeros_l, a_lora_up[l]], axis=0).astype(BF16)
    hid = jnp.arange(d_r) // RWKV_HEAD
    bd = (hid[:, None] == hid[None, :]).astype(BF16)
    tr = min(RWKV_ROWS, s)
    ti = jnp.arange(tr)
    tri = ((ti[:, None] >= ti[None, :]) & (ti[:, None] // CHUNK == ti[None, :] // CHUNK)).astype(BF16)
    o_b = _rwkv7(ub.reshape(b, s, -1), row(shift_mu), row(w0), wlu, row(a0), alu,
                 g_lora_up[l].astype(BF16), row(k_k), row(k_a), row(r_k), row(ln_x_g), row(ln_x_b),
                 bd, tri, CHUNK)

    w_out_b = w_out[l].astype(BF16)
    wr = jnp.concatenate([w_router_group[l], w_router_expert[l],
                          jnp.zeros((d, LANES - N_GROUPS - N_EXPERTS), F32)], axis=1)
    wr_hi = wr.astype(BF16)
    wr_lo = (wr - wr_hi.astype(F32)).astype(BF16)
    br = jnp.concatenate([b_router_group[l], b_router_expert[l],
                          jnp.zeros((LANES - N_GROUPS - N_EXPERTS,), F32)]).reshape(1, LANES)
    tm_r = min(512, n)
    tril = (jnp.arange(tm_r)[:, None] > jnp.arange(tm_r)[None, :]).astype(BF16)
    h, t, eid, gate, rank, cnt = _out_proj_route(
        x2, o_a.reshape(n, d_attn), o_b.reshape(n, d_r), w_out_b[:d_attn], w_out_b[d_attn:],
        row(ln2_g), wr_hi, wr_lo, br, tril)

    counts = cnt[0, :N_EXPERTS].astype(I32)
    gend = jnp.cumsum(counts)
    gstart = jnp.concatenate([jnp.zeros((1,), I32), gend]).astype(I32)
    onehot = (eid[:, :2, None] == jnp.arange(N_EXPERTS)[None, None, :])
    dest = jnp.sum(jnp.where(onehot, gstart[:N_EXPERTS], 0), axis=-1) + rank[:, :2]
    d0 = dest[:, 0].astype(I32)
    d1 = dest[:, 1].astype(I32)

    m = 2 * n
    bm = min(256, m)
    n_blk = m // bm
    max_units = n_blk + N_EXPERTS - 1
    first_blk = gstart[:N_EXPERTS] // bm
    last_blk = jnp.where(counts > 0, (gend - 1) // bm, first_blk)
    units_e = jnp.where(counts > 0, last_blk - first_blk + 1, 0)
    unit_end = jnp.cumsum(units_e)
    unit_start = unit_end - units_e
    n_units = unit_end[-1]
    w_ids = jnp.arange(max_units)
    u_exp = jnp.minimum(jnp.sum((unit_end[None, :] <= w_ids[:, None]).astype(I32), axis=1),
                        N_EXPERTS - 1)
    u_blk = first_blk[u_exp] + (w_ids - unit_start[u_exp])
    last = jnp.maximum(n_units - 1, 0)
    pad = w_ids >= n_units
    u_exp = jnp.where(pad, u_exp[last], u_exp).astype(I32)
    u_blk = jnp.where(pad, u_blk[last], u_blk).astype(I32)

    xs = _dispatch(d0, d1, t)
    y = _experts(u_blk, u_exp, gstart, n_units.reshape(1).astype(I32), xs,
                 w_exp_gate[l].astype(BF16), w_exp_up[l].astype(BF16), w_exp_down[l].astype(BF16), bm)
    out = _combine(d0, d1, y, h, gate)
    return out.reshape(b, s, d)


def kernel(x, ln1_g, w_in, q_norm_g, k_norm_g, lambda_q1, lambda_k1, lambda_q2, lambda_k2, subln_g, shift_mu, w0, w_lora_up, a0, a_lora_up, g_lora_up, k_k, k_a, r_k, ln_x_g, ln_x_b, w_out, ln2_g, w_router_group, b_router_group, w_router_expert, b_router_expert, w_exp_gate, w_exp_up, w_exp_down):
    h = x
    for l in range(ln1_g.shape[0]):
        lambda_init = 0.8 - 0.6 * math.exp(-0.3 * l)
        h = _layer(h, l, lambda_init, ln1_g, w_in, q_norm_g, k_norm_g, lambda_q1, lambda_k1,
                   lambda_q2, lambda_k2, subln_g, shift_mu, w0, w_lora_up, a0, a_lora_up, g_lora_up,
                   k_k, k_a, r_k, ln_x_g, ln_x_b, w_out, ln2_g, w_router_group, b_router_group,
                   w_router_expert, b_router_expert, w_exp_gate, w_exp_up, w_exp_down)
    return h
```

```python
import functools
import math

import jax
import jax.numpy as jnp
from jax import lax
from jax.experimental import pallas as pl
from jax.experimental.pallas import tpu as pltpu

F32 = jnp.float32
BF16 = jnp.bfloat16
I32 = jnp.int32

CHUNK = 64
DH_A = 64
DV_A = 128
ROT_DIM = 16
ROPE_THETA = 500000.0
QK_NORM_EPS = 1e-6
SUBLN_EPS = 1e-5
RWKV_HEAD = 64
LN_X_EPS = 64e-5
RWKV_ROWS = 256
N_GROUPS = 4
EXPERTS_PER_GROUP = 8
N_EXPERTS = 32
NORM_EPS = 1e-6

LANES = 128
VMEM_LIMIT = 56 * 1024 * 1024

_NEG = -1e30


def _dot(a, b):
    return jnp.dot(a, b, preferred_element_type=F32)


def _dot_nt(a, b):
    return lax.dot_general(a, b, (((1,), (1,)), ((), ())), preferred_element_type=F32)


def _bdot(a, b):
    return _dot(a.astype(BF16), b.astype(BF16))


def _split(x):
    hi = x.astype(BF16)
    lo = (x - hi.astype(F32)).astype(BF16)
    return hi, lo


def _in_proj_body(x_ref, g_ref, wa_ref, wb_ref, ua_ref, ub_ref):
    x = x_ref[...]
    ms = jnp.mean(x * x, axis=-1, keepdims=True)
    xn = (x * lax.rsqrt(ms + NORM_EPS) * g_ref[...]).astype(BF16)
    ua_ref[...] = _dot(xn, wa_ref[...])
    ub_ref[...] = _dot(xn, wb_ref[...])


def _in_proj(x2, g, wa, wb):
    n, d = x2.shape
    tm = min(512, n)
    na, nb = wa.shape[1], wb.shape[1]
    return pl.pallas_call(
        _in_proj_body,
        grid=(n // tm,),
        in_specs=[
            pl.BlockSpec((tm, d), lambda i: (i, 0)),
            pl.BlockSpec((1, d), lambda i: (0, 0)),
            pl.BlockSpec((d, na), lambda i: (0, 0)),
            pl.BlockSpec((d, nb), lambda i: (0, 0)),
        ],
        out_specs=[
            pl.BlockSpec((tm, na), lambda i: (i, 0)),
            pl.BlockSpec((tm, nb), lambda i: (i, 0)),
        ],
        out_shape=[jax.ShapeDtypeStruct((n, na), F32), jax.ShapeDtypeStruct((n, nb), F32)],
        compiler_params=pltpu.CompilerParams(
            dimension_semantics=("arbitrary",), vmem_limit_bytes=VMEM_LIMIT),
        name="in_proj",
    )(x2, g, wa, wb)


def _attn_body(q_ref, k_ref, v_ref, cos_ref, sa_ref, sb_ref, gq_ref, gk_ref, l_ref, sg_ref,
               o_ref, q0_s, q1_s, k_s, v_s, *, lambda_init, bq):
    s_len = q_ref.shape[0]
    lane = lax.broadcasted_iota(I32, (1, LANES), 1)
    lo = lane < DH_A

    def prep(t, g):
        t2 = t * t
        s_lo = jnp.sum(jnp.where(lo, t2, 0.0), axis=-1, keepdims=True)
        s_hi = jnp.sum(jnp.where(lo, 0.0, t2), axis=-1, keepdims=True)
        ms = jnp.where(lo, s_lo, s_hi) * (1.0 / DH_A)
        tn = t * lax.rsqrt(ms + QK_NORM_EPS) * g
        return (tn * cos_ref[...] + pltpu.roll(tn, LANES - ROT_DIM // 2, 1) * sa_ref[...]
                + pltpu.roll(tn, ROT_DIM // 2, 1) * sb_ref[...])

    q = prep(q_ref[...], gq_ref[...]) * (DH_A ** -0.5 * math.log2(math.e))
    q0_s[...] = jnp.where(lo, q, 0.0).astype(BF16)
    q1_s[...] = jnp.where(lo, 0.0, q).astype(BF16)
    k_s[...] = prep(k_ref[...], gk_ref[...]).astype(BF16)
    v_s[...] = v_ref[...].astype(BF16)

    l = l_ref[...]
    lam = (jnp.exp(jnp.sum(l[0:1] * l[1:2], axis=-1, keepdims=True))
           - jnp.exp(jnp.sum(l[2:3] * l[3:4], axis=-1, keepdims=True)) + lambda_init)

    chunk_shift = int(math.log2(CHUNK))
    rr = lax.broadcasted_iota(I32, (bq, bq), 0) >> chunk_shift
    cc = lax.broadcasted_iota(I32, (bq, bq), 1) >> chunk_shift
    diag_ok = cc <= rr

    diag_ok2 = jnp.concatenate([diag_ok, diag_ok], axis=0)

    for i in range(s_len // bq):
        r0 = i * bq
        qq = jnp.concatenate([q0_s[r0:r0 + bq, :], q1_s[r0:r0 + bq, :]], axis=0)
        m = l_run = acc = None
        for j in range(i + 1):
            c0 = j * bq
            s = _dot_nt(qq, k_s[c0:c0 + bq, :])
            if j == i:
                s = jnp.where(diag_ok2, s, _NEG)
            m_tile = jnp.max(s, axis=-1, keepdims=True)
            if j == 0:
                m = m_tile
                e = jnp.exp2(s - m)
                l_run = jnp.sum(e, axis=-1, keepdims=True)
                acc = _dot(e.astype(BF16), v_s[c0:c0 + bq, :])
            else:
                m_new = jnp.maximum(m, m_tile)
                alpha = jnp.exp2(m - m_new)
                e = jnp.exp2(s - m_new)
                l_run = l_run * alpha + jnp.sum(e, axis=-1, keepdims=True)
                acc = acc * alpha + _dot(e.astype(BF16), v_s[c0:c0 + bq, :])
                m = m_new
        w0 = 1.0 / l_run[:bq]
        w1 = lam / l_run[bq:]
        o = acc[:bq] * w0 - acc[bq:] * w1
        o = o * lax.rsqrt(jnp.mean(o * o, axis=-1, keepdims=True) + SUBLN_EPS)
        o_ref[r0:r0 + bq, :] = o * (sg_ref[...] * (1.0 - lambda_init))


def _diff_attn(ua3, cosf, sina, sinb, gq, gk, lvec, sg, lambda_init):
    b, s, _ = ua3.shape
    h_a = ua3.shape[2] // (3 * DV_A)
    bq = min(256, s)
    blk = lambda off: pl.BlockSpec((None, s, DV_A), lambda bi, hi: (bi, 0, off + hi))
    const = lambda shape: pl.BlockSpec(shape, lambda bi, hi: (0,) * len(shape))
    return pl.pallas_call(
        functools.partial(_attn_body, lambda_init=lambda_init, bq=bq),
        grid=(b, h_a),
        in_specs=[blk(0), blk(h_a), blk(2 * h_a),
                  const((s, LANES)), const((s, LANES)), const((s, LANES)),
                  const((1, LANES)), const((1, LANES)), const((4, DH_A)), const((1, LANES))],
        out_specs=pl.BlockSpec((None, s, DV_A), lambda bi, hi: (bi, 0, hi)),
        out_shape=jax.ShapeDtypeStruct((b, s, h_a * DV_A), F32),
        scratch_shapes=[pltpu.VMEM((s, LANES), BF16)] * 4,
        compiler_params=pltpu.CompilerParams(
            dimension_semantics=("arbitrary", "arbitrary"), vmem_limit_bytes=VMEM_LIMIT),
        name="diff_attn",
    )(ua3, ua3, ua3, cosf, sina, sinb, gq, gk, lvec, sg)


def _rwkv_body(u_ref, mu_ref, w0_ref, wlu_ref, a0_ref, alu_ref, glu_ref, kkw_ref, ka_ref, rk_ref,
               lg_ref, lb_ref, bd_ref, tri_ref, o_ref, prev_s, st_s, *, c_len):
    tr = u_ref.shape[0]
    d_r = o_ref.shape[1]
    n_pair = d_r // LANES
    n_ch = tr // c_len

    @pl.when(pl.program_id(1) == 0)
    def _():
        prev_s[...] = jnp.zeros_like(prev_s)
        st_s[...] = jnp.zeros_like(st_s)

    u = u_ref[...]
    row = lax.broadcasted_iota(I32, (tr, 1), 0)
    u_prev = jnp.where(row == 0, prev_s[...], pltpu.roll(u, 1, 0))
    prev_s[...] = u[tr - 1:tr, :]
    us = u + (u_prev - u) * mu_ref[...]

    r = us[:, 0:d_r]
    k = us[:, d_r:2 * d_r]
    v = us[:, 2 * d_r:3 * d_r]
    lo_in = us[:, 3 * d_r:3 * d_r + LANES]
    g_in = us[:, 3 * d_r + LANES:3 * d_r + 2 * LANES]

    bd = bd_ref[...]

    def head_sum(x):
        hi, lo = _split(x)
        return _dot(hi, bd) + _dot(lo, bd)

    z = -(w0_ref[...] + _bdot(jnp.tanh(lo_in), wlu_ref[...]))
    softplus = jnp.maximum(z, 0.0) + jnp.log(1.0 + jnp.exp(-jnp.abs(z)))
    logw = -jnp.exp(-softplus - 0.5)
    a = 1.0 / (1.0 + jnp.exp(-(a0_ref[...] + _bdot(lo_in, alu_ref[...]))))
    g = _bdot(1.0 / (1.0 + jnp.exp(-g_in)), glu_ref[...])
    kk_raw = k * kkw_ref[...]
    kk = kk_raw / jnp.maximum(jnp.sqrt(head_sum(kk_raw * kk_raw)), 1e-12)
    k_mod = k * (1.0 + (a - 1.0) * ka_ref[...])

    tri = tri_ref[...]
    w_hi, w_lo = _split(logw)
    cum = _dot(tri, w_hi) + _dot(tri, w_lo)
    last_rows = [cum[(c + 1) * c_len - 1:(c + 1) * c_len, :] for c in range(n_ch)]
    cum_last = jnp.concatenate([jnp.broadcast_to(lr, (c_len, d_r)) for lr in last_rows], axis=0)
    e_neg = jnp.exp(-cum)
    e_rem = jnp.exp(cum_last - cum)
    a_t = -kk * jnp.exp(cum - logw)
    r_t = r * jnp.exp(cum)
    b_raw = kk * a
    b_t = b_raw * e_neg
    k_t = k_mod * e_neg
    b_2 = b_raw * e_rem
    k_2 = k_mod * e_rem

    lane = lax.broadcasted_iota(I32, (1, LANES), 1)
    head0 = lane < RWKV_HEAD
    two_c = 2 * c_len
    rr = lax.broadcasted_iota(I32, (two_c, two_c), 0)
    cc = lax.broadcasted_iota(I32, (two_c, two_c), 1)
    t_i = rr & (c_len - 1)
    s_i = cc & (c_len - 1)
    strict = t_i > s_i
    incl = t_i >= s_i
    eye = rr == cc

    def stack(x, c, j):
        xp = x[c * c_len:(c + 1) * c_len, j * LANES:(j + 1) * LANES]
        return jnp.concatenate([jnp.where(head0, xp, 0.0), jnp.where(head0, 0.0, xp)], axis=0)

    units = [(c, j) for c in range(n_ch) for j in range(n_pair)]
    n_double = int(math.log2(c_len))

    a_s = [stack(a_t, c, j) for c, j in units]
    r_s = [stack(r_t, c, j) for c, j in units]
    v_b = [stack(v, c, j).astype(BF16) for c, j in units]
    gram = [_dot_nt(jnp.concatenate([a_s[i], r_s[i]], axis=0).astype(BF16),
                    jnp.concatenate([stack(b_t, c, j), stack(k_t, c, j)], axis=0).astype(BF16))
            for i, (c, j) in enumerate(units)]
    p = [jnp.where(strict, gm[:two_c, :two_c], 0.0) for gm in gram]
    m_rb = [jnp.where(incl, gm[two_c:, :two_c], 0.0).astype(BF16) for gm in gram]
    vv = [_dot(jnp.concatenate([jnp.where(strict, gm[:two_c, two_c:], 0.0),
                                jnp.where(incl, gm[two_c:, two_c:], 0.0)], axis=0).astype(BF16), vb)
          for gm, vb in zip(gram, v_b)]
    x = [jnp.concatenate([a_s[i], vv[i][:two_c]], axis=1) for i in range(len(units))]
    for it in range(n_double):
        p_b = [pi.astype(BF16) for pi in p]
        x = [xi + _dot(pb, xi.astype(BF16)) for xi, pb in zip(x, p_b)]
        if it + 1 < n_double:
            p = [_dot(pb, pb) for pb in p_b]
    x_b = [xi.astype(BF16) for xi in x]
    mx = [_dot(mb, xb) for mb, xb in zip(m_rb, x_b)]
    bx = [_dot(stack(b_2, c, j).T.astype(BF16), x_b[i]) for i, (c, j) in enumerate(units)]
    kv = [_dot(stack(k_2, c, j).T.astype(BF16), v_b[i]) for i, (c, j) in enumerate(units)]

    y_rows = []
    st = [st_s[j] for j in range(n_pair)]
    for c in range(n_ch):
        p_c = jnp.exp(last_rows[c])
        y_pairs = []
        for j in range(n_pair):
            i = c * n_pair + j
            r_h = (r_s[i] + mx[i][:, :two_c]).astype(BF16)
            y_h = mx[i][:, two_c:] + vv[i][two_c:]
            g_m = (jnp.where(eye, p_c[:, j * LANES:(j + 1) * LANES], 0.0) + bx[i][:, :two_c]).astype(BF16)
            h_m = bx[i][:, two_c:] + kv[i]
            st_b = st[j].astype(BF16)
            y_s = _dot(r_h, st_b) + y_h
            st[j] = _dot(g_m, st_b) + h_m
            y_pairs.append(y_s[:c_len] + y_s[c_len:])
        y_rows.append(jnp.concatenate(y_pairs, axis=1))
    for j in range(n_pair):
        st_s[j] = st[j]
    y = jnp.concatenate(y_rows, axis=0)

    inv_n = 1.0 / RWKV_HEAD
    mu = head_sum(y) * inv_n
    dlt = y - mu
    var = head_sum(dlt * dlt) * inv_n
    yn = dlt * lax.rsqrt(var + LN_X_EPS) * lg_ref[...] + lb_ref[...]
    bonus = head_sum(r * k_mod * rk_ref[...]) * v
    o_ref[...] = (yn + bonus) * g


def _rwkv7(ub3, mu, w0, wlu, a0, alu, glu, kkw, ka, rk, lg, lb, bd, tri, c_len):
    b, s, cols = ub3.shape
    d_r = w0.shape[1]
    tr = tri.shape[0]
    const = lambda a: pl.BlockSpec(a.shape, lambda bi, ci: (0,) * a.ndim)
    params = (mu, w0, wlu, a0, alu, glu, kkw, ka, rk, lg, lb, bd, tri)
    return pl.pallas_call(
        functools.partial(_rwkv_body, c_len=c_len),
        grid=(b, s // tr),
        in_specs=[pl.BlockSpec((None, tr, cols), lambda bi, ci: (bi, ci, 0))]
                 + [const(a) for a in params],
        out_specs=pl.BlockSpec((None, tr, d_r), lambda bi, ci: (bi, ci, 0)),
        out_shape=jax.ShapeDtypeStruct((b, s, d_r), F32),
        scratch_shapes=[pltpu.VMEM((1, cols), F32),
                        pltpu.VMEM((d_r // LANES, LANES, LANES), F32)],
        compiler_params=pltpu.CompilerParams(
            dimension_semantics=("arbitrary", "arbitrary"), vmem_limit_bytes=VMEM_LIMIT),
        name="rwkv7",
    )(ub3, *params)


def _out_body(x_ref, oa_ref, ob_ref, wa_ref, wb_ref, g2_ref, wrh_ref, wrl_ref, br_ref, tril_ref,
              h_ref, t_ref, eid_ref, gate_ref, rank_ref, cnt_ref, run_s):
    tm = x_ref.shape[0]

    @pl.when(pl.program_id(0) == 0)
    def _():
        run_s[...] = jnp.zeros_like(run_s)

    h = x_ref[...] + _bdot(oa_ref[...], wa_ref[...]) + _bdot(ob_ref[...], wb_ref[...])
    h_ref[...] = h
    t = h * lax.rsqrt(jnp.mean(h * h, axis=-1, keepdims=True) + NORM_EPS) * g2_ref[...]
    t_ref[...] = t

    t_hi, t_lo = _split(t)
    wrh = wrh_ref[...]
    lg = _dot(t_hi, wrh) + _dot(t_lo, wrh) + _dot(t_hi, wrl_ref[...]) + br_ref[...]

    lane_i = lax.broadcasted_iota(I32, (1, LANES), 1)
    lane = lane_i.astype(F32)
    big = float(LANES)

    def first_index(mask):
        return jnp.min(jnp.where(mask, lane, big), axis=-1, keepdims=True)

    def masked_softmax(mask):
        m = jnp.max(jnp.where(mask, lg, _NEG), axis=-1, keepdims=True)
        e = jnp.where(mask, jnp.exp(lg - m), 0.0)
        return e / jnp.sum(e, axis=-1, keepdims=True)

    is_g = lane_i < N_GROUPS
    gprob = masked_softmax(is_g)
    p_group = jnp.max(gprob, axis=-1, keepdims=True)
    gsel = first_index(is_g & (gprob == p_group))
    base = N_GROUPS + gsel * EXPERTS_PER_GROUP
    in_e = (lane >= base) & (lane < base + EXPERTS_PER_GROUP)
    eprob = masked_softmax(in_e)
    p1 = jnp.max(eprob, axis=-1, keepdims=True)
    i1 = first_index(in_e & (eprob == p1))
    rest = in_e & (lane != i1)
    p2 = jnp.max(jnp.where(rest, eprob, -1.0), axis=-1, keepdims=True)
    i2 = first_index(rest & (eprob == p2))
    den = p1 + p2
    g1 = p_group * p1 / den
    g2 = p_group * p2 / den
    e1 = i1 - N_GROUPS
    e2 = i2 - N_GROUPS

    oh1 = (lane == e1).astype(F32)
    oh2 = (lane == e2).astype(F32)
    tril = tril_ref[...]
    before1 = _dot(tril, oh1.astype(BF16))
    tot1 = jnp.sum(oh1, axis=0, keepdims=True)
    before2 = _dot(tril, oh2.astype(BF16)) + tot1
    run = run_s[...]
    rk1 = jnp.sum(oh1 * (before1 + run), axis=-1, keepdims=True)
    rk2 = jnp.sum(oh2 * (before2 + run), axis=-1, keepdims=True)
    run = run + tot1 + jnp.sum(oh2, axis=0, keepdims=True)
    run_s[...] = run
    cnt_ref[...] = run

    sel0 = lane_i == 0
    sel1 = lane_i == 1
    eid_ref[...] = jnp.where(sel0, e1, jnp.where(sel1, e2, 0.0)).astype(I32)
    rank_ref[...] = jnp.where(sel0, rk1, jnp.where(sel1, rk2, 0.0)).astype(I32)
    gate_ref[...] = jnp.where(sel0, g1, jnp.where(sel1, g2, 0.0))


def _out_proj_route(x2, oa, ob, wa, wb, g2, wrh, wrl, br, tril):
    n, d = x2.shape
    tm = tril.shape[0]
    da, db = oa.shape[1], ob.shape[1]
    row = lambda w: pl.BlockSpec((tm, w), lambda i: (i, 0))
    const = lambda a: pl.BlockSpec(a.shape, lambda i: (0,) * a.ndim)
    return pl.pallas_call(
        _out_body,
        grid=(n // tm,),
        in_specs=[row(d), row(da), row(db), const(wa), const(wb), const(g2), const(wrh), const(wrl),
                  const(br), const(tril)],
        out_specs=[row(d), row(d), row(LANES), row(LANES), row(LANES),
                   pl.BlockSpec((1, LANES), lambda i: (0, 0))],
        out_shape=[jax.ShapeDtypeStruct((n, d), F32), jax.ShapeDtypeStruct((n, d), F32),
                   jax.ShapeDtypeStruct((n, LANES), I32), jax.ShapeDtypeStruct((n, LANES), F32),
                   jax.ShapeDtypeStruct((n, LANES), I32), jax.ShapeDtypeStruct((1, LANES), F32)],
        scratch_shapes=[pltpu.VMEM((1, LANES), F32)],
        compiler_params=pltpu.CompilerParams(
            dimension_semantics=("arbitrary",), vmem_limit_bytes=VMEM_LIMIT),
        name="out_proj_route",
    )(x2, oa, ob, wa, wb, g2, wrh, wrl, br, tril)


def _dispatch_body(d0_ref, d1_ref, t_ref, xs_ref, sem):
    tm = d0_ref.shape[0]

    def copy(i, d_ref):
        return pltpu.make_async_copy(t_ref.at[pl.ds(i, 1)], xs_ref.at[pl.ds(d_ref[i], 1)], sem)

    def start(i, carry):
        copy(i, d0_ref).start(priority=0)
        copy(i, d1_ref).start(priority=1)
        return carry

    def wait(i, carry):
        copy(i, d0_ref).wait()
        copy(i, d1_ref).wait()
        return carry

    lax.fori_loop(0, tm, start, 0, unroll=8)
    lax.fori_loop(0, tm, wait, 0, unroll=8)


def _dispatch(d0, d1, t):
    n, d = t.shape
    tm = min(512, n)
    smem = pl.BlockSpec((tm,), lambda i: (i,), memory_space=pltpu.SMEM)
    return pl.pallas_call(
        _dispatch_body,
        grid=(n // tm,),
        in_specs=[smem, smem, pl.BlockSpec((tm, d), lambda i: (i, 0))],
        out_specs=pl.BlockSpec(memory_space=pl.ANY),
        out_shape=jax.ShapeDtypeStruct((2 * n, d), t.dtype),
        scratch_shapes=[pltpu.SemaphoreType.DMA],
        compiler_params=pltpu.CompilerParams(dimension_semantics=("arbitrary",)),
        name="moe_dispatch",
    )(d0, d1, t)


def _experts_body(ub_ref, ue_ref, gs_ref, nu_ref, x_ref, wg_ref, wu_ref, wd_ref, y_ref,
                  wg_s, wu_s, wd_s):
    w = pl.program_id(0)
    bm = x_ref.shape[0]

    @pl.when(w < nu_ref[0])
    def _():
        e = ue_ref[w]
        blk = ub_ref[w]
        prev = jnp.maximum(w - 1, 0)

        @pl.when(jnp.logical_or(w == 0, ue_ref[prev] != e))
        def _():
            wg_s[...] = wg_ref[...].astype(BF16)
            wu_s[...] = wu_ref[...].astype(BF16)
            wd_s[...] = wd_ref[...].astype(BF16)

        rows = blk * bm + lax.broadcasted_iota(I32, (bm, 1), 0)
        valid = (rows >= gs_ref[e]) & (rows < gs_ref[e + 1])
        xb = x_ref[...].astype(BF16)
        gt = _dot(xb, wg_s[...])
        up = _dot(xb, wu_s[...])
        act = (gt / (1.0 + jnp.exp(-gt))) * up
        y = jnp.where(valid, _dot(act.astype(BF16), wd_s[...]), 0.0)
        first = jnp.logical_or(w == 0, ub_ref[prev] != blk)

        @pl.when(first)
        def _():
            y_ref[...] = y

        @pl.when(jnp.logical_not(first))
        def _():
            y_ref[...] += y


def _experts(unit_blk, unit_exp, gstart, n_units, xs, wg, wu, wd, bm):
    m, d = xs.shape
    de = wg.shape[2]
    max_units = unit_blk.shape[0]
    grid_spec = pltpu.PrefetchScalarGridSpec(
        num_scalar_prefetch=4,
        grid=(max_units,),
        in_specs=[
            pl.BlockSpec((bm, d), lambda w, ub, ue, gs, nu: (ub[w], 0)),
            pl.BlockSpec((None, d, de), lambda w, ub, ue, gs, nu: (ue[w], 0, 0)),
            pl.BlockSpec((None, d, de), lambda w, ub, ue, gs, nu: (ue[w], 0, 0)),
            pl.BlockSpec((None, de, d), lambda w, ub, ue, gs, nu: (ue[w], 0, 0)),
        ],
        out_specs=pl.BlockSpec((bm, d), lambda w, ub, ue, gs, nu: (ub[w], 0)),
        scratch_shapes=[pltpu.VMEM((d, de), BF16), pltpu.VMEM((d, de), BF16),
                        pltpu.VMEM((de, d), BF16)],
    )
    return pl.pallas_call(
        _experts_body,
        grid_spec=grid_spec,
        out_shape=jax.ShapeDtypeStruct((m, d), F32),
        compiler_params=pltpu.CompilerParams(
            dimension_semantics=("arbitrary",), vmem_limit_bytes=VMEM_LIMIT),
        name="moe_experts",
    )(unit_blk, unit_exp, gstart, n_units, xs, wg, wu, wd)


def _combine_body(d0_ref, d1_ref, y_ref, h_ref, gate_ref, o_ref, b0, b1, sem):
    tm = d0_ref.shape[0]

    def copy(i, d_ref, buf):
        return pltpu.make_async_copy(y_ref.at[pl.ds(d_ref[i], 1)], buf.at[pl.ds(i, 1)], sem)

    def start(i, carry):
        copy(i, d0_ref, b0).start(priority=0)
        copy(i, d1_ref, b1).start(priority=1)
        return carry

    def wait(i, carry):
        copy(i, d0_ref, b0).wait()
        copy(i, d1_ref, b1).wait()
        return carry

    lax.fori_loop(0, tm, start, 0, unroll=8)
    lax.fori_loop(0, tm, wait, 0, unroll=8)
    gate = gate_ref[...]
    o_ref[...] = h_ref[...] + b0[...] * gate[:, 0:1] + b1[...] * gate[:, 1:2]


def _combine(d0, d1, y, h, gate):
    n, d = h.shape
    tm = min(256, n)
    smem = pl.BlockSpec((tm,), lambda i: (i,), memory_space=pltpu.SMEM)
    return pl.pallas_call(
        _combine_body,
        grid=(n // tm,),
        in_specs=[smem, smem, pl.BlockSpec(memory_space=pl.ANY),
                  pl.BlockSpec((tm, d), lambda i: (i, 0)),
                  pl.BlockSpec((tm, LANES), lambda i: (i, 0))],
        out_specs=pl.BlockSpec((tm, d), lambda i: (i, 0)),
        out_shape=jax.ShapeDtypeStruct((n, d), F32),
        scratch_shapes=[pltpu.VMEM((tm, d), F32), pltpu.VMEM((tm, d), F32), pltpu.SemaphoreType.DMA],
        compiler_params=pltpu.CompilerParams(
            dimension_semantics=("arbitrary",), vmem_limit_bytes=VMEM_LIMIT),
        name="moe_combine",
    )(d0, d1, y, h, gate)


def _rope_tables(seq):
    half = ROT_DIM // 2
    inv = ROPE_THETA ** (-jnp.arange(0, ROT_DIM, 2, dtype=F32) / ROT_DIM)
    ang = jnp.arange(seq, dtype=F32)[:, None] * inv[None, :]
    cos, sin = jnp.cos(ang), jnp.sin(ang)
    ones = jnp.ones((seq, DH_A - ROT_DIM), F32)
    zeros = jnp.zeros((seq, DH_A - half), F32)
    zeros_h = jnp.zeros((seq, half), F32)
    zeros_r = jnp.zeros((seq, DH_A - ROT_DIM), F32)
    cos_c = jnp.concatenate([cos, cos, ones], axis=1)
    sa_c = jnp.concatenate([-sin, zeros], axis=1)
    sb_c = jnp.concatenate([zeros_h, sin, zeros_r], axis=1)
    tile = lambda t: jnp.concatenate([t, t], axis=1)
    return tile(cos_c), tile(sa_c), tile(sb_c)


def _layer(x, l, lambda_init, ln1_g, w_in, q_norm_g, k_norm_g, lambda_q1, lambda_k1, lambda_q2,
           lambda_k2, subln_g, shift_mu, w0, w_lora_up, a0, a_lora_up, g_lora_up, k_k, k_a, r_k,
           ln_x_g, ln_x_b, w_out, ln2_g, w_router_group, b_router_group, w_router_expert,
           b_router_expert, w_exp_gate, w_exp_up, w_exp_down):
    b, s, d = x.shape
    n = b * s
    d_attn3 = 3 * (d // 2)
    d_attn = d // 2
    d_r = w0.shape[1]
    x2 = x.reshape(n, d)
    row = lambda a: a[l].reshape(1, -1)

    w_in_b = w_in[l].astype(BF16)
    ua, ub = _in_proj(x2, row(ln1_g), w_in_b[:, :d_attn3], w_in_b[:, d_attn3:])

    cosf, sina, sinb = _rope_tables(s)
    tile2 = lambda a: jnp.concatenate([a[l], a[l]]).reshape(1, -1)
    lvec = jnp.stack([lambda_q1[l], lambda_k1[l], lambda_q2[l], lambda_k2[l]])
    o_a = _diff_attn(ua.reshape(b, s, d_attn3), cosf, sina, sinb, tile2(q_norm_g), tile2(k_norm_g),
                     lvec, row(subln_g), lambda_init)

    dl = w_lora_up.shape[1]
    zeros_l = jnp.zeros((dl, d_r), F32)
    wlu = jnp.concatenate([w_lora_up[l], zeros_l], axis=0).astype(BF16)
    alu = jnp.concatenate([zeros_l, a_lora_up[l]], axis=0).astype(BF16)
    hid = jnp.arange(d_r) // RWKV_HEAD
    bd = (hid[:, None] == hid[None, :]).astype(BF16)
    tr = min(RWKV_ROWS, s)
    ti = jnp.arange(tr)
    tri = ((ti[:, None] >= ti[None, :]) & (ti[:, None] // CHUNK == ti[None, :] // CHUNK)).astype(BF16)
    o_b = _rwkv7(ub.reshape(b, s, -1), row(shift_mu), row(w0), wlu, row(a0), alu,
                 g_lora_up[l].astype(BF16), row(k_k), row(k_a), row(r_k), row(ln_x_g), row(ln_x_b),
                 bd, tri, CHUNK)

    w_out_b = w_out[l].astype(BF16)
    wr = jnp.concatenate([w_router_group[l], w_router_expert[l],
                          jnp.zeros((d, LANES - N_GROUPS - N_EXPERTS), F32)], axis=1)
    wr_hi = wr.astype(BF16)
    wr_lo = (wr - wr_hi.astype(F32)).astype(BF16)
    br = jnp.concatenate([b_router_group[l], b_router_expert[l],
                          jnp.zeros((LANES - N_GROUPS - N_EXPERTS,), F32)]).reshape(1, LANES)
    tm_r = min(512, n)
    tril = (jnp.arange(tm_r)[:, None] > jnp.arange(tm_r)[None, :]).astype(BF16)
    h, t, eid, gate, rank, cnt = _out_proj_route(
        x2, o_a.reshape(n, d_attn), o_b.reshape(n, d_r), w_out_b[:d_attn], w_out_b[d_attn:],
        row(ln2_g), wr_hi, wr_lo, br, tril)

    counts = cnt[0, :N_EXPERTS].astype(I32)
    gend = jnp.cumsum(counts)
    gstart = jnp.concatenate([jnp.zeros((1,), I32), gend]).astype(I32)
    onehot = (eid[:, :2, None] == jnp.arange(N_EXPERTS)[None, None, :])
    dest = jnp.sum(jnp.where(onehot, gstart[:N_EXPERTS], 0), axis=-1) + rank[:, :2]
    d0 = dest[:, 0].astype(I32)
    d1 = dest[:, 1].astype(I32)

    m = 2 * n
    bm = min(256, m)
    n_blk = m // bm
    max_units = n_blk + N_EXPERTS - 1
    first_blk = gstart[:N_EXPERTS] // bm
    last_blk = jnp.where(counts > 0, (gend - 1) // bm, first_blk)
    units_e = jnp.where(counts > 0, last_blk - first_blk + 1, 0)
    unit_end = jnp.cumsum(units_e)
    unit_start = unit_end - units_e
    n_units = unit_end[-1]
    w_ids = jnp.arange(max_units)
    u_exp = jnp.minimum(jnp.sum((unit_end[None, :] <= w_ids[:, None]).astype(I32), axis=1),
                        N_EXPERTS - 1)
    u_blk = first_blk[u_exp] + (w_ids - unit_start[u_exp])
    last = jnp.maximum(n_units - 1, 0)
    pad = w_ids >= n_units
    u_exp = jnp.where(pad, u_exp[last], u_exp).astype(I32)
    u_blk = jnp.where(pad, u_blk[last], u_blk).astype(I32)

    xs = _dispatch(d0, d1, t)
    y = _experts(u_blk, u_exp, gstart, n_units.reshape(1).astype(I32), xs,
                 w_exp_gate[l], w_exp_up[l], w_exp_down[l], bm)
    out = _combine(d0, d1, y, h, gate)
    return out.reshape(b, s, d)


def kernel(x, ln1_g, w_in, q_norm_g, k_norm_g, lambda_q1, lambda_k1, lambda_q2, lambda_k2, subln_g, shift_mu, w0, w_lora_up, a0, a_lora_up, g_lora_up, k_k, k_a, r_k, ln_x_g, ln_x_b, w_out, ln2_g, w_router_group, b_router_group, w_router_expert, b_router_expert, w_exp_gate, w_exp_up, w_exp_down):
    h = x
    for l in range(ln1_g.shape[0]):
        lambda_init = 0.8 - 0.6 * math.exp(-0.3 * l)
        h = _layer(h, l, lambda_init, ln1_g, w_in, q_norm_g, k_norm_g, lambda_q1, lambda_k1,
                   lambda_q2, lambda_k2, subln_g, shift_mu, w0, w_lora_up, a0, a_lora_up, g_lora_up,
                   k_k, k_a, r_k, ln_x_g, ln_x_b, w_out, ln2_g, w_router_group, b_router_group,
                   w_router_expert, b_router_expert, w_exp_gate, w_exp_up, w_exp_down)
    return h
```

```python
import functools
import math

import jax
import jax.numpy as jnp
from jax import lax
from jax.experimental import pallas as pl
from jax.experimental.pallas import tpu as pltpu

F32 = jnp.float32
BF16 = jnp.bfloat16
I32 = jnp.int32

CHUNK = 64
DH_A = 64
DV_A = 128
ROT_DIM = 16
ROPE_THETA = 500000.0
QK_NORM_EPS = 1e-6
SUBLN_EPS = 1e-5
RWKV_HEAD = 64
LN_X_EPS = 64e-5
RWKV_ROWS = 256
N_GROUPS = 4
EXPERTS_PER_GROUP = 8
N_EXPERTS = 32
NORM_EPS = 1e-6

LANES = 128
ROW_GROUP = 8
VMEM_LIMIT = 56 * 1024 * 1024

_NEG = -1e30


def _dot(a, b):
    return jnp.dot(a, b, preferred_element_type=F32)


def _dot_nt(a, b):
    return lax.dot_general(a, b, (((1,), (1,)), ((), ())), preferred_element_type=F32)


def _bdot(a, b):
    return _dot(a.astype(BF16), b.astype(BF16))


def _split(x):
    hi = x.astype(BF16)
    lo = (x - hi.astype(F32)).astype(BF16)
    return hi, lo


def _in_proj_body(x_ref, g_ref, wa_ref, wb_ref, ua_ref, ub_ref):
    x = x_ref[...]
    ms = jnp.mean(x * x, axis=-1, keepdims=True)
    xn = (x * lax.rsqrt(ms + NORM_EPS) * g_ref[...]).astype(BF16)
    ua_ref[...] = _dot(xn, wa_ref[...]).astype(ua_ref.dtype)
    ub_ref[...] = _dot(xn, wb_ref[...]).astype(ub_ref.dtype)


def _in_proj(x2, g, wa, wb):
    n, d = x2.shape
    tm = min(512, n)
    na, nb = wa.shape[1], wb.shape[1]
    return pl.pallas_call(
        _in_proj_body,
        grid=(n // tm,),
        in_specs=[
            pl.BlockSpec((tm, d), lambda i: (i, 0)),
            pl.BlockSpec((1, d), lambda i: (0, 0)),
            pl.BlockSpec((d, na), lambda i: (0, 0)),
            pl.BlockSpec((d, nb), lambda i: (0, 0)),
        ],
        out_specs=[
            pl.BlockSpec((tm, na), lambda i: (i, 0)),
            pl.BlockSpec((tm, nb), lambda i: (i, 0)),
        ],
        out_shape=[jax.ShapeDtypeStruct((n, na), BF16), jax.ShapeDtypeStruct((n, nb), BF16)],
        compiler_params=pltpu.CompilerParams(
            dimension_semantics=("arbitrary",), vmem_limit_bytes=VMEM_LIMIT),
        name="in_proj",
    )(x2, g, wa, wb)


def _attn_body(q_ref, k_ref, v_ref, cos_ref, sa_ref, sb_ref, gq_ref, gk_ref, l_ref, sg_ref,
               o_ref, q0_s, q1_s, k_s, v_s, *, lambda_init, bq):
    s_len = q_ref.shape[0]
    lane = lax.broadcasted_iota(I32, (1, LANES), 1)
    lo = lane < DH_A

    def prep(t, g):
        t2 = t * t
        s_lo = jnp.sum(jnp.where(lo, t2, 0.0), axis=-1, keepdims=True)
        s_hi = jnp.sum(jnp.where(lo, 0.0, t2), axis=-1, keepdims=True)
        ms = jnp.where(lo, s_lo, s_hi) * (1.0 / DH_A)
        tn = t * lax.rsqrt(ms + QK_NORM_EPS) * g
        return (tn * cos_ref[...] + pltpu.roll(tn, LANES - ROT_DIM // 2, 1) * sa_ref[...]
                + pltpu.roll(tn, ROT_DIM // 2, 1) * sb_ref[...])

    q = prep(q_ref[...].astype(F32), gq_ref[...]) * (DH_A ** -0.5 * math.log2(math.e))
    q0_s[...] = jnp.where(lo, q, 0.0).astype(BF16)
    q1_s[...] = jnp.where(lo, 0.0, q).astype(BF16)
    k_s[...] = prep(k_ref[...].astype(F32), gk_ref[...]).astype(BF16)
    v_s[...] = v_ref[...].astype(BF16)

    l = l_ref[...]
    lam = (jnp.exp(jnp.sum(l[0:1] * l[1:2], axis=-1, keepdims=True))
           - jnp.exp(jnp.sum(l[2:3] * l[3:4], axis=-1, keepdims=True)) + lambda_init)

    chunk_shift = int(math.log2(CHUNK))
    rr = lax.broadcasted_iota(I32, (bq, bq), 0) >> chunk_shift
    cc = lax.broadcasted_iota(I32, (bq, bq), 1) >> chunk_shift
    diag_ok = cc <= rr

    diag_ok2 = jnp.concatenate([diag_ok, diag_ok], axis=0)

    for i in range(s_len // bq):
        r0 = i * bq
        qq = jnp.concatenate([q0_s[r0:r0 + bq, :], q1_s[r0:r0 + bq, :]], axis=0)
        m = l_run = acc = None
        for j in range(i + 1):
            c0 = j * bq
            s = _dot_nt(qq, k_s[c0:c0 + bq, :])
            if j == i:
                s = jnp.where(diag_ok2, s, _NEG)
            m_tile = jnp.max(s, axis=-1, keepdims=True)
            if j == 0:
                m = m_tile
                e = jnp.exp2(s - m)
                l_run = jnp.sum(e, axis=-1, keepdims=True)
                acc = _dot(e.astype(BF16), v_s[c0:c0 + bq, :])
            else:
                m_new = jnp.maximum(m, m_tile)
                alpha = jnp.exp2(m - m_new)
                e = jnp.exp2(s - m_new)
                l_run = l_run * alpha + jnp.sum(e, axis=-1, keepdims=True)
                acc = acc * alpha + _dot(e.astype(BF16), v_s[c0:c0 + bq, :])
                m = m_new
        w0 = 1.0 / l_run[:bq]
        w1 = lam / l_run[bq:]
        o = acc[:bq] * w0 - acc[bq:] * w1
        o = o * lax.rsqrt(jnp.mean(o * o, axis=-1, keepdims=True) + SUBLN_EPS)
        o_ref[r0:r0 + bq, :] = o * (sg_ref[...] * (1.0 - lambda_init))


def _diff_attn(ua3, cosf, sina, sinb, gq, gk, lvec, sg, lambda_init):
    b, s, _ = ua3.shape
    h_a = ua3.shape[2] // (3 * DV_A)
    bq = min(256, s)
    blk = lambda off: pl.BlockSpec((None, s, DV_A), lambda bi, hi: (bi, 0, off + hi))
    const = lambda shape: pl.BlockSpec(shape, lambda bi, hi: (0,) * len(shape))
    return pl.pallas_call(
        functools.partial(_attn_body, lambda_init=lambda_init, bq=bq),
        grid=(b, h_a),
        in_specs=[blk(0), blk(h_a), blk(2 * h_a),
                  const((s, LANES)), const((s, LANES)), const((s, LANES)),
                  const((1, LANES)), const((1, LANES)), const((4, DH_A)), const((1, LANES))],
        out_specs=pl.BlockSpec((None, s, DV_A), lambda bi, hi: (bi, 0, hi)),
        out_shape=jax.ShapeDtypeStruct((b, s, h_a * DV_A), F32),
        scratch_shapes=[pltpu.VMEM((s, LANES), BF16)] * 4,
        compiler_params=pltpu.CompilerParams(
            dimension_semantics=("arbitrary", "arbitrary"), vmem_limit_bytes=VMEM_LIMIT),
        name="diff_attn",
    )(ua3, ua3, ua3, cosf, sina, sinb, gq, gk, lvec, sg)


def _rwkv_body(u_ref, mu_ref, w0_ref, wlu_ref, a0_ref, alu_ref, glu_ref, kkw_ref, ka_ref, rk_ref,
               lg_ref, lb_ref, bd_ref, tri_ref, o_ref, prev_s, st_s, *, c_len):
    tr = u_ref.shape[0]
    d_r = o_ref.shape[1]
    n_pair = d_r // LANES
    n_ch = tr // c_len

    @pl.when(pl.program_id(1) == 0)
    def _():
        prev_s[...] = jnp.zeros_like(prev_s)
        st_s[...] = jnp.zeros_like(st_s)

    u = u_ref[...].astype(F32)
    row = lax.broadcasted_iota(I32, (tr, 1), 0)
    u_prev = jnp.where(row == 0, prev_s[...], pltpu.roll(u, 1, 0))
    prev_s[...] = u[tr - 1:tr, :]
    us = u + (u_prev - u) * mu_ref[...]

    r = us[:, 0:d_r]
    k = us[:, d_r:2 * d_r]
    v = us[:, 2 * d_r:3 * d_r]
    lo_in = us[:, 3 * d_r:3 * d_r + LANES]
    g_in = us[:, 3 * d_r + LANES:3 * d_r + 2 * LANES]

    bd = bd_ref[...]

    def head_sum(x):
        hi, lo = _split(x)
        return _dot(hi, bd) + _dot(lo, bd)

    z = -(w0_ref[...] + _bdot(jnp.tanh(lo_in), wlu_ref[...]))
    softplus = jnp.maximum(z, 0.0) + jnp.log(1.0 + jnp.exp(-jnp.abs(z)))
    logw = -jnp.exp(-softplus - 0.5)
    a = 1.0 / (1.0 + jnp.exp(-(a0_ref[...] + _bdot(lo_in, alu_ref[...]))))
    g = _bdot(1.0 / (1.0 + jnp.exp(-g_in)), glu_ref[...])
    kk_raw = k * kkw_ref[...]
    kk = kk_raw / jnp.maximum(jnp.sqrt(head_sum(kk_raw * kk_raw)), 1e-12)
    k_mod = k * (1.0 + (a - 1.0) * ka_ref[...])

    tri = tri_ref[...]
    w_hi, w_lo = _split(logw)
    cum = _dot(tri, w_hi) + _dot(tri, w_lo)
    last_rows = [cum[(c + 1) * c_len - 1:(c + 1) * c_len, :] for c in range(n_ch)]
    cum_last = jnp.concatenate([jnp.broadcast_to(lr, (c_len, d_r)) for lr in last_rows], axis=0)
    e_neg = jnp.exp(-cum)
    e_rem = jnp.exp(cum_last - cum)
    a_t = -kk * jnp.exp(cum - logw)
    r_t = r * jnp.exp(cum)
    b_raw = kk * a
    b_t = b_raw * e_neg
    k_t = k_mod * e_neg
    b_2 = b_raw * e_rem
    k_2 = k_mod * e_rem

    lane = lax.broadcasted_iota(I32, (1, LANES), 1)
    head0 = lane < RWKV_HEAD
    two_c = 2 * c_len
    rr = lax.broadcasted_iota(I32, (two_c, two_c), 0)
    cc = lax.broadcasted_iota(I32, (two_c, two_c), 1)
    t_i = rr & (c_len - 1)
    s_i = cc & (c_len - 1)
    strict = t_i > s_i
    incl = t_i >= s_i
    eye = rr == cc

    def stack(x, c, j):
        xp = x[c * c_len:(c + 1) * c_len, j * LANES:(j + 1) * LANES]
        return jnp.concatenate([jnp.where(head0, xp, 0.0), jnp.where(head0, 0.0, xp)], axis=0)

    units = [(c, j) for c in range(n_ch) for j in range(n_pair)]
    n_double = int(math.log2(c_len))

    a_s = [stack(a_t, c, j) for c, j in units]
    r_s = [stack(r_t, c, j) for c, j in units]
    v_b = [stack(v, c, j).astype(BF16) for c, j in units]
    gram = [_dot_nt(jnp.concatenate([a_s[i], r_s[i]], axis=0).astype(BF16),
                    jnp.concatenate([stack(b_t, c, j), stack(k_t, c, j)], axis=0).astype(BF16))
            for i, (c, j) in enumerate(units)]
    p = [jnp.where(strict, gm[:two_c, :two_c], 0.0) for gm in gram]
    m_rb = [jnp.where(incl, gm[two_c:, :two_c], 0.0).astype(BF16) for gm in gram]
    vv = [_dot(jnp.concatenate([jnp.where(strict, gm[:two_c, two_c:], 0.0),
                                jnp.where(incl, gm[two_c:, two_c:], 0.0)], axis=0).astype(BF16), vb)
          for gm, vb in zip(gram, v_b)]
    x = [jnp.concatenate([a_s[i], vv[i][:two_c]], axis=1) for i in range(len(units))]
    for it in range(n_double):
        p_b = [pi.astype(BF16) for pi in p]
        x = [xi + _dot(pb, xi.astype(BF16)) for xi, pb in zip(x, p_b)]
        if it + 1 < n_double:
            p = [_dot(pb, pb) for pb in p_b]
    x_b = [xi.astype(BF16) for xi in x]
    mx = [_dot(mb, xb) for mb, xb in zip(m_rb, x_b)]
    bx = [_dot(stack(b_2, c, j).T.astype(BF16), x_b[i]) for i, (c, j) in enumerate(units)]
    kv = [_dot(stack(k_2, c, j).T.astype(BF16), v_b[i]) for i, (c, j) in enumerate(units)]

    y_rows = []
    st = [st_s[j] for j in range(n_pair)]
    for c in range(n_ch):
        p_c = jnp.exp(last_rows[c])
        y_pairs = []
        for j in range(n_pair):
            i = c * n_pair + j
            r_h = (r_s[i] + mx[i][:, :two_c]).astype(BF16)
            y_h = mx[i][:, two_c:] + vv[i][two_c:]
            g_m = (jnp.where(eye, p_c[:, j * LANES:(j + 1) * LANES], 0.0) + bx[i][:, :two_c]).astype(BF16)
            h_m = bx[i][:, two_c:] + kv[i]
            st_b = st[j].astype(BF16)
            y_s = _dot(r_h, st_b) + y_h
            st[j] = _dot(g_m, st_b) + h_m
            y_pairs.append(y_s[:c_len] + y_s[c_len:])
        y_rows.append(jnp.concatenate(y_pairs, axis=1))
    for j in range(n_pair):
        st_s[j] = st[j]
    y = jnp.concatenate(y_rows, axis=0)

    inv_n = 1.0 / RWKV_HEAD
    mu = head_sum(y) * inv_n
    dlt = y - mu
    var = head_sum(dlt * dlt) * inv_n
    yn = dlt * lax.rsqrt(var + LN_X_EPS) * lg_ref[...] + lb_ref[...]
    bonus = head_sum(r * k_mod * rk_ref[...]) * v
    o_ref[...] = (yn + bonus) * g


def _rwkv7(ub3, mu, w0, wlu, a0, alu, glu, kkw, ka, rk, lg, lb, bd, tri, c_len):
    b, s, cols = ub3.shape
    d_r = w0.shape[1]
    tr = tri.shape[0]
    const = lambda a: pl.BlockSpec(a.shape, lambda bi, ci: (0,) * a.ndim)
    params = (mu, w0, wlu, a0, alu, glu, kkw, ka, rk, lg, lb, bd, tri)
    return pl.pallas_call(
        functools.partial(_rwkv_body, c_len=c_len),
        grid=(b, s // tr),
        in_specs=[pl.BlockSpec((None, tr, cols), lambda bi, ci: (bi, ci, 0))]
                 + [const(a) for a in params],
        out_specs=pl.BlockSpec((None, tr, d_r), lambda bi, ci: (bi, ci, 0)),
        out_shape=jax.ShapeDtypeStruct((b, s, d_r), F32),
        scratch_shapes=[pltpu.VMEM((1, cols), F32),
                        pltpu.VMEM((d_r // LANES, LANES, LANES), F32)],
        compiler_params=pltpu.CompilerParams(
            dimension_semantics=("arbitrary", "arbitrary"), vmem_limit_bytes=VMEM_LIMIT),
        name="rwkv7",
    )(ub3, *params)


def _out_body(x_ref, oa_ref, ob_ref, wa_ref, wb_ref, g2_ref, wrh_ref, wrl_ref, br_ref, tril_ref,
              h_ref, t_ref, eid_ref, gate_ref, rank_ref, cnt_ref):
    h = x_ref[...] + _bdot(oa_ref[...], wa_ref[...]) + _bdot(ob_ref[...], wb_ref[...])
    h_ref[...] = h
    t = h * lax.rsqrt(jnp.mean(h * h, axis=-1, keepdims=True) + NORM_EPS) * g2_ref[...]
    t_ref[...] = t.astype(BF16)

    t_hi, t_lo = _split(t)
    wrh = wrh_ref[...]
    lg = _dot(t_hi, wrh) + _dot(t_lo, wrh) + _dot(t_hi, wrl_ref[...]) + br_ref[...]

    lane_i = lax.broadcasted_iota(I32, (1, LANES), 1)
    lane = lane_i.astype(F32)
    big = float(LANES)

    def first_index(mask):
        return jnp.min(jnp.where(mask, lane, big), axis=-1, keepdims=True)

    def masked_softmax(mask):
        m = jnp.max(jnp.where(mask, lg, _NEG), axis=-1, keepdims=True)
        e = jnp.where(mask, jnp.exp(lg - m), 0.0)
        return e / jnp.sum(e, axis=-1, keepdims=True)

    is_g = lane_i < N_GROUPS
    gprob = masked_softmax(is_g)
    p_group = jnp.max(gprob, axis=-1, keepdims=True)
    gsel = first_index(is_g & (gprob == p_group))
    base = N_GROUPS + gsel * EXPERTS_PER_GROUP
    in_e = (lane >= base) & (lane < base + EXPERTS_PER_GROUP)
    eprob = masked_softmax(in_e)
    p1 = jnp.max(eprob, axis=-1, keepdims=True)
    i1 = first_index(in_e & (eprob == p1))
    rest = in_e & (lane != i1)
    p2 = jnp.max(jnp.where(rest, eprob, -1.0), axis=-1, keepdims=True)
    i2 = first_index(rest & (eprob == p2))
    den = p1 + p2
    g1 = p_group * p1 / den
    g2 = p_group * p2 / den
    e1 = i1 - N_GROUPS
    e2 = i2 - N_GROUPS

    oh1 = (lane == e1).astype(F32)
    oh2 = (lane == e2).astype(F32)
    tril = tril_ref[...]
    before1 = _dot(tril, oh1.astype(BF16))
    tot1 = jnp.sum(oh1, axis=0, keepdims=True)
    before2 = _dot(tril, oh2.astype(BF16)) + tot1
    rk1 = jnp.sum(oh1 * before1, axis=-1, keepdims=True)
    rk2 = jnp.sum(oh2 * before2, axis=-1, keepdims=True)
    cnt_ref[...] = tot1 + jnp.sum(oh2, axis=0, keepdims=True)

    sel0 = lane_i == 0
    sel1 = lane_i == 1
    eid_ref[...] = jnp.where(sel0, e1, jnp.where(sel1, e2, 0.0)).astype(I32)
    rank_ref[...] = jnp.where(sel0, rk1, jnp.where(sel1, rk2, 0.0)).astype(I32)
    gate_ref[...] = jnp.where(sel0, g1, jnp.where(sel1, g2, 0.0))


def _out_proj_route(x2, oa, ob, wa, wb, g2, wrh, wrl, br, tril):
    n, d = x2.shape
    tm = tril.shape[0]
    da, db = oa.shape[1], ob.shape[1]
    row = lambda w: pl.BlockSpec((tm, w), lambda i: (i, 0))
    const = lambda a: pl.BlockSpec(a.shape, lambda i: (0,) * a.ndim)
    return pl.pallas_call(
        _out_body,
        grid=(n // tm,),
        in_specs=[row(d), row(da), row(db), const(wa), const(wb), const(g2), const(wrh), const(wrl),
                  const(br), const(tril)],
        out_specs=[row(d), row(d), row(LANES), row(LANES), row(LANES),
                   pl.BlockSpec((None, 1, LANES), lambda i: (i, 0, 0))],
        out_shape=[jax.ShapeDtypeStruct((n, d), F32), jax.ShapeDtypeStruct((n, d), BF16),
                   jax.ShapeDtypeStruct((n, LANES), I32), jax.ShapeDtypeStruct((n, LANES), F32),
                   jax.ShapeDtypeStruct((n, LANES), I32),
                   jax.ShapeDtypeStruct((n // tm, 1, LANES), F32)],
        compiler_params=pltpu.CompilerParams(
            dimension_semantics=("arbitrary",), vmem_limit_bytes=VMEM_LIMIT),
        name="out_proj_route",
    )(x2, oa, ob, wa, wb, g2, wrh, wrl, br, tril)


def _run_copies(g_ref, l_ref, nc_ref, tile, make_copy, op):
    base = tile * N_EXPERTS
    for e in range(N_EXPERTS):
        src0 = l_ref[base + e]
        dst0 = g_ref[base + e]

        def body(c, carry, src0=src0, dst0=dst0):
            off = c * ROW_GROUP
            op(make_copy(pl.multiple_of(src0 + off, ROW_GROUP), pl.multiple_of(dst0 + off, ROW_GROUP)))
            return carry

        lax.fori_loop(0, nc_ref[base + e], body, 0)


def _dispatch_body(g_ref, l_ref, nc_ref, fill_ref, p0_ref, p1_ref, t_ref, xs_ref, buf, zbuf, sem):
    tile = pl.program_id(0)
    lb = buf.shape[0]
    zrows = zbuf.shape[0]
    r = lax.broadcasted_iota(I32, (lb, 1), 0)
    onehot = jnp.logical_or(r == p0_ref[...], r == p1_ref[...]).astype(F32).astype(BF16)
    buf[...] = _dot(onehot, t_ref[...])

    def make_copy(src, dst):
        return pltpu.make_async_copy(buf.at[pl.ds(src, ROW_GROUP)], xs_ref.at[pl.ds(dst, ROW_GROUP)], sem)

    def small_fill(c):
        dst = pl.multiple_of(fill_ref[0] + c * ROW_GROUP, ROW_GROUP)
        return pltpu.make_async_copy(zbuf.at[pl.ds(0, ROW_GROUP)], xs_ref.at[pl.ds(dst, ROW_GROUP)], sem)

    def big_fill(c):
        dst = pl.multiple_of(fill_ref[2] + c * zrows, ROW_GROUP)
        return pltpu.make_async_copy(zbuf, xs_ref.at[pl.ds(dst, zrows)], sem)

    def fills(op):
        lax.fori_loop(0, fill_ref[1], lambda c, carry: (op(small_fill(c)), carry)[1], 0)
        lax.fori_loop(0, fill_ref[3], lambda c, carry: (op(big_fill(c)), carry)[1], 0)

    is_last = tile == pl.num_programs(0) - 1

    @pl.when(is_last)
    def _():
        zbuf[...] = jnp.zeros_like(zbuf)
        fills(lambda cp: cp.start())

    _run_copies(g_ref, l_ref, nc_ref, tile, make_copy, lambda cp: cp.start())
    _run_copies(g_ref, l_ref, nc_ref, tile, make_copy, lambda cp: cp.wait())

    @pl.when(is_last)
    def _():
        fills(lambda cp: cp.wait())


def _dispatch(g_tab, l_tab, nc_tab, fill, p0, p1, t, m_pad, lb, zrows):
    n, d = t.shape
    n_tiles, _, tm = p0.shape
    grid_spec = pltpu.PrefetchScalarGridSpec(
        num_scalar_prefetch=4,
        grid=(n_tiles,),
        in_specs=[
            pl.BlockSpec((None, 1, tm), lambda i, *_: (i, 0, 0)),
            pl.BlockSpec((None, 1, tm), lambda i, *_: (i, 0, 0)),
            pl.BlockSpec((tm, d), lambda i, *_: (i, 0)),
        ],
        out_specs=pl.BlockSpec(memory_space=pl.ANY),
        scratch_shapes=[pltpu.VMEM((lb, d), F32), pltpu.VMEM((zrows, d), F32),
                        pltpu.SemaphoreType.DMA],
    )
    return pl.pallas_call(
        _dispatch_body,
        grid_spec=grid_spec,
        out_shape=jax.ShapeDtypeStruct((m_pad, d), F32),
        compiler_params=pltpu.CompilerParams(
            dimension_semantics=("arbitrary",), vmem_limit_bytes=VMEM_LIMIT),
        name="moe_dispatch",
    )(g_tab, l_tab, nc_tab, fill, p0, p1, t)


def _experts_body(ub_ref, ue_ref, gs_ref, nu_ref, x_ref, wg_ref, wu_ref, wd_ref, y_ref,
                  wg_s, wu_s, wd_s):
    w = pl.program_id(0)
    bm = x_ref.shape[0]

    @pl.when(w < nu_ref[0])
    def _():
        e = ue_ref[w]
        blk = ub_ref[w]
        prev = jnp.maximum(w - 1, 0)

        @pl.when(jnp.logical_or(w == 0, ue_ref[prev] != e))
        def _():
            wg_s[...] = wg_ref[...].astype(BF16)
            wu_s[...] = wu_ref[...].astype(BF16)
            wd_s[...] = wd_ref[...].astype(BF16)

        rows = blk * bm + lax.broadcasted_iota(I32, (bm, 1), 0)
        valid = (rows >= gs_ref[e]) & (rows < gs_ref[e + 1])
        xb = x_ref[...].astype(BF16)
        gt = _dot(xb, wg_s[...])
        up = _dot(xb, wu_s[...])
        act = (gt / (1.0 + jnp.exp(-gt))) * up
        y = jnp.where(valid, _dot(act.astype(BF16), wd_s[...]), 0.0)
        first = jnp.logical_or(w == 0, ub_ref[prev] != blk)

        @pl.when(first)
        def _():
            y_ref[...] = y

        @pl.when(jnp.logical_not(first))
        def _():
            y_ref[...] += y

    @pl.when(jnp.logical_and(w >= nu_ref[0], ub_ref[w] >= nu_ref[1]))
    def _():
        y_ref[...] = jnp.zeros_like(y_ref)


def _experts(unit_blk, unit_exp, gstart, n_units, xs, wg, wu, wd, bm):
    m, d = xs.shape
    de = wg.shape[2]
    max_units = unit_blk.shape[0]
    grid_spec = pltpu.PrefetchScalarGridSpec(
        num_scalar_prefetch=4,
        grid=(max_units,),
        in_specs=[
            pl.BlockSpec((bm, d), lambda w, ub, ue, gs, nu: (ub[w], 0)),
            pl.BlockSpec((None, d, de), lambda w, ub, ue, gs, nu: (ue[w], 0, 0)),
            pl.BlockSpec((None, d, de), lambda w, ub, ue, gs, nu: (ue[w], 0, 0)),
            pl.BlockSpec((None, de, d), lambda w, ub, ue, gs, nu: (ue[w], 0, 0)),
        ],
        out_specs=pl.BlockSpec((bm, d), lambda w, ub, ue, gs, nu: (ub[w], 0)),
        scratch_shapes=[pltpu.VMEM((d, de), BF16), pltpu.VMEM((d, de), BF16),
                        pltpu.VMEM((de, d), BF16)],
    )
    return pl.pallas_call(
        _experts_body,
        grid_spec=grid_spec,
        out_shape=jax.ShapeDtypeStruct((m, d), F32),
        compiler_params=pltpu.CompilerParams(
            dimension_semantics=("arbitrary",), vmem_limit_bytes=VMEM_LIMIT),
        name="moe_experts",
    )(unit_blk, unit_exp, gstart, n_units, xs, wg, wu, wd)


def _combine_body(g_ref, l_ref, nc_ref, p0_ref, p1_ref, y_ref, h_ref, gate_ref, o_ref, ybuf, sem):
    tile = pl.program_id(0)
    lb = ybuf.shape[0]

    @pl.when(tile == 0)
    def _():
        ybuf[...] = jnp.zeros_like(ybuf)

    def make_copy(src, dst):
        return pltpu.make_async_copy(y_ref.at[pl.ds(dst, ROW_GROUP)], ybuf.at[pl.ds(src, ROW_GROUP)], sem)

    _run_copies(g_ref, l_ref, nc_ref, tile, make_copy, lambda cp: cp.start())
    _run_copies(g_ref, l_ref, nc_ref, tile, make_copy, lambda cp: cp.wait())

    yb = ybuf[...].astype(BF16)
    lane = lax.broadcasted_iota(I32, (1, lb), 1)
    oh0 = (lane == p0_ref[...]).astype(F32).astype(BF16)
    oh1 = (lane == p1_ref[...]).astype(F32).astype(BF16)
    gate = gate_ref[...]
    o_ref[...] = h_ref[...] + gate[:, 0:1] * _dot(oh0, yb) + gate[:, 1:2] * _dot(oh1, yb)


def _combine(g_tab, l_tab, nc_tab, p0c, p1c, y, h, gate, lb):
    n, d = h.shape
    tm = n // (g_tab.shape[0] // N_EXPERTS)
    grid_spec = pltpu.PrefetchScalarGridSpec(
        num_scalar_prefetch=3,
        grid=(n // tm,),
        in_specs=[
            pl.BlockSpec((tm, 1), lambda i, *_: (i, 0)),
            pl.BlockSpec((tm, 1), lambda i, *_: (i, 0)),
            pl.BlockSpec(memory_space=pl.ANY),
            pl.BlockSpec((tm, d), lambda i, *_: (i, 0)),
            pl.BlockSpec((tm, LANES), lambda i, *_: (i, 0)),
        ],
        out_specs=pl.BlockSpec((tm, d), lambda i, *_: (i, 0)),
        scratch_shapes=[pltpu.VMEM((lb, d), F32), pltpu.SemaphoreType.DMA],
    )
    return pl.pallas_call(
        _combine_body,
        grid_spec=grid_spec,
        out_shape=jax.ShapeDtypeStruct((n, d), F32),
        compiler_params=pltpu.CompilerParams(
            dimension_semantics=("arbitrary",), vmem_limit_bytes=VMEM_LIMIT),
        name="moe_combine",
    )(g_tab, l_tab, nc_tab, p0c, p1c, y, h, gate)


def _rope_tables(seq):
    half = ROT_DIM // 2
    inv = ROPE_THETA ** (-jnp.arange(0, ROT_DIM, 2, dtype=F32) / ROT_DIM)
    ang = jnp.arange(seq, dtype=F32)[:, None] * inv[None, :]
    cos, sin = jnp.cos(ang), jnp.sin(ang)
    ones = jnp.ones((seq, DH_A - ROT_DIM), F32)
    zeros = jnp.zeros((seq, DH_A - half), F32)
    zeros_h = jnp.zeros((seq, half), F32)
    zeros_r = jnp.zeros((seq, DH_A - ROT_DIM), F32)
    cos_c = jnp.concatenate([cos, cos, ones], axis=1)
    sa_c = jnp.concatenate([-sin, zeros], axis=1)
    sb_c = jnp.concatenate([zeros_h, sin, zeros_r], axis=1)
    tile = lambda t: jnp.concatenate([t, t], axis=1)
    return tile(cos_c), tile(sa_c), tile(sb_c)


def _layer(x, l, lambda_init, ln1_g, w_in, q_norm_g, k_norm_g, lambda_q1, lambda_k1, lambda_q2,
           lambda_k2, subln_g, shift_mu, w0, w_lora_up, a0, a_lora_up, g_lora_up, k_k, k_a, r_k,
           ln_x_g, ln_x_b, w_out, ln2_g, w_router_group, b_router_group, w_router_expert,
           b_router_expert, w_exp_gate, w_exp_up, w_exp_down):
    b, s, d = x.shape
    n = b * s
    d_attn3 = 3 * (d // 2)
    d_attn = d // 2
    d_r = w0.shape[1]
    x2 = x.reshape(n, d)
    row = lambda a: a[l].reshape(1, -1)

    w_in_b = w_in[l].astype(BF16)
    ua, ub = _in_proj(x2, row(ln1_g), w_in_b[:, :d_attn3], w_in_b[:, d_attn3:])

    cosf, sina, sinb = _rope_tables(s)
    tile2 = lambda a: jnp.concatenate([a[l], a[l]]).reshape(1, -1)
    lvec = jnp.stack([lambda_q1[l], lambda_k1[l], lambda_q2[l], lambda_k2[l]])
    o_a = _diff_attn(ua.reshape(b, s, d_attn3), cosf, sina, sinb, tile2(q_norm_g), tile2(k_norm_g),
                     lvec, row(subln_g), lambda_init)

    dl = w_lora_up.shape[1]
    zeros_l = jnp.zeros((dl, d_r), F32)
    wlu = jnp.concatenate([w_lora_up[l], zeros_l], axis=0).astype(BF16)
    alu = jnp.concatenate([zeros_l, a_lora_up[l]], axis=0).astype(BF16)
    hid = jnp.arange(d_r) // RWKV_HEAD
    bd = (hid[:, None] == hid[None, :]).astype(BF16)
    tr = min(RWKV_ROWS, s)
    ti = jnp.arange(tr)
    tri = ((ti[:, None] >= ti[None, :]) & (ti[:, None] // CHUNK == ti[None, :] // CHUNK)).astype(BF16)
    o_b = _rwkv7(ub.reshape(b, s, -1), row(shift_mu), row(w0), wlu, row(a0), alu,
                 g_lora_up[l].astype(BF16), row(k_k), row(k_a), row(r_k), row(ln_x_g), row(ln_x_b),
                 bd, tri, CHUNK)

    w_out_b = w_out[l].astype(BF16)
    wr = jnp.concatenate([w_router_group[l], w_router_expert[l],
                          jnp.zeros((d, LANES - N_GROUPS - N_EXPERTS), F32)], axis=1)
    wr_hi = wr.astype(BF16)
    wr_lo = (wr - wr_hi.astype(F32)).astype(BF16)
    br = jnp.concatenate([b_router_group[l], b_router_expert[l],
                          jnp.zeros((LANES - N_GROUPS - N_EXPERTS,), F32)]).reshape(1, LANES)
    tm_r = min(512, n)
    tril = (jnp.arange(tm_r)[:, None] > jnp.arange(tm_r)[None, :]).astype(BF16)
    h, t, eid, gate, rank, cnt = _out_proj_route(
        x2, o_a.reshape(n, d_attn), o_b.reshape(n, d_r), w_out_b[:d_attn], w_out_b[d_attn:],
        row(ln2_g), wr_hi, wr_lo, br, tril)

    n_tiles = n // tm_r
    cnt8 = (cnt[:, 0, :N_EXPERTS].astype(I32) + ROW_GROUP - 1) // ROW_GROUP * ROW_GROUP
    counts = jnp.sum(cnt8, axis=0)
    gend = jnp.cumsum(counts)
    gstart = jnp.concatenate([jnp.zeros((1,), I32), gend]).astype(I32)
    g_tab = gstart[None, :N_EXPERTS] + jnp.cumsum(cnt8, axis=0) - cnt8
    l_tab = jnp.cumsum(cnt8, axis=1) - cnt8
    onehot = (eid[:, :2, None] == jnp.arange(N_EXPERTS)[None, None, :])
    l_tok = jnp.broadcast_to(l_tab[:, None, None, :], (n_tiles, tm_r, 1, N_EXPERTS))
    lpos = (jnp.sum(jnp.where(onehot, l_tok.reshape(n, 1, N_EXPERTS), 0), axis=-1)
            + rank[:, :2]).astype(I32)
    g_tab = g_tab.reshape(-1).astype(I32)
    l_tab = l_tab.reshape(-1).astype(I32)
    nc_tab = (cnt8 // ROW_GROUP).reshape(-1).astype(I32)

    lb = 2 * tm_r + N_EXPERTS * ROW_GROUP
    m = 2 * n + n_tiles * N_EXPERTS * ROW_GROUP
    bm = min(256, m)
    total = gend[-1]
    n_small = ((-total) % bm) // ROW_GROUP
    big0 = total + n_small * ROW_GROUP
    fill = jnp.stack([total, n_small, big0, (m - big0) // bm]).astype(I32)
    n_blk = m // bm
    max_units = n_blk + N_EXPERTS - 1
    first_blk = gstart[:N_EXPERTS] // bm
    last_blk = jnp.where(counts > 0, (gend - 1) // bm, first_blk)
    units_e = jnp.where(counts > 0, last_blk - first_blk + 1, 0)
    unit_end = jnp.cumsum(units_e)
    unit_start = unit_end - units_e
    n_units = unit_end[-1]
    w_ids = jnp.arange(max_units)
    u_exp = jnp.minimum(jnp.sum((unit_end[None, :] <= w_ids[:, None]).astype(I32), axis=1),
                        N_EXPERTS - 1)
    u_blk = first_blk[u_exp] + (w_ids - unit_start[u_exp])
    last = jnp.maximum(n_units - 1, 0)
    pad = w_ids >= n_units
    used_blk = (total + bm - 1) // bm
    u_exp = jnp.where(pad, u_exp[last], u_exp).astype(I32)
    u_blk = jnp.where(pad, jnp.minimum(used_blk + w_ids - n_units, n_blk - 1), u_blk).astype(I32)
    unit_info = jnp.stack([n_units, used_blk]).astype(I32)

    xs = _dispatch(g_tab, l_tab, nc_tab, fill, lpos[:, 0].reshape(n_tiles, 1, tm_r),
                   lpos[:, 1].reshape(n_tiles, 1, tm_r), t, m, lb, bm)
    y = _experts(u_blk, u_exp, gstart, unit_info, xs,
                 w_exp_gate[l], w_exp_up[l], w_exp_down[l], bm)
    out = _combine(g_tab, l_tab, nc_tab, lpos[:, 0:1], lpos[:, 1:2], y, h, gate, lb)
    return out.reshape(b, s, d)


def kernel(x, ln1_g, w_in, q_norm_g, k_norm_g, lambda_q1, lambda_k1, lambda_q2, lambda_k2, subln_g, shift_mu, w0, w_lora_up, a0, a_lora_up, g_lora_up, k_k, k_a, r_k, ln_x_g, ln_x_b, w_out, ln2_g, w_router_group, b_router_group, w_router_expert, b_router_expert, w_exp_gate, w_exp_up, w_exp_down):
    h = x
    for l in range(ln1_g.shape[0]):
        lambda_init = 0.8 - 0.6 * math.exp(-0.3 * l)
        h = _layer(h, l, lambda_init, ln1_g, w_in, q_norm_g, k_norm_g, lambda_q1, lambda_k1,
                   lambda_q2, lambda_k2, subln_g, shift_mu, w0, w_lora_up, a0, a_lora_up, g_lora_up,
                   k_k, k_a, r_k, ln_x_g, ln_x_b, w_out, ln2_g, w_router_group, b_router_group,
                   w_router_expert, b_router_expert, w_exp_gate, w_exp_up, w_exp_down)
    return h
```

```python
import functools
import math

import jax
import jax.numpy as jnp
from jax import lax
from jax.experimental import pallas as pl
from jax.experimental.pallas import tpu as pltpu

F32 = jnp.float32
BF16 = jnp.bfloat16
I32 = jnp.int32

CHUNK = 64
DH_A = 64
DV_A = 128
ROT_DIM = 16
ROPE_THETA = 500000.0
QK_NORM_EPS = 1e-6
SUBLN_EPS = 1e-5
RWKV_HEAD = 64
LN_X_EPS = 64e-5
RWKV_ROWS = 256
N_GROUPS = 4
EXPERTS_PER_GROUP = 8
N_EXPERTS = 32
NORM_EPS = 1e-6

LANES = 128
VMEM_LIMIT = 56 * 1024 * 1024

_NEG = -1e30


def _dot(a, b):
    return jnp.dot(a, b, preferred_element_type=F32)


def _dot_nt(a, b):
    return lax.dot_general(a, b, (((1,), (1,)), ((), ())), preferred_element_type=F32)


def _bdot(a, b):
    return _dot(a.astype(BF16), b.astype(BF16))


def _split(x):
    hi = x.astype(BF16)
    lo = (x - hi.astype(F32)).astype(BF16)
    return hi, lo


def _in_proj_body(x_ref, g_ref, wa_ref, wb_ref, ua_ref, ub_ref):
    x = x_ref[...]
    ms = jnp.mean(x * x, axis=-1, keepdims=True)
    xn = (x * lax.rsqrt(ms + NORM_EPS) * g_ref[...]).astype(BF16)
    ua_ref[...] = _dot(xn, wa_ref[...])
    ub_ref[...] = _dot(xn, wb_ref[...])


def _in_proj(x2, g, wa, wb):
    n, d = x2.shape
    tm = min(512, n)
    na, nb = wa.shape[1], wb.shape[1]
    return pl.pallas_call(
        _in_proj_body,
        grid=(n // tm,),
        in_specs=[
            pl.BlockSpec((tm, d), lambda i: (i, 0)),
            pl.BlockSpec((1, d), lambda i: (0, 0)),
            pl.BlockSpec((d, na), lambda i: (0, 0)),
            pl.BlockSpec((d, nb), lambda i: (0, 0)),
        ],
        out_specs=[
            pl.BlockSpec((tm, na), lambda i: (i, 0)),
            pl.BlockSpec((tm, nb), lambda i: (i, 0)),
        ],
        out_shape=[jax.ShapeDtypeStruct((n, na), F32), jax.ShapeDtypeStruct((n, nb), F32)],
        compiler_params=pltpu.CompilerParams(
            dimension_semantics=("arbitrary",), vmem_limit_bytes=VMEM_LIMIT),
        name="in_proj",
    )(x2, g, wa, wb)


def _attn_body(q_ref, k_ref, v_ref, cos_ref, sa_ref, sb_ref, gq_ref, gk_ref, l_ref, sg_ref,
               o_ref, q0_s, q1_s, k_s, v_s, *, lambda_init, bq):
    s_len = q_ref.shape[0]
    lane = lax.broadcasted_iota(I32, (1, LANES), 1)
    lo = lane < DH_A

    def prep(t, g):
        t2 = t * t
        s_lo = jnp.sum(jnp.where(lo, t2, 0.0), axis=-1, keepdims=True)
        s_hi = jnp.sum(jnp.where(lo, 0.0, t2), axis=-1, keepdims=True)
        ms = jnp.where(lo, s_lo, s_hi) * (1.0 / DH_A)
        tn = t * lax.rsqrt(ms + QK_NORM_EPS) * g
        return (tn * cos_ref[...] + pltpu.roll(tn, LANES - ROT_DIM // 2, 1) * sa_ref[...]
                + pltpu.roll(tn, ROT_DIM // 2, 1) * sb_ref[...])

    q = prep(q_ref[...], gq_ref[...]) * (DH_A ** -0.5 * math.log2(math.e))
    q0_s[...] = jnp.where(lo, q, 0.0).astype(BF16)
    q1_s[...] = jnp.where(lo, 0.0, q).astype(BF16)
    k_s[...] = prep(k_ref[...], gk_ref[...]).astype(BF16)
    v_s[...] = v_ref[...].astype(BF16)

    l = l_ref[...]
    lam = (jnp.exp(jnp.sum(l[0:1] * l[1:2], axis=-1, keepdims=True))
           - jnp.exp(jnp.sum(l[2:3] * l[3:4], axis=-1, keepdims=True)) + lambda_init)

    chunk_shift = int(math.log2(CHUNK))
    rr = lax.broadcasted_iota(I32, (bq, bq), 0) >> chunk_shift
    cc = lax.broadcasted_iota(I32, (bq, bq), 1) >> chunk_shift
    diag_ok = cc <= rr

    diag_ok2 = jnp.concatenate([diag_ok, diag_ok], axis=0)

    for i in range(s_len // bq):
        r0 = i * bq
        qq = jnp.concatenate([q0_s[r0:r0 + bq, :], q1_s[r0:r0 + bq, :]], axis=0)
        m = l_run = acc = None
        for j in range(i + 1):
            c0 = j * bq
            s = _dot_nt(qq, k_s[c0:c0 + bq, :])
            if j == i:
                s = jnp.where(diag_ok2, s, _NEG)
            m_tile = jnp.max(s, axis=-1, keepdims=True)
            if j == 0:
                m = m_tile
                e = jnp.exp2(s - m)
                l_run = jnp.sum(e, axis=-1, keepdims=True)
                acc = _dot(e.astype(BF16), v_s[c0:c0 + bq, :])
            else:
                m_new = jnp.maximum(m, m_tile)
                alpha = jnp.exp2(m - m_new)
                e = jnp.exp2(s - m_new)
                l_run = l_run * alpha + jnp.sum(e, axis=-1, keepdims=True)
                acc = acc * alpha + _dot(e.astype(BF16), v_s[c0:c0 + bq, :])
                m = m_new
        w0 = 1.0 / l_run[:bq]
        w1 = lam / l_run[bq:]
        o = acc[:bq] * w0 - acc[bq:] * w1
        o = o * lax.rsqrt(jnp.mean(o * o, axis=-1, keepdims=True) + SUBLN_EPS)
        o_ref[r0:r0 + bq, :] = o * (sg_ref[...] * (1.0 - lambda_init))


def _diff_attn(ua3, cosf, sina, sinb, gq, gk, lvec, sg, lambda_init):
    b, s, _ = ua3.shape
    h_a = ua3.shape[2] // (3 * DV_A)
    bq = min(256, s)
    blk = lambda off: pl.BlockSpec((None, s, DV_A), lambda bi, hi: (bi, 0, off + hi))
    const = lambda shape: pl.BlockSpec(shape, lambda bi, hi: (0,) * len(shape))
    return pl.pallas_call(
        functools.partial(_attn_body, lambda_init=lambda_init, bq=bq),
        grid=(b, h_a),
        in_specs=[blk(0), blk(h_a), blk(2 * h_a),
                  const((s, LANES)), const((s, LANES)), const((s, LANES)),
                  const((1, LANES)), const((1, LANES)), const((4, DH_A)), const((1, LANES))],
        out_specs=pl.BlockSpec((None, s, DV_A), lambda bi, hi: (bi, 0, hi)),
        out_shape=jax.ShapeDtypeStruct((b, s, h_a * DV_A), F32),
        scratch_shapes=[pltpu.VMEM((s, LANES), BF16)] * 4,
        compiler_params=pltpu.CompilerParams(
            dimension_semantics=("arbitrary", "arbitrary"), vmem_limit_bytes=VMEM_LIMIT),
        name="diff_attn",
    )(ua3, ua3, ua3, cosf, sina, sinb, gq, gk, lvec, sg)


def _rwkv_body(u_ref, mu_ref, w0_ref, wlu_ref, a0_ref, alu_ref, glu_ref, kkw_ref, ka_ref, rk_ref,
               lg_ref, lb_ref, bd_ref, tri_ref, o_ref, prev_s, st_s, *, c_len):
    tr = u_ref.shape[0]
    d_r = o_ref.shape[1]
    n_pair = d_r // LANES
    n_ch = tr // c_len

    @pl.when(pl.program_id(1) == 0)
    def _():
        prev_s[...] = jnp.zeros_like(prev_s)
        st_s[...] = jnp.zeros_like(st_s)

    u = u_ref[...]
    row = lax.broadcasted_iota(I32, (tr, 1), 0)
    u_prev = jnp.where(row == 0, prev_s[...], pltpu.roll(u, 1, 0))
    prev_s[...] = u[tr - 1:tr, :]
    us = u + (u_prev - u) * mu_ref[...]

    r = us[:, 0:d_r]
    k = us[:, d_r:2 * d_r]
    v = us[:, 2 * d_r:3 * d_r]
    lo_in = us[:, 3 * d_r:3 * d_r + LANES]
    g_in = us[:, 3 * d_r + LANES:3 * d_r + 2 * LANES]

    bd = bd_ref[...]

    def head_sum(x):
        hi, lo = _split(x)
        return _dot(hi, bd) + _dot(lo, bd)

    z = -(w0_ref[...] + _bdot(jnp.tanh(lo_in), wlu_ref[...]))
    softplus = jnp.maximum(z, 0.0) + jnp.log(1.0 + jnp.exp(-jnp.abs(z)))
    logw = -jnp.exp(-softplus - 0.5)
    a = 1.0 / (1.0 + jnp.exp(-(a0_ref[...] + _bdot(lo_in, alu_ref[...]))))
    g = _bdot(1.0 / (1.0 + jnp.exp(-g_in)), glu_ref[...])
    kk_raw = k * kkw_ref[...]
    kk = kk_raw / jnp.maximum(jnp.sqrt(head_sum(kk_raw * kk_raw)), 1e-12)
    k_mod = k * (1.0 + (a - 1.0) * ka_ref[...])

    tri = tri_ref[...]
    w_hi, w_lo = _split(logw)
    cum = _dot(tri, w_hi) + _dot(tri, w_lo)
    last_rows = [cum[(c + 1) * c_len - 1:(c + 1) * c_len, :] for c in range(n_ch)]
    cum_last = jnp.concatenate([jnp.broadcast_to(lr, (c_len, d_r)) for lr in last_rows], axis=0)
    e_neg = jnp.exp(-cum)
    e_rem = jnp.exp(cum_last - cum)
    a_t = -kk * jnp.exp(cum - logw)
    r_t = r * jnp.exp(cum)
    b_raw = kk * a
    b_t = b_raw * e_neg
    k_t = k_mod * e_neg
    b_2 = b_raw * e_rem
    k_2 = k_mod * e_rem

    lane = lax.broadcasted_iota(I32, (1, LANES), 1)
    head0 = lane < RWKV_HEAD
    two_c = 2 * c_len
    rr = lax.broadcasted_iota(I32, (two_c, two_c), 0)
    cc = lax.broadcasted_iota(I32, (two_c, two_c), 1)
    t_i = rr & (c_len - 1)
    s_i = cc & (c_len - 1)
    strict = t_i > s_i
    incl = t_i >= s_i
    eye = rr == cc

    def stack(x, c, j):
        xp = x[c * c_len:(c + 1) * c_len, j * LANES:(j + 1) * LANES]
        return jnp.concatenate([jnp.where(head0, xp, 0.0), jnp.where(head0, 0.0, xp)], axis=0)

    units = [(c, j) for c in range(n_ch) for j in range(n_pair)]
    n_double = int(math.log2(c_len))

    a_s = [stack(a_t, c, j) for c, j in units]
    r_s = [stack(r_t, c, j) for c, j in units]
    v_b = [stack(v, c, j).astype(BF16) for c, j in units]
    gram = [_dot_nt(jnp.concatenate([a_s[i], r_s[i]], axis=0).astype(BF16),
                    jnp.concatenate([stack(b_t, c, j), stack(k_t, c, j)], axis=0).astype(BF16))
            for i, (c, j) in enumerate(units)]
    p = [jnp.where(strict, gm[:two_c, :two_c], 0.0) for gm in gram]
    m_rb = [jnp.where(incl, gm[two_c:, :two_c], 0.0).astype(BF16) for gm in gram]
    vv = [_dot(jnp.concatenate([jnp.where(strict, gm[:two_c, two_c:], 0.0),
                                jnp.where(incl, gm[two_c:, two_c:], 0.0)], axis=0).astype(BF16), vb)
          for gm, vb in zip(gram, v_b)]
    x = [jnp.concatenate([a_s[i], vv[i][:two_c]], axis=1) for i in range(len(units))]
    for it in range(n_double):
        p_b = [pi.astype(BF16) for pi in p]
        x = [xi + _dot(pb, xi.astype(BF16)) for xi, pb in zip(x, p_b)]
        if it + 1 < n_double:
            p = [_dot(pb, pb) for pb in p_b]
    x_b = [xi.astype(BF16) for xi in x]
    mx = [_dot(mb, xb) for mb, xb in zip(m_rb, x_b)]
    bx = [_dot(stack(b_2, c, j).T.astype(BF16), x_b[i]) for i, (c, j) in enumerate(units)]
    kv = [_dot(stack(k_2, c, j).T.astype(BF16), v_b[i]) for i, (c, j) in enumerate(units)]

    y_rows = []
    st = [st_s[j] for j in range(n_pair)]
    for c in range(n_ch):
        p_c = jnp.exp(last_rows[c])
        y_pairs = []
        for j in range(n_pair):
            i = c * n_pair + j
            r_h = (r_s[i] + mx[i][:, :two_c]).astype(BF16)
            y_h = mx[i][:, two_c:] + vv[i][two_c:]
            g_m = (jnp.where(eye, p_c[:, j * LANES:(j + 1) * LANES], 0.0) + bx[i][:, :two_c]).astype(BF16)
            h_m = bx[i][:, two_c:] + kv[i]
            st_b = st[j].astype(BF16)
            y_s = _dot(r_h, st_b) + y_h
            st[j] = _dot(g_m, st_b) + h_m
            y_pairs.append(y_s[:c_len] + y_s[c_len:])
        y_rows.append(jnp.concatenate(y_pairs, axis=1))
    for j in range(n_pair):
        st_s[j] = st[j]
    y = jnp.concatenate(y_rows, axis=0)

    inv_n = 1.0 / RWKV_HEAD
    mu = head_sum(y) * inv_n
    dlt = y - mu
    var = head_sum(dlt * dlt) * inv_n
    yn = dlt * lax.rsqrt(var + LN_X_EPS) * lg_ref[...] + lb_ref[...]
    bonus = head_sum(r * k_mod * rk_ref[...]) * v
    o_ref[...] = (yn + bonus) * g


def _rwkv7(ub3, mu, w0, wlu, a0, alu, glu, kkw, ka, rk, lg, lb, bd, tri, c_len):
    b, s, cols = ub3.shape
    d_r = w0.shape[1]
    tr = tri.shape[0]
    const = lambda a: pl.BlockSpec(a.shape, lambda bi, ci: (0,) * a.ndim)
    params = (mu, w0, wlu, a0, alu, glu, kkw, ka, rk, lg, lb, bd, tri)
    return pl.pallas_call(
        functools.partial(_rwkv_body, c_len=c_len),
        grid=(b, s // tr),
        in_specs=[pl.BlockSpec((None, tr, cols), lambda bi, ci: (bi, ci, 0))]
                 + [const(a) for a in params],
        out_specs=pl.BlockSpec((None, tr, d_r), lambda bi, ci: (bi, ci, 0)),
        out_shape=jax.ShapeDtypeStruct((b, s, d_r), F32),
        scratch_shapes=[pltpu.VMEM((1, cols), F32),
                        pltpu.VMEM((d_r // LANES, LANES, LANES), F32)],
        compiler_params=pltpu.CompilerParams(
            dimension_semantics=("arbitrary", "arbitrary"), vmem_limit_bytes=VMEM_LIMIT),
        name="rwkv7",
    )(ub3, *params)


def _out_body(x_ref, oa_ref, ob_ref, wa_ref, wb_ref, g2_ref, wrh_ref, wrl_ref, br_ref, tril_ref,
              h_ref, t_ref, eid_ref, gate_ref, rank_ref, cnt_ref, run_s):
    tm = x_ref.shape[0]

    @pl.when(pl.program_id(0) == 0)
    def _():
        run_s[...] = jnp.zeros_like(run_s)

    h = x_ref[...] + _bdot(oa_ref[...], wa_ref[...]) + _bdot(ob_ref[...], wb_ref[...])
    h_ref[...] = h
    t = h * lax.rsqrt(jnp.mean(h * h, axis=-1, keepdims=True) + NORM_EPS) * g2_ref[...]
    t_ref[...] = t

    t_hi, t_lo = _split(t)
    wrh = wrh_ref[...]
    lg = _dot(t_hi, wrh) + _dot(t_lo, wrh) + _dot(t_hi, wrl_ref[...]) + br_ref[...]

    lane_i = lax.broadcasted_iota(I32, (1, LANES), 1)
    lane = lane_i.astype(F32)
    big = float(LANES)

    def first_index(mask):
        return jnp.min(jnp.where(mask, lane, big), axis=-1, keepdims=True)

    def masked_softmax(mask):
        m = jnp.max(jnp.where(mask, lg, _NEG), axis=-1, keepdims=True)
        e = jnp.where(mask, jnp.exp(lg - m), 0.0)
        return e / jnp.sum(e, axis=-1, keepdims=True)

    is_g = lane_i < N_GROUPS
    gprob = masked_softmax(is_g)
    p_group = jnp.max(gprob, axis=-1, keepdims=True)
    gsel = first_index(is_g & (gprob == p_group))
    base = N_GROUPS + gsel * EXPERTS_PER_GROUP
    in_e = (lane >= base) & (lane < base + EXPERTS_PER_GROUP)
    eprob = masked_softmax(in_e)
    p1 = jnp.max(eprob, axis=-1, keepdims=True)
    i1 = first_index(in_e & (eprob == p1))
    rest = in_e & (lane != i1)
    p2 = jnp.max(jnp.where(rest, eprob, -1.0), axis=-1, keepdims=True)
    i2 = first_index(rest & (eprob == p2))
    den = p1 + p2
    g1 = p_group * p1 / den
    g2 = p_group * p2 / den
    e1 = i1 - N_GROUPS
    e2 = i2 - N_GROUPS

    oh1 = (lane == e1).astype(F32)
    oh2 = (lane == e2).astype(F32)
    tril = tril_ref[...]
    before1 = _dot(tril, oh1.astype(BF16))
    tot1 = jnp.sum(oh1, axis=0, keepdims=True)
    before2 = _dot(tril, oh2.astype(BF16)) + tot1
    run = run_s[...]
    rk1 = jnp.sum(oh1 * (before1 + run), axis=-1, keepdims=True)
    rk2 = jnp.sum(oh2 * (before2 + run), axis=-1, keepdims=True)
    run = run + tot1 + jnp.sum(oh2, axis=0, keepdims=True)
    run_s[...] = run
    cnt_ref[...] = run

    sel0 = lane_i == 0
    sel1 = lane_i == 1
    eid_ref[...] = jnp.where(sel0, e1, jnp.where(sel1, e2, 0.0)).astype(I32)
    rank_ref[...] = jnp.where(sel0, rk1, jnp.where(sel1, rk2, 0.0)).astype(I32)
    gate_ref[...] = jnp.where(sel0, g1, jnp.where(sel1, g2, 0.0))


def _out_proj_route(x2, oa, ob, wa, wb, g2, wrh, wrl, br, tril):
    n, d = x2.shape
    tm = tril.shape[0]
    da, db = oa.shape[1], ob.shape[1]
    row = lambda w: pl.BlockSpec((tm, w), lambda i: (i, 0))
    const = lambda a: pl.BlockSpec(a.shape, lambda i: (0,) * a.ndim)
    return pl.pallas_call(
        _out_body,
        grid=(n // tm,),
        in_specs=[row(d), row(da), row(db), const(wa), const(wb), const(g2), const(wrh), const(wrl),
                  const(br), const(tril)],
        out_specs=[row(d), row(d), row(LANES), row(LANES), row(LANES),
                   pl.BlockSpec((1, LANES), lambda i: (0, 0))],
        out_shape=[jax.ShapeDtypeStruct((n, d), F32), jax.ShapeDtypeStruct((n, d), F32),
                   jax.ShapeDtypeStruct((n, LANES), I32), jax.ShapeDtypeStruct((n, LANES), F32),
                   jax.ShapeDtypeStruct((n, LANES), I32), jax.ShapeDtypeStruct((1, LANES), F32)],
        scratch_shapes=[pltpu.VMEM((1, LANES), F32)],
        compiler_params=pltpu.CompilerParams(
            dimension_semantics=("arbitrary",), vmem_limit_bytes=VMEM_LIMIT),
        name="out_proj_route",
    )(x2, oa, ob, wa, wb, g2, wrh, wrl, br, tril)


def _dispatch_body(d0_ref, d1_ref, t_ref, xs_ref, sem):
    tm = d0_ref.shape[0]

    def copy(i, d_ref):
        return pltpu.make_async_copy(t_ref.at[pl.ds(i, 1)], xs_ref.at[pl.ds(d_ref[i], 1)], sem)

    def start(i, carry):
        copy(i, d0_ref).start(priority=0)
        copy(i, d1_ref).start(priority=1)
        return carry

    lax.fori_loop(0, tm, start, 0, unroll=8)
    for _ in range(2):
        pltpu.make_async_copy(t_ref, xs_ref.at[pl.ds(0, tm)], sem).wait()


def _dispatch(d0, d1, t):
    n, d = t.shape
    tm = min(512, n)
    smem = pl.BlockSpec((tm,), lambda i: (i,), memory_space=pltpu.SMEM)
    return pl.pallas_call(
        _dispatch_body,
        grid=(n // tm,),
        in_specs=[smem, smem, pl.BlockSpec((tm, d), lambda i: (i, 0))],
        out_specs=pl.BlockSpec(memory_space=pl.ANY),
        out_shape=jax.ShapeDtypeStruct((2 * n, d), t.dtype),
        scratch_shapes=[pltpu.SemaphoreType.DMA],
        compiler_params=pltpu.CompilerParams(dimension_semantics=("arbitrary",)),
        name="moe_dispatch",
    )(d0, d1, t)


def _experts_body(ub_ref, ue_ref, gs_ref, nu_ref, x_ref, wg_ref, wu_ref, wd_ref, y_ref,
                  wg_s, wu_s, wd_s):
    w = pl.program_id(0)
    bm = x_ref.shape[0]

    @pl.when(w < nu_ref[0])
    def _():
        e = ue_ref[w]
        blk = ub_ref[w]
        prev = jnp.maximum(w - 1, 0)

        @pl.when(jnp.logical_or(w == 0, ue_ref[prev] != e))
        def _():
            wg_s[...] = wg_ref[...].astype(BF16)
            wu_s[...] = wu_ref[...].astype(BF16)
            wd_s[...] = wd_ref[...].astype(BF16)

        rows = blk * bm + lax.broadcasted_iota(I32, (bm, 1), 0)
        valid = (rows >= gs_ref[e]) & (rows < gs_ref[e + 1])
        xb = x_ref[...].astype(BF16)
        gt = _dot(xb, wg_s[...])
        up = _dot(xb, wu_s[...])
        act = (gt / (1.0 + jnp.exp(-gt))) * up
        y = jnp.where(valid, _dot(act.astype(BF16), wd_s[...]), 0.0)
        first = jnp.logical_or(w == 0, ub_ref[prev] != blk)

        @pl.when(first)
        def _():
            y_ref[...] = y

        @pl.when(jnp.logical_not(first))
        def _():
            y_ref[...] += y


def _experts(unit_blk, unit_exp, gstart, n_units, xs, wg, wu, wd, bm):
    m, d = xs.shape
    de = wg.shape[2]
    max_units = unit_blk.shape[0]
    grid_spec = pltpu.PrefetchScalarGridSpec(
        num_scalar_prefetch=4,
        grid=(max_units,),
        in_specs=[
            pl.BlockSpec((bm, d), lambda w, ub, ue, gs, nu: (ub[w], 0)),
            pl.BlockSpec((None, d, de), lambda w, ub, ue, gs, nu: (ue[w], 0, 0)),
            pl.BlockSpec((None, d, de), lambda w, ub, ue, gs, nu: (ue[w], 0, 0)),
            pl.BlockSpec((None, de, d), lambda w, ub, ue, gs, nu: (ue[w], 0, 0)),
        ],
        out_specs=pl.BlockSpec((bm, d), lambda w, ub, ue, gs, nu: (ub[w], 0)),
        scratch_shapes=[pltpu.VMEM((d, de), BF16), pltpu.VMEM((d, de), BF16),
                        pltpu.VMEM((de, d), BF16)],
    )
    return pl.pallas_call(
        _experts_body,
        grid_spec=grid_spec,
        out_shape=jax.ShapeDtypeStruct((m, d), F32),
        compiler_params=pltpu.CompilerParams(
            dimension_semantics=("arbitrary",), vmem_limit_bytes=VMEM_LIMIT),
        name="moe_experts",
    )(unit_blk, unit_exp, gstart, n_units, xs, wg, wu, wd)


def _combine_body(d0_ref, d1_ref, y_ref, h_ref, gate_ref, o_ref, b0, b1, sem):
    tm = d0_ref.shape[0]

    def copy(i, d_ref, buf):
        return pltpu.make_async_copy(y_ref.at[pl.ds(d_ref[i], 1)], buf.at[pl.ds(i, 1)], sem)

    def start(i, carry):
        copy(i, d0_ref, b0).start(priority=0)
        copy(i, d1_ref, b1).start(priority=1)
        return carry

    lax.fori_loop(0, tm, start, 0, unroll=8)
    for buf in (b0, b1):
        pltpu.make_async_copy(y_ref.at[pl.ds(0, tm)], buf, sem).wait()
    gate = gate_ref[...]
    o_ref[...] = h_ref[...] + b0[...] * gate[:, 0:1] + b1[...] * gate[:, 1:2]


def _combine(d0, d1, y, h, gate):
    n, d = h.shape
    tm = min(512, n)
    smem = pl.BlockSpec((tm,), lambda i: (i,), memory_space=pltpu.SMEM)
    return pl.pallas_call(
        _combine_body,
        grid=(n // tm,),
        in_specs=[smem, smem, pl.BlockSpec(memory_space=pl.ANY),
                  pl.BlockSpec((tm, d), lambda i: (i, 0)),
                  pl.BlockSpec((tm, LANES), lambda i: (i, 0))],
        out_specs=pl.BlockSpec((tm, d), lambda i: (i, 0)),
        out_shape=jax.ShapeDtypeStruct((n, d), F32),
        scratch_shapes=[pltpu.VMEM((tm, d), F32), pltpu.VMEM((tm, d), F32), pltpu.SemaphoreType.DMA],
        compiler_params=pltpu.CompilerParams(
            dimension_semantics=("arbitrary",), vmem_limit_bytes=VMEM_LIMIT),
        name="moe_combine",
    )(d0, d1, y, h, gate)


def _rope_tables(seq):
    half = ROT_DIM // 2
    inv = ROPE_THETA ** (-jnp.arange(0, ROT_DIM, 2, dtype=F32) / ROT_DIM)
    ang = jnp.arange(seq, dtype=F32)[:, None] * inv[None, :]
    cos, sin = jnp.cos(ang), jnp.sin(ang)
    ones = jnp.ones((seq, DH_A - ROT_DIM), F32)
    zeros = jnp.zeros((seq, DH_A - half), F32)
    zeros_h = jnp.zeros((seq, half), F32)
    zeros_r = jnp.zeros((seq, DH_A - ROT_DIM), F32)
    cos_c = jnp.concatenate([cos, cos, ones], axis=1)
    sa_c = jnp.concatenate([-sin, zeros], axis=1)
    sb_c = jnp.concatenate([zeros_h, sin, zeros_r], axis=1)
    tile = lambda t: jnp.concatenate([t, t], axis=1)
    return tile(cos_c), tile(sa_c), tile(sb_c)


def _layer(x, l, lambda_init, ln1_g, w_in, q_norm_g, k_norm_g, lambda_q1, lambda_k1, lambda_q2,
           lambda_k2, subln_g, shift_mu, w0, w_lora_up, a0, a_lora_up, g_lora_up, k_k, k_a, r_k,
           ln_x_g, ln_x_b, w_out, ln2_g, w_router_group, b_router_group, w_router_expert,
           b_router_expert, w_exp_gate, w_exp_up, w_exp_down):
    b, s, d = x.shape
    n = b * s
    d_attn3 = 3 * (d // 2)
    d_attn = d // 2
    d_r = w0.shape[1]
    x2 = x.reshape(n, d)
    row = lambda a: a[l].reshape(1, -1)

    w_in_b = w_in[l].astype(BF16)
    ua, ub = _in_proj(x2, row(ln1_g), w_in_b[:, :d_attn3], w_in_b[:, d_attn3:])

    cosf, sina, sinb = _rope_tables(s)
    tile2 = lambda a: jnp.concatenate([a[l], a[l]]).reshape(1, -1)
    lvec = jnp.stack([lambda_q1[l], lambda_k1[l], lambda_q2[l], lambda_k2[l]])
    o_a = _diff_attn(ua.reshape(b, s, d_attn3), cosf, sina, sinb, tile2(q_norm_g), tile2(k_norm_g),
                     lvec, row(subln_g), lambda_init)

    dl = w_lora_up.shape[1]
    zeros_l = jnp.zeros((dl, d_r), F32)
    wlu = jnp.concatenate([w_lora_up[l], zeros_l], axis=0).astype(BF16)
    alu = jnp.concatenate([zeros_l, a_lora_up[l]], axis=0).astype(BF16)
    hid = jnp.arange(d_r) // RWKV_HEAD
    bd = (hid[:, None] == hid[None, :]).astype(BF16)
    tr = min(RWKV_ROWS, s)
    ti = jnp.arange(tr)
    tri = ((ti[:, None] >= ti[None, :]) & (ti[:, None] // CHUNK == ti[None, :] // CHUNK)).astype(BF16)
    o_b = _rwkv7(ub.reshape(b, s, -1), row(shift_mu), row(w0), wlu, row(a0), alu,
                 g_lora_up[l].astype(BF16), row(k_k), row(k_a), row(r_k), row(ln_x_g), row(ln_x_b),
                 bd, tri, CHUNK)

    w_out_b = w_out[l].astype(BF16)
    wr = jnp.concatenate([w_router_group[l], w_router_expert[l],
                          jnp.zeros((d, LANES - N_GROUPS - N_EXPERTS), F32)], axis=1)
    wr_hi = wr.astype(BF16)
    wr_lo = (wr - wr_hi.astype(F32)).astype(BF16)
    br = jnp.concatenate([b_router_group[l], b_router_expert[l],
                          jnp.zeros((LANES - N_GROUPS - N_EXPERTS,), F32)]).reshape(1, LANES)
    tm_r = min(512, n)
    tril = (jnp.arange(tm_r)[:, None] > jnp.arange(tm_r)[None, :]).astype(BF16)
    h, t, eid, gate, rank, cnt = _out_proj_route(
        x2, o_a.reshape(n, d_attn), o_b.reshape(n, d_r), w_out_b[:d_attn], w_out_b[d_attn:],
        row(ln2_g), wr_hi, wr_lo, br, tril)

    counts = cnt[0, :N_EXPERTS].astype(I32)
    gend = jnp.cumsum(counts)
    gstart = jnp.concatenate([jnp.zeros((1,), I32), gend]).astype(I32)
    onehot = (eid[:, :2, None] == jnp.arange(N_EXPERTS)[None, None, :])
    dest = jnp.sum(jnp.where(onehot, gstart[:N_EXPERTS], 0), axis=-1) + rank[:, :2]
    d0 = dest[:, 0].astype(I32)
    d1 = dest[:, 1].astype(I32)

    m = 2 * n
    bm = min(256, m)
    n_blk = m // bm
    max_units = n_blk + N_EXPERTS - 1
    first_blk = gstart[:N_EXPERTS] // bm
    last_blk = jnp.where(counts > 0, (gend - 1) // bm, first_blk)
    units_e = jnp.where(counts > 0, last_blk - first_blk + 1, 0)
    unit_end = jnp.cumsum(units_e)
    unit_start = unit_end - units_e
    n_units = unit_end[-1]
    w_ids = jnp.arange(max_units)
    u_exp = jnp.minimum(jnp.sum((unit_end[None, :] <= w_ids[:, None]).astype(I32), axis=1),
                        N_EXPERTS - 1)
    u_blk = first_blk[u_exp] + (w_ids - unit_start[u_exp])
    last = jnp.maximum(n_units - 1, 0)
    pad = w_ids >= n_units
    u_exp = jnp.where(pad, u_exp[last], u_exp).astype(I32)
    u_blk = jnp.where(pad, u_blk[last], u_blk).astype(I32)

    xs = _dispatch(d0, d1, t)
    y = _experts(u_blk, u_exp, gstart, n_units.reshape(1).astype(I32), xs,
                 w_exp_gate[l], w_exp_up[l], w_exp_down[l], bm)
    out = _combine(d0, d1, y, h, gate)
    return out.reshape(b, s, d)


def kernel(x, ln1_g, w_in, q_norm_g, k_norm_g, lambda_q1, lambda_k1, lambda_q2, lambda_k2, subln_g, shift_mu, w0, w_lora_up, a0, a_lora_up, g_lora_up, k_k, k_a, r_k, ln_x_g, ln_x_b, w_out, ln2_g, w_router_group, b_router_group, w_router_expert, b_router_expert, w_exp_gate, w_exp_up, w_exp_down):
    h = x
    for l in range(ln1_g.shape[0]):
        lambda_init = 0.8 - 0.6 * math.exp(-0.3 * l)
        h = _layer(h, l, lambda_init, ln1_g, w_in, q_norm_g, k_norm_g, lambda_q1, lambda_k1,
                   lambda_q2, lambda_k2, subln_g, shift_mu, w0, w_lora_up, a0, a_lora_up, g_lora_up,
                   k_k, k_a, r_k, ln_x_g, ln_x_b, w_out, ln2_g, w_router_group, b_router_group,
                   w_router_expert, b_router_expert, w_exp_gate, w_exp_up, w_exp_down)
    return h
```

```python
import functools
import math

import jax
import jax.numpy as jnp
from jax import lax
from jax.experimental import pallas as pl
from jax.experimental.pallas import tpu as pltpu

F32 = jnp.float32
BF16 = jnp.bfloat16
I32 = jnp.int32

CHUNK = 64
DH_A = 64
DV_A = 128
ROT_DIM = 16
ROPE_THETA = 500000.0
QK_NORM_EPS = 1e-6
SUBLN_EPS = 1e-5
RWKV_HEAD = 64
LN_X_EPS = 64e-5
RWKV_ROWS = 256
N_GROUPS = 4
EXPERTS_PER_GROUP = 8
N_EXPERTS = 32
NORM_EPS = 1e-6

LANES = 128
VMEM_LIMIT = 56 * 1024 * 1024

_NEG = -1e30


def _dot(a, b):
    return jnp.dot(a, b, preferred_element_type=F32)


def _dot_nt(a, b):
    return lax.dot_general(a, b, (((1,), (1,)), ((), ())), preferred_element_type=F32)


def _bdot(a, b):
    return _dot(a.astype(BF16), b.astype(BF16))


def _split(x):
    hi = x.astype(BF16)
    lo = (x - hi.astype(F32)).astype(BF16)
    return hi, lo


def _in_proj_body(x_ref, g_ref, wa_ref, wb_ref, ua_ref, ub_ref):
    x = x_ref[...]
    ms = jnp.mean(x * x, axis=-1, keepdims=True)
    xn = (x * lax.rsqrt(ms + NORM_EPS) * g_ref[...]).astype(BF16)
    ua_ref[...] = _dot(xn, wa_ref[...])
    ub_ref[...] = _dot(xn, wb_ref[...])


def _in_proj(x2, g, wa, wb):
    n, d = x2.shape
    tm = min(512, n)
    na, nb = wa.shape[1], wb.shape[1]
    return pl.pallas_call(
        _in_proj_body,
        grid=(n // tm,),
        in_specs=[
            pl.BlockSpec((tm, d), lambda i: (i, 0)),
            pl.BlockSpec((1, d), lambda i: (0, 0)),
            pl.BlockSpec((d, na), lambda i: (0, 0)),
            pl.BlockSpec((d, nb), lambda i: (0, 0)),
        ],
        out_specs=[
            pl.BlockSpec((tm, na), lambda i: (i, 0)),
            pl.BlockSpec((tm, nb), lambda i: (i, 0)),
        ],
        out_shape=[jax.ShapeDtypeStruct((n, na), F32), jax.ShapeDtypeStruct((n, nb), F32)],
        compiler_params=pltpu.CompilerParams(
            dimension_semantics=("arbitrary",), vmem_limit_bytes=VMEM_LIMIT),
        name="in_proj",
    )(x2, g, wa, wb)


def _attn_body(q_ref, k_ref, v_ref, cos_ref, sa_ref, sb_ref, gq_ref, gk_ref, l_ref, sg_ref,
               o_ref, q0_s, q1_s, k_s, v_s, *, lambda_init, bq):
    s_len = q_ref.shape[0]
    lane = lax.broadcasted_iota(I32, (1, LANES), 1)
    lo = lane < DH_A

    def prep(t, g):
        t2 = t * t
        s_lo = jnp.sum(jnp.where(lo, t2, 0.0), axis=-1, keepdims=True)
        s_hi = jnp.sum(jnp.where(lo, 0.0, t2), axis=-1, keepdims=True)
        ms = jnp.where(lo, s_lo, s_hi) * (1.0 / DH_A)
        tn = t * lax.rsqrt(ms + QK_NORM_EPS) * g
        return (tn * cos_ref[...] + pltpu.roll(tn, LANES - ROT_DIM // 2, 1) * sa_ref[...]
                + pltpu.roll(tn, ROT_DIM // 2, 1) * sb_ref[...])

    q = prep(q_ref[...], gq_ref[...]) * (DH_A ** -0.5 * math.log2(math.e))
    q0_s[...] = jnp.where(lo, q, 0.0).astype(BF16)
    q1_s[...] = jnp.where(lo, 0.0, q).astype(BF16)
    k_s[...] = prep(k_ref[...], gk_ref[...]).astype(BF16)
    v_s[...] = v_ref[...].astype(BF16)

    l = l_ref[...]
    lam = (jnp.exp(jnp.sum(l[0:1] * l[1:2], axis=-1, keepdims=True))
           - jnp.exp(jnp.sum(l[2:3] * l[3:4], axis=-1, keepdims=True)) + lambda_init)

    chunk_shift = int(math.log2(CHUNK))
    rr = lax.broadcasted_iota(I32, (bq, bq), 0) >> chunk_shift
    cc = lax.broadcasted_iota(I32, (bq, bq), 1) >> chunk_shift
    diag_ok = cc <= rr

    diag_ok2 = jnp.concatenate([diag_ok, diag_ok], axis=0)

    for i in range(s_len // bq):
        r0 = i * bq
        qq = jnp.concatenate([q0_s[r0:r0 + bq, :], q1_s[r0:r0 + bq, :]], axis=0)
        m = l_run = acc = None
        for j in range(i + 1):
            c0 = j * bq
            s = _dot_nt(qq, k_s[c0:c0 + bq, :])
            if j == i:
                s = jnp.where(diag_ok2, s, _NEG)
            m_tile = jnp.max(s, axis=-1, keepdims=True)
            if j == 0:
                m = m_tile
                e = jnp.exp2(s - m)
                l_run = jnp.sum(e, axis=-1, keepdims=True)
                acc = _dot(e.astype(BF16), v_s[c0:c0 + bq, :])
            else:
                m_new = jnp.maximum(m, m_tile)
                alpha = jnp.exp2(m - m_new)
                e = jnp.exp2(s - m_new)
                l_run = l_run * alpha + jnp.sum(e, axis=-1, keepdims=True)
                acc = acc * alpha + _dot(e.astype(BF16), v_s[c0:c0 + bq, :])
                m = m_new
        w0 = 1.0 / l_run[:bq]
        w1 = lam / l_run[bq:]
        o = acc[:bq] * w0 - acc[bq:] * w1
        o = o * lax.rsqrt(jnp.mean(o * o, axis=-1, keepdims=True) + SUBLN_EPS)
        o_ref[r0:r0 + bq, :] = o * (sg_ref[...] * (1.0 - lambda_init))


def _diff_attn(ua3, cosf, sina, sinb, gq, gk, lvec, sg, lambda_init):
    b, s, _ = ua3.shape
    h_a = ua3.shape[2] // (3 * DV_A)
    bq = min(256, s)
    blk = lambda off: pl.BlockSpec((None, s, DV_A), lambda bi, hi: (bi, 0, off + hi))
    const = lambda shape: pl.BlockSpec(shape, lambda bi, hi: (0,) * len(shape))
    return pl.pallas_call(
        functools.partial(_attn_body, lambda_init=lambda_init, bq=bq),
        grid=(b, h_a),
        in_specs=[blk(0), blk(h_a), blk(2 * h_a),
                  const((s, LANES)), const((s, LANES)), const((s, LANES)),
                  const((1, LANES)), const((1, LANES)), const((4, DH_A)), const((1, LANES))],
        out_specs=pl.BlockSpec((None, s, DV_A), lambda bi, hi: (bi, 0, hi)),
        out_shape=jax.ShapeDtypeStruct((b, s, h_a * DV_A), F32),
        scratch_shapes=[pltpu.VMEM((s, LANES), BF16)] * 4,
        compiler_params=pltpu.CompilerParams(
            dimension_semantics=("arbitrary", "arbitrary"), vmem_limit_bytes=VMEM_LIMIT),
        name="diff_attn",
    )(ua3, ua3, ua3, cosf, sina, sinb, gq, gk, lvec, sg)


def _rwkv_body(u_ref, mu_ref, w0_ref, wlu_ref, a0_ref, alu_ref, glu_ref, kkw_ref, ka_ref, rk_ref,
               lg_ref, lb_ref, bd_ref, tri_ref, o_ref, prev_s, st_s, *, c_len):
    tr = u_ref.shape[0]
    d_r = o_ref.shape[1]
    n_pair = d_r // LANES
    n_ch = tr // c_len

    @pl.when(pl.program_id(1) == 0)
    def _():
        prev_s[...] = jnp.zeros_like(prev_s)
        st_s[...] = jnp.zeros_like(st_s)

    u = u_ref[...]
    row = lax.broadcasted_iota(I32, (tr, 1), 0)
    u_prev = jnp.where(row == 0, prev_s[...], pltpu.roll(u, 1, 0))
    prev_s[...] = u[tr - 1:tr, :]
    us = u + (u_prev - u) * mu_ref[...]

    r = us[:, 0:d_r]
    k = us[:, d_r:2 * d_r]
    v = us[:, 2 * d_r:3 * d_r]
    lo_in = us[:, 3 * d_r:3 * d_r + LANES]
    g_in = us[:, 3 * d_r + LANES:3 * d_r + 2 * LANES]

    bd = bd_ref[...]

    def head_sum(x):
        hi, lo = _split(x)
        return _dot(hi, bd) + _dot(lo, bd)

    z = -(w0_ref[...] + _bdot(jnp.tanh(lo_in), wlu_ref[...]))
    softplus = jnp.maximum(z, 0.0) + jnp.log(1.0 + jnp.exp(-jnp.abs(z)))
    logw = -jnp.exp(-softplus - 0.5)
    a = 1.0 / (1.0 + jnp.exp(-(a0_ref[...] + _bdot(lo_in, alu_ref[...]))))
    g = _bdot(1.0 / (1.0 + jnp.exp(-g_in)), glu_ref[...])
    kk_raw = k * kkw_ref[...]
    kk = kk_raw / jnp.maximum(jnp.sqrt(head_sum(kk_raw * kk_raw)), 1e-12)
    k_mod = k * (1.0 + (a - 1.0) * ka_ref[...])

    tri = tri_ref[...]
    w_hi, w_lo = _split(logw)
    cum = _dot(tri, w_hi) + _dot(tri, w_lo)
    last_rows = [cum[(c + 1) * c_len - 1:(c + 1) * c_len, :] for c in range(n_ch)]
    cum_last = jnp.concatenate([jnp.broadcast_to(lr, (c_len, d_r)) for lr in last_rows], axis=0)
    e_neg = jnp.exp(-cum)
    e_rem = jnp.exp(cum_last - cum)
    a_t = -kk * jnp.exp(cum - logw)
    r_t = r * jnp.exp(cum)
    b_raw = kk * a
    b_t = b_raw * e_neg
    k_t = k_mod * e_neg
    b_2 = b_raw * e_rem
    k_2 = k_mod * e_rem

    lane = lax.broadcasted_iota(I32, (1, LANES), 1)
    head0 = lane < RWKV_HEAD
    two_c = 2 * c_len
    rr = lax.broadcasted_iota(I32, (two_c, two_c), 0)
    cc = lax.broadcasted_iota(I32, (two_c, two_c), 1)
    t_i = rr & (c_len - 1)
    s_i = cc & (c_len - 1)
    strict = t_i > s_i
    incl = t_i >= s_i
    eye = rr == cc

    def stack(x, c, j):
        xp = x[c * c_len:(c + 1) * c_len, j * LANES:(j + 1) * LANES]
        return jnp.concatenate([jnp.where(head0, xp, 0.0), jnp.where(head0, 0.0, xp)], axis=0)

    units = [(c, j) for c in range(n_ch) for j in range(n_pair)]
    n_double = int(math.log2(c_len))

    a_s = [stack(a_t, c, j) for c, j in units]
    r_s = [stack(r_t, c, j) for c, j in units]
    v_b = [stack(v, c, j).astype(BF16) for c, j in units]
    gram = [_dot_nt(jnp.concatenate([a_s[i], r_s[i]], axis=0).astype(BF16),
                    jnp.concatenate([stack(b_t, c, j), stack(k_t, c, j)], axis=0).astype(BF16))
            for i, (c, j) in enumerate(units)]
    p = [jnp.where(strict, gm[:two_c, :two_c], 0.0) for gm in gram]
    m_rb = [jnp.where(incl, gm[two_c:, :two_c], 0.0).astype(BF16) for gm in gram]
    vv = [_dot(jnp.concatenate([jnp.where(strict, gm[:two_c, two_c:], 0.0),
                                jnp.where(incl, gm[two_c:, two_c:], 0.0)], axis=0).astype(BF16), vb)
          for gm, vb in zip(gram, v_b)]
    assert c_len == RWKV_HEAD and 2 * c_len == LANES
    half = LANES // 2
    lo_half = lax.broadcasted_iota(I32, (1, LANES), 1) < half
    keep = lax.broadcasted_iota(I32, (1, 2 * LANES), 1) >= half
    zeros_b = jnp.zeros((c_len, 2 * LANES), BF16)
    z = []
    for i in range(len(units)):
        for hp in range(2):
            rs = slice(hp * c_len, (hp + 1) * c_len)
            p_r, a_r, v_r = p[i][rs, :], a_s[i][rs, :], vv[i][rs, :]
            if hp == 0:
                z.append(jnp.concatenate([p_r + pltpu.roll(a_r, half, 1), v_r], axis=1))
            else:
                z.append(jnp.concatenate([pltpu.roll(p_r, half, 1) + a_r, pltpu.roll(v_r, half, 1)], axis=1))
    for it in range(n_double):
        lhs = [jnp.where(lo_half, zi[:, :LANES], 0.0).astype(BF16) for zi in z]
        rhs = [jnp.concatenate([zi.astype(BF16), zeros_b], axis=0) for zi in z]
        z = [_dot(li, ri) + jnp.where(keep, zi, 0.0) for li, ri, zi in zip(lhs, rhs, z)]
    x_b = []
    for i in range(len(units)):
        z0, z1 = z[2 * i], z[2 * i + 1]
        rows0 = jnp.concatenate([pltpu.roll(jnp.where(lo_half, 0.0, z0[:, :LANES]), half, 1),
                                 z0[:, LANES:]], axis=1)
        rows1 = jnp.concatenate([jnp.where(lo_half, 0.0, z1[:, :LANES]),
                                 pltpu.roll(z1[:, LANES:], half, 1)], axis=1)
        x_b.append(jnp.concatenate([rows0, rows1], axis=0).astype(BF16))
    mx = [_dot(mb, xb) for mb, xb in zip(m_rb, x_b)]
    bx = [_dot(stack(b_2, c, j).T.astype(BF16), x_b[i]) for i, (c, j) in enumerate(units)]
    kv = [_dot(stack(k_2, c, j).T.astype(BF16), v_b[i]) for i, (c, j) in enumerate(units)]

    y_rows = []
    st = [st_s[j] for j in range(n_pair)]
    for c in range(n_ch):
        p_c = jnp.exp(last_rows[c])
        y_pairs = []
        for j in range(n_pair):
            i = c * n_pair + j
            r_h = (r_s[i] + mx[i][:, :two_c]).astype(BF16)
            y_h = mx[i][:, two_c:] + vv[i][two_c:]
            g_m = (jnp.where(eye, p_c[:, j * LANES:(j + 1) * LANES], 0.0) + bx[i][:, :two_c]).astype(BF16)
            h_m = bx[i][:, two_c:] + kv[i]
            st_b = st[j].astype(BF16)
            y_s = _dot(r_h, st_b) + y_h
            st[j] = _dot(g_m, st_b) + h_m
            y_pairs.append(y_s[:c_len] + y_s[c_len:])
        y_rows.append(jnp.concatenate(y_pairs, axis=1))
    for j in range(n_pair):
        st_s[j] = st[j]
    y = jnp.concatenate(y_rows, axis=0)

    inv_n = 1.0 / RWKV_HEAD
    mu = head_sum(y) * inv_n
    dlt = y - mu
    var = head_sum(dlt * dlt) * inv_n
    yn = dlt * lax.rsqrt(var + LN_X_EPS) * lg_ref[...] + lb_ref[...]
    bonus = head_sum(r * k_mod * rk_ref[...]) * v
    o_ref[...] = (yn + bonus) * g


def _rwkv7(ub3, mu, w0, wlu, a0, alu, glu, kkw, ka, rk, lg, lb, bd, tri, c_len):
    b, s, cols = ub3.shape
    d_r = w0.shape[1]
    tr = tri.shape[0]
    const = lambda a: pl.BlockSpec(a.shape, lambda bi, ci: (0,) * a.ndim)
    params = (mu, w0, wlu, a0, alu, glu, kkw, ka, rk, lg, lb, bd, tri)
    return pl.pallas_call(
        functools.partial(_rwkv_body, c_len=c_len),
        grid=(b, s // tr),
        in_specs=[pl.BlockSpec((None, tr, cols), lambda bi, ci: (bi, ci, 0))]
                 + [const(a) for a in params],
        out_specs=pl.BlockSpec((None, tr, d_r), lambda bi, ci: (bi, ci, 0)),
        out_shape=jax.ShapeDtypeStruct((b, s, d_r), F32),
        scratch_shapes=[pltpu.VMEM((1, cols), F32),
                        pltpu.VMEM((d_r // LANES, LANES, LANES), F32)],
        compiler_params=pltpu.CompilerParams(
            dimension_semantics=("arbitrary", "arbitrary"), vmem_limit_bytes=VMEM_LIMIT),
        name="rwkv7",
    )(ub3, *params)


def _out_body(x_ref, oa_ref, ob_ref, wa_ref, wb_ref, g2_ref, wrh_ref, wrl_ref, br_ref, tril_ref,
              h_ref, t_ref, eid_ref, gate_ref, rank_ref, cnt_ref, run_s):
    tm = x_ref.shape[0]

    @pl.when(pl.program_id(0) == 0)
    def _():
        run_s[...] = jnp.zeros_like(run_s)

    h = x_ref[...] + _bdot(oa_ref[...], wa_ref[...]) + _bdot(ob_ref[...], wb_ref[...])
    h_ref[...] = h
    t = h * lax.rsqrt(jnp.mean(h * h, axis=-1, keepdims=True) + NORM_EPS) * g2_ref[...]
    t_ref[...] = t

    t_hi, t_lo = _split(t)
    wrh = wrh_ref[...]
    lg = _dot(t_hi, wrh) + _dot(t_lo, wrh) + _dot(t_hi, wrl_ref[...]) + br_ref[...]

    lane_i = lax.broadcasted_iota(I32, (1, LANES), 1)
    lane = lane_i.astype(F32)
    big = float(LANES)

    def first_index(mask):
        return jnp.min(jnp.where(mask, lane, big), axis=-1, keepdims=True)

    def masked_softmax(mask):
        m = jnp.max(jnp.where(mask, lg, _NEG), axis=-1, keepdims=True)
        e = jnp.where(mask, jnp.exp(lg - m), 0.0)
        return e / jnp.sum(e, axis=-1, keepdims=True)

    is_g = lane_i < N_GROUPS
    gprob = masked_softmax(is_g)
    p_group = jnp.max(gprob, axis=-1, keepdims=True)
    gsel = first_index(is_g & (gprob == p_group))
    base = N_GROUPS + gsel * EXPERTS_PER_GROUP
    in_e = (lane >= base) & (lane < base + EXPERTS_PER_GROUP)
    eprob = masked_softmax(in_e)
    p1 = jnp.max(eprob, axis=-1, keepdims=True)
    i1 = first_index(in_e & (eprob == p1))
    rest = in_e & (lane != i1)
    p2 = jnp.max(jnp.where(rest, eprob, -1.0), axis=-1, keepdims=True)
    i2 = first_index(rest & (eprob == p2))
    den = p1 + p2
    g1 = p_group * p1 / den
    g2 = p_group * p2 / den
    e1 = i1 - N_GROUPS
    e2 = i2 - N_GROUPS

    oh1 = (lane == e1).astype(F32)
    oh2 = (lane == e2).astype(F32)
    tril = tril_ref[...]
    before1 = _dot(tril, oh1.astype(BF16))
    tot1 = jnp.sum(oh1, axis=0, keepdims=True)
    before2 = _dot(tril, oh2.astype(BF16)) + tot1
    run = run_s[...]
    rk1 = jnp.sum(oh1 * (before1 + run), axis=-1, keepdims=True)
    rk2 = jnp.sum(oh2 * (before2 + run), axis=-1, keepdims=True)
    run = run + tot1 + jnp.sum(oh2, axis=0, keepdims=True)
    run_s[...] = run
    cnt_ref[...] = run

    sel0 = lane_i == 0
    sel1 = lane_i == 1
    eid_ref[...] = jnp.where(sel0, e1, jnp.where(sel1, e2, 0.0)).astype(I32)
    rank_ref[...] = jnp.where(sel0, rk1, jnp.where(sel1, rk2, 0.0)).astype(I32)
    gate_ref[...] = jnp.where(sel0, g1, jnp.where(sel1, g2, 0.0))


def _out_proj_route(x2, oa, ob, wa, wb, g2, wrh, wrl, br, tril):
    n, d = x2.shape
    tm = tril.shape[0]
    da, db = oa.shape[1], ob.shape[1]
    row = lambda w: pl.BlockSpec((tm, w), lambda i: (i, 0))
    const = lambda a: pl.BlockSpec(a.shape, lambda i: (0,) * a.ndim)
    return pl.pallas_call(
        _out_body,
        grid=(n // tm,),
        in_specs=[row(d), row(da), row(db), const(wa), const(wb), const(g2), const(wrh), const(wrl),
                  const(br), const(tril)],
        out_specs=[row(d), row(d), row(LANES), row(LANES), row(LANES),
                   pl.BlockSpec((1, LANES), lambda i: (0, 0))],
        out_shape=[jax.ShapeDtypeStruct((n, d), F32), jax.ShapeDtypeStruct((n, d), F32),
                   jax.ShapeDtypeStruct((n, LANES), I32), jax.ShapeDtypeStruct((n, LANES), F32),
                   jax.ShapeDtypeStruct((n, LANES), I32), jax.ShapeDtypeStruct((1, LANES), F32)],
        scratch_shapes=[pltpu.VMEM((1, LANES), F32)],
        compiler_params=pltpu.CompilerParams(
            dimension_semantics=("arbitrary",), vmem_limit_bytes=VMEM_LIMIT),
        name="out_proj_route",
    )(x2, oa, ob, wa, wb, g2, wrh, wrl, br, tril)


def _dispatch_body(d0_ref, d1_ref, t_ref, xs_ref, sem):
    tm = d0_ref.shape[0]

    def copy(i, d_ref):
        return pltpu.make_async_copy(t_ref.at[pl.ds(i, 1)], xs_ref.at[pl.ds(d_ref[i], 1)], sem)

    def start(i, carry):
        copy(i, d0_ref).start(priority=0)
        copy(i, d1_ref).start(priority=1)
        return carry

    lax.fori_loop(0, tm, start, 0, unroll=8)
    for _ in range(2):
        pltpu.make_async_copy(t_ref, xs_ref.at[pl.ds(0, tm)], sem).wait()


def _dispatch(d0, d1, t):
    n, d = t.shape
    tm = min(512, n)
    smem = pl.BlockSpec((tm,), lambda i: (i,), memory_space=pltpu.SMEM)
    return pl.pallas_call(
        _dispatch_body,
        grid=(n // tm,),
        in_specs=[smem, smem, pl.BlockSpec((tm, d), lambda i: (i, 0))],
        out_specs=pl.BlockSpec(memory_space=pl.ANY),
        out_shape=jax.ShapeDtypeStruct((2 * n, d), t.dtype),
        scratch_shapes=[pltpu.SemaphoreType.DMA],
        compiler_params=pltpu.CompilerParams(dimension_semantics=("arbitrary",)),
        name="moe_dispatch",
    )(d0, d1, t)


def _experts_body(ub_ref, ue_ref, gs_ref, nu_ref, x_ref, wg_ref, wu_ref, wd_ref, y_ref,
                  wg_s, wu_s, wd_s):
    w = pl.program_id(0)
    bm = x_ref.shape[0]

    @pl.when(w < nu_ref[0])
    def _():
        e = ue_ref[w]
        blk = ub_ref[w]
        prev = jnp.maximum(w - 1, 0)

        @pl.when(jnp.logical_or(w == 0, ue_ref[prev] != e))
        def _():
            wg_s[...] = wg_ref[...].astype(BF16)
            wu_s[...] = wu_ref[...].astype(BF16)
            wd_s[...] = wd_ref[...].astype(BF16)

        rows = blk * bm + lax.broadcasted_iota(I32, (bm, 1), 0)
        valid = (rows >= gs_ref[e]) & (rows < gs_ref[e + 1])
        xb = x_ref[...].astype(BF16)
        gt = _dot(xb, wg_s[...])
        up = _dot(xb, wu_s[...])
        act = (gt / (1.0 + jnp.exp(-gt))) * up
        y = jnp.where(valid, _dot(act.astype(BF16), wd_s[...]), 0.0)
        first = jnp.logical_or(w == 0, ub_ref[prev] != blk)

        @pl.when(first)
        def _():
            y_ref[...] = y

        @pl.when(jnp.logical_not(first))
        def _():
            y_ref[...] += y


def _experts(unit_blk, unit_exp, gstart, n_units, xs, wg, wu, wd, bm):
    m, d = xs.shape
    de = wg.shape[2]
    max_units = unit_blk.shape[0]
    grid_spec = pltpu.PrefetchScalarGridSpec(
        num_scalar_prefetch=4,
        grid=(max_units,),
        in_specs=[
            pl.BlockSpec((bm, d), lambda w, ub, ue, gs, nu: (ub[w], 0)),
            pl.BlockSpec((None, d, de), lambda w, ub, ue, gs, nu: (ue[w], 0, 0)),
            pl.BlockSpec((None, d, de), lambda w, ub, ue, gs, nu: (ue[w], 0, 0)),
            pl.BlockSpec((None, de, d), lambda w, ub, ue, gs, nu: (ue[w], 0, 0)),
        ],
        out_specs=pl.BlockSpec((bm, d), lambda w, ub, ue, gs, nu: (ub[w], 0)),
        scratch_shapes=[pltpu.VMEM((d, de), BF16), pltpu.VMEM((d, de), BF16),
                        pltpu.VMEM((de, d), BF16)],
    )
    return pl.pallas_call(
        _experts_body,
        grid_spec=grid_spec,
        out_shape=jax.ShapeDtypeStruct((m, d), F32),
        compiler_params=pltpu.CompilerParams(
            dimension_semantics=("arbitrary",), vmem_limit_bytes=VMEM_LIMIT),
        name="moe_experts",
    )(unit_blk, unit_exp, gstart, n_units, xs, wg, wu, wd)


def _combine_body(d0_ref, d1_ref, y_ref, h_ref, gate_ref, o_ref, b0, b1, sem):
    tm = d0_ref.shape[0]

    def copy(i, d_ref, buf):
        return pltpu.make_async_copy(y_ref.at[pl.ds(d_ref[i], 1)], buf.at[pl.ds(i, 1)], sem)

    def start(i, carry):
        copy(i, d0_ref, b0).start(priority=0)
        copy(i, d1_ref, b1).start(priority=1)
        return carry

    lax.fori_loop(0, tm, start, 0, unroll=8)
    for buf in (b0, b1):
        pltpu.make_async_copy(y_ref.at[pl.ds(0, tm)], buf, sem).wait()
    gate = gate_ref[...]
    o_ref[...] = h_ref[...] + b0[...] * gate[:, 0:1] + b1[...] * gate[:, 1:2]


def _combine(d0, d1, y, h, gate):
    n, d = h.shape
    tm = min(512, n)
    smem = pl.BlockSpec((tm,), lambda i: (i,), memory_space=pltpu.SMEM)
    return pl.pallas_call(
        _combine_body,
        grid=(n // tm,),
        in_specs=[smem, smem, pl.BlockSpec(memory_space=pl.ANY),
                  pl.BlockSpec((tm, d), lambda i: (i, 0)),
                  pl.BlockSpec((tm, LANES), lambda i: (i, 0))],
        out_specs=pl.BlockSpec((tm, d), lambda i: (i, 0)),
        out_shape=jax.ShapeDtypeStruct((n, d), F32),
        scratch_shapes=[pltpu.VMEM((tm, d), F32), pltpu.VMEM((tm, d), F32), pltpu.SemaphoreType.DMA],
        compiler_params=pltpu.CompilerParams(
            dimension_semantics=("arbitrary",), vmem_limit_bytes=VMEM_LIMIT),
        name="moe_combine",
    )(d0, d1, y, h, gate)


def _rope_tables(seq):
    half = ROT_DIM // 2
    inv = ROPE_THETA ** (-jnp.arange(0, ROT_DIM, 2, dtype=F32) / ROT_DIM)
    ang = jnp.arange(seq, dtype=F32)[:, None] * inv[None, :]
    cos, sin = jnp.cos(ang), jnp.sin(ang)
    ones = jnp.ones((seq, DH_A - ROT_DIM), F32)
    zeros = jnp.zeros((seq, DH_A - half), F32)
    zeros_h = jnp.zeros((seq, half), F32)
    zeros_r = jnp.zeros((seq, DH_A - ROT_DIM), F32)
    cos_c = jnp.concatenate([cos, cos, ones], axis=1)
    sa_c = jnp.concatenate([-sin, zeros], axis=1)
    sb_c = jnp.concatenate([zeros_h, sin, zeros_r], axis=1)
    tile = lambda t: jnp.concatenate([t, t], axis=1)
    return tile(cos_c), tile(sa_c), tile(sb_c)


def _layer(x, l, lambda_init, ln1_g, w_in, q_norm_g, k_norm_g, lambda_q1, lambda_k1, lambda_q2,
           lambda_k2, subln_g, shift_mu, w0, w_lora_up, a0, a_lora_up, g_lora_up, k_k, k_a, r_k,
           ln_x_g, ln_x_b, w_out, ln2_g, w_router_group, b_router_group, w_router_expert,
           b_router_expert, w_exp_gate, w_exp_up, w_exp_down):
    b, s, d = x.shape
    n = b * s
    d_attn3 = 3 * (d // 2)
    d_attn = d // 2
    d_r = w0.shape[1]
    x2 = x.reshape(n, d)
    row = lambda a: a[l].reshape(1, -1)

    w_in_b = w_in[l].astype(BF16)
    ua, ub = _in_proj(x2, row(ln1_g), w_in_b[:, :d_attn3], w_in_b[:, d_attn3:])

    cosf, sina, sinb = _rope_tables(s)
    tile2 = lambda a: jnp.concatenate([a[l], a[l]]).reshape(1, -1)
    lvec = jnp.stack([lambda_q1[l], lambda_k1[l], lambda_q2[l], lambda_k2[l]])
    o_a = _diff_attn(ua.reshape(b, s, d_attn3), cosf, sina, sinb, tile2(q_norm_g), tile2(k_norm_g),
                     lvec, row(subln_g), lambda_init)

    dl = w_lora_up.shape[1]
    zeros_l = jnp.zeros((dl, d_r), F32)
    wlu = jnp.concatenate([w_lora_up[l], zeros_l], axis=0).astype(BF16)
    alu = jnp.concatenate([zeros_l, a_lora_up[l]], axis=0).astype(BF16)
    hid = jnp.arange(d_r) // RWKV_HEAD
    bd = (hid[:, None] == hid[None, :]).astype(BF16)
    tr = min(RWKV_ROWS, s)
    ti = jnp.arange(tr)
    tri = ((ti[:, None] >= ti[None, :]) & (ti[:, None] // CHUNK == ti[None, :] // CHUNK)).astype(BF16)
    o_b = _rwkv7(ub.reshape(b, s, -1), row(shift_mu), row(w0), wlu, row(a0), alu,
                 g_lora_up[l].astype(BF16), row(k_k), row(k_a), row(r_k), row(ln_x_g), row(ln_x_b),
                 bd, tri, CHUNK)

    w_out_b = w_out[l].astype(BF16)
    wr = jnp.concatenate([w_router_group[l], w_router_expert[l],
                          jnp.zeros((d, LANES - N_GROUPS - N_EXPERTS), F32)], axis=1)
    wr_hi = wr.astype(BF16)
    wr_lo = (wr - wr_hi.astype(F32)).astype(BF16)
    br = jnp.concatenate([b_router_group[l], b_router_expert[l],
                          jnp.zeros((LANES - N_GROUPS - N_EXPERTS,), F32)]).reshape(1, LANES)
    tm_r = min(512, n)
    tril = (jnp.arange(tm_r)[:, None] > jnp.arange(tm_r)[None, :]).astype(BF16)
    h, t, eid, gate, rank, cnt = _out_proj_route(
        x2, o_a.reshape(n, d_attn), o_b.reshape(n, d_r), w_out_b[:d_attn], w_out_b[d_attn:],
        row(ln2_g), wr_hi, wr_lo, br, tril)

    counts = cnt[0, :N_EXPERTS].astype(I32)
    gend = jnp.cumsum(counts)
    gstart = jnp.concatenate([jnp.zeros((1,), I32), gend]).astype(I32)
    onehot = (eid[:, :2, None] == jnp.arange(N_EXPERTS)[None, None, :])
    dest = jnp.sum(jnp.where(onehot, gstart[:N_EXPERTS], 0), axis=-1) + rank[:, :2]
    d0 = dest[:, 0].astype(I32)
    d1 = dest[:, 1].astype(I32)

    m = 2 * n
    bm = min(512, m)
    n_blk = m // bm
    max_units = n_blk + N_EXPERTS - 1
    first_blk = gstart[:N_EXPERTS] // bm
    last_blk = jnp.where(counts > 0, (gend - 1) // bm, first_blk)
    units_e = jnp.where(counts > 0, last_blk - first_blk + 1, 0)
    unit_end = jnp.cumsum(units_e)
    unit_start = unit_end - units_e
    n_units = unit_end[-1]
    w_ids = jnp.arange(max_units)
    u_exp = jnp.minimum(jnp.sum((unit_end[None, :] <= w_ids[:, None]).astype(I32), axis=1),
                        N_EXPERTS - 1)
    u_blk = first_blk[u_exp] + (w_ids - unit_start[u_exp])
    last = jnp.maximum(n_units - 1, 0)
    pad = w_ids >= n_units
    u_exp = jnp.where(pad, u_exp[last], u_exp).astype(I32)
    u_blk = jnp.where(pad, u_blk[last], u_blk).astype(I32)

    xs = _dispatch(d0, d1, t)
    y = _experts(u_blk, u_exp, gstart, n_units.reshape(1).astype(I32), xs,
                 w_exp_gate[l], w_exp_up[l], w_exp_down[l], bm)
    out = _combine(d0, d1, y, h, gate)
    return out.reshape(b, s, d)


def kernel(x, ln1_g, w_in, q_norm_g, k_norm_g, lambda_q1, lambda_k1, lambda_q2, lambda_k2, subln_g, shift_mu, w0, w_lora_up, a0, a_lora_up, g_lora_up, k_k, k_a, r_k, ln_x_g, ln_x_b, w_out, ln2_g, w_router_group, b_router_group, w_router_expert, b_router_expert, w_exp_gate, w_exp_up, w_exp_down):
    h = x
    for l in range(ln1_g.shape[0]):
        lambda_init = 0.8 - 0.6 * math.exp(-0.3 * l)
        h = _layer(h, l, lambda_init, ln1_g, w_in, q_norm_g, k_norm_g, lambda_q1, lambda_k1,
                   lambda_q2, lambda_k2, subln_g, shift_mu, w0, w_lora_up, a0, a_lora_up, g_lora_up,
                   k_k, k_a, r_k, ln_x_g, ln_x_b, w_out, ln2_g, w_router_group, b_router_group,
                   w_router_expert, b_router_expert, w_exp_gate, w_exp_up, w_exp_down)
    return h
```

```python
import functools
import math

import jax
import jax.numpy as jnp
from jax import lax
from jax.experimental import pallas as pl
from jax.experimental.pallas import tpu as pltpu

F32 = jnp.float32
BF16 = jnp.bfloat16
I32 = jnp.int32

CHUNK = 64
DH_A = 64
DV_A = 128
ROT_DIM = 16
ROPE_THETA = 500000.0
QK_NORM_EPS = 1e-6
SUBLN_EPS = 1e-5
RWKV_HEAD = 64
LN_X_EPS = 64e-5
RWKV_ROWS = 256
N_GROUPS = 4
EXPERTS_PER_GROUP = 8
N_EXPERTS = 32
NORM_EPS = 1e-6

LANES = 128
SUBLANES = 8
VMEM_LIMIT = 56 * 1024 * 1024

_NEG = -1e30


def _dot(a, b):
    return jnp.dot(a, b, preferred_element_type=F32)


def _dot_nt(a, b):
    return lax.dot_general(a, b, (((1,), (1,)), ((), ())), preferred_element_type=F32)


def _bdot(a, b):
    return _dot(a.astype(BF16), b.astype(BF16))


def _split(x):
    hi = x.astype(BF16)
    lo = (x - hi.astype(F32)).astype(BF16)
    return hi, lo


def _tile_rows_load(ref, rows):
    return jnp.concatenate([ref[pl.ds(j, rows, stride=SUBLANES), :] for j in range(SUBLANES)], axis=1)


def _tile_rows_store(ref, x):
    rows = x.shape[0]
    for j in range(SUBLANES):
        ref[pl.ds(j, rows, stride=SUBLANES), :] = x[:, j * LANES:(j + 1) * LANES]


def _in_proj_body(x_ref, g_ref, wa_ref, wb_ref, ua_ref, ub_ref):
    x = x_ref[...]
    ms = jnp.mean(x * x, axis=-1, keepdims=True)
    xn = (x * lax.rsqrt(ms + NORM_EPS) * g_ref[...]).astype(BF16)
    ua_ref[...] = _dot(xn, wa_ref[...])
    ub_ref[...] = _dot(xn, wb_ref[...])


def _in_proj(x2, g, wa, wb):
    n, d = x2.shape
    tm = min(512, n)
    na, nb = wa.shape[1], wb.shape[1]
    return pl.pallas_call(
        _in_proj_body,
        grid=(n // tm,),
        in_specs=[
            pl.BlockSpec((tm, d), lambda i: (i, 0)),
            pl.BlockSpec((1, d), lambda i: (0, 0)),
            pl.BlockSpec((d, na), lambda i: (0, 0)),
            pl.BlockSpec((d, nb), lambda i: (0, 0)),
        ],
        out_specs=[
            pl.BlockSpec((tm, na), lambda i: (i, 0)),
            pl.BlockSpec((tm, nb), lambda i: (i, 0)),
        ],
        out_shape=[jax.ShapeDtypeStruct((n, na), F32), jax.ShapeDtypeStruct((n, nb), F32)],
        compiler_params=pltpu.CompilerParams(
            dimension_semantics=("arbitrary",), vmem_limit_bytes=VMEM_LIMIT),
        name="in_proj",
    )(x2, g, wa, wb)


def _attn_body(q_ref, k_ref, v_ref, cos_ref, sa_ref, sb_ref, gq_ref, gk_ref, l_ref, sg_ref,
               o_ref, q0_s, q1_s, k_s, v_s, *, lambda_init, bq):
    s_len = q_ref.shape[0]
    lane = lax.broadcasted_iota(I32, (1, LANES), 1)
    lo = lane < DH_A

    def prep(t, g):
        t2 = t * t
        s_lo = jnp.sum(jnp.where(lo, t2, 0.0), axis=-1, keepdims=True)
        s_hi = jnp.sum(jnp.where(lo, 0.0, t2), axis=-1, keepdims=True)
        ms = jnp.where(lo, s_lo, s_hi) * (1.0 / DH_A)
        tn = t * lax.rsqrt(ms + QK_NORM_EPS) * g
        return (tn * cos_ref[...] + pltpu.roll(tn, LANES - ROT_DIM // 2, 1) * sa_ref[...]
                + pltpu.roll(tn, ROT_DIM // 2, 1) * sb_ref[...])

    q = prep(q_ref[...], gq_ref[...]) * (DH_A ** -0.5 * math.log2(math.e))
    q0_s[...] = jnp.where(lo, q, 0.0).astype(BF16)
    q1_s[...] = jnp.where(lo, 0.0, q).astype(BF16)
    k_s[...] = prep(k_ref[...], gk_ref[...]).astype(BF16)
    v_s[...] = v_ref[...].astype(BF16)

    l = l_ref[...]
    lam = (jnp.exp(jnp.sum(l[0:1] * l[1:2], axis=-1, keepdims=True))
           - jnp.exp(jnp.sum(l[2:3] * l[3:4], axis=-1, keepdims=True)) + lambda_init)

    chunk_shift = int(math.log2(CHUNK))
    rr = lax.broadcasted_iota(I32, (bq, bq), 0) >> chunk_shift
    cc = lax.broadcasted_iota(I32, (bq, bq), 1) >> chunk_shift
    diag_ok = cc <= rr

    diag_ok2 = jnp.concatenate([diag_ok, diag_ok], axis=0)

    for i in range(s_len // bq):
        r0 = i * bq
        qq = jnp.concatenate([q0_s[r0:r0 + bq, :], q1_s[r0:r0 + bq, :]], axis=0)
        m = l_run = acc = None
        for j in range(i + 1):
            c0 = j * bq
            s = _dot_nt(qq, k_s[c0:c0 + bq, :])
            if j == i:
                s = jnp.where(diag_ok2, s, _NEG)
            m_tile = jnp.max(s, axis=-1, keepdims=True)
            if j == 0:
                m = m_tile
                e = jnp.exp2(s - m)
                l_run = jnp.sum(e, axis=-1, keepdims=True)
                acc = _dot(e.astype(BF16), v_s[c0:c0 + bq, :])
            else:
                m_new = jnp.maximum(m, m_tile)
                alpha = jnp.exp2(m - m_new)
                e = jnp.exp2(s - m_new)
                l_run = l_run * alpha + jnp.sum(e, axis=-1, keepdims=True)
                acc = acc * alpha + _dot(e.astype(BF16), v_s[c0:c0 + bq, :])
                m = m_new
        w0 = 1.0 / l_run[:bq]
        w1 = lam / l_run[bq:]
        o = acc[:bq] * w0 - acc[bq:] * w1
        o = o * lax.rsqrt(jnp.mean(o * o, axis=-1, keepdims=True) + SUBLN_EPS)
        o_ref[r0:r0 + bq, :] = o * (sg_ref[...] * (1.0 - lambda_init))


def _diff_attn(ua3, cosf, sina, sinb, gq, gk, lvec, sg, lambda_init):
    b, s, _ = ua3.shape
    h_a = ua3.shape[2] // (3 * DV_A)
    bq = min(256, s)
    blk = lambda off: pl.BlockSpec((None, s, DV_A), lambda bi, hi: (bi, 0, off + hi))
    const = lambda shape: pl.BlockSpec(shape, lambda bi, hi: (0,) * len(shape))
    return pl.pallas_call(
        functools.partial(_attn_body, lambda_init=lambda_init, bq=bq),
        grid=(b, h_a),
        in_specs=[blk(0), blk(h_a), blk(2 * h_a),
                  const((s, LANES)), const((s, LANES)), const((s, LANES)),
                  const((1, LANES)), const((1, LANES)), const((4, DH_A)), const((1, LANES))],
        out_specs=pl.BlockSpec((None, s, DV_A), lambda bi, hi: (bi, 0, hi)),
        out_shape=jax.ShapeDtypeStruct((b, s, h_a * DV_A), F32),
        scratch_shapes=[pltpu.VMEM((s, LANES), BF16)] * 4,
        compiler_params=pltpu.CompilerParams(
            dimension_semantics=("arbitrary", "arbitrary"), vmem_limit_bytes=VMEM_LIMIT),
        name="diff_attn",
    )(ua3, ua3, ua3, cosf, sina, sinb, gq, gk, lvec, sg)


def _rwkv_body(u_ref, mu_ref, w0_ref, wlu_ref, a0_ref, alu_ref, glu_ref, kkw_ref, ka_ref, rk_ref,
               lg_ref, lb_ref, bd_ref, tri_ref, o_ref, prev_s, st_s, *, c_len):
    tr = u_ref.shape[0]
    d_r = o_ref.shape[1]
    n_pair = d_r // LANES
    n_ch = tr // c_len

    @pl.when(pl.program_id(1) == 0)
    def _():
        prev_s[...] = jnp.zeros_like(prev_s)
        st_s[...] = jnp.zeros_like(st_s)

    u = u_ref[...]
    row = lax.broadcasted_iota(I32, (tr, 1), 0)
    u_prev = jnp.where(row == 0, prev_s[...], pltpu.roll(u, 1, 0))
    prev_s[...] = u[tr - 1:tr, :]
    us = u + (u_prev - u) * mu_ref[...]

    r = us[:, 0:d_r]
    k = us[:, d_r:2 * d_r]
    v = us[:, 2 * d_r:3 * d_r]
    lo_in = us[:, 3 * d_r:3 * d_r + LANES]
    g_in = us[:, 3 * d_r + LANES:3 * d_r + 2 * LANES]

    bd = bd_ref[...]

    def head_sum(x):
        hi, lo = _split(x)
        return _dot(hi, bd) + _dot(lo, bd)

    z = -(w0_ref[...] + _bdot(jnp.tanh(lo_in), wlu_ref[...]))
    softplus = jnp.maximum(z, 0.0) + jnp.log(1.0 + jnp.exp(-jnp.abs(z)))
    logw = -jnp.exp(-softplus - 0.5)
    a = 1.0 / (1.0 + jnp.exp(-(a0_ref[...] + _bdot(lo_in, alu_ref[...]))))
    g = _bdot(1.0 / (1.0 + jnp.exp(-g_in)), glu_ref[...])
    kk_raw = k * kkw_ref[...]
    kk = kk_raw / jnp.maximum(jnp.sqrt(head_sum(kk_raw * kk_raw)), 1e-12)
    k_mod = k * (1.0 + (a - 1.0) * ka_ref[...])

    tri = tri_ref[...]
    w_hi, w_lo = _split(logw)
    cum = _dot(tri, w_hi) + _dot(tri, w_lo)
    last_rows = [cum[(c + 1) * c_len - 1:(c + 1) * c_len, :] for c in range(n_ch)]
    cum_last = jnp.concatenate([jnp.broadcast_to(lr, (c_len, d_r)) for lr in last_rows], axis=0)
    e_neg = jnp.exp(-cum)
    e_rem = jnp.exp(cum_last - cum)
    a_t = -kk * jnp.exp(cum - logw)
    r_t = r * jnp.exp(cum)
    b_raw = kk * a
    b_t = b_raw * e_neg
    k_t = k_mod * e_neg
    b_2 = b_raw * e_rem
    k_2 = k_mod * e_rem

    lane = lax.broadcasted_iota(I32, (1, LANES), 1)
    head0 = lane < RWKV_HEAD
    two_c = 2 * c_len
    rr = lax.broadcasted_iota(I32, (two_c, two_c), 0)
    cc = lax.broadcasted_iota(I32, (two_c, two_c), 1)
    t_i = rr & (c_len - 1)
    s_i = cc & (c_len - 1)
    strict = t_i > s_i
    incl = t_i >= s_i
    eye = rr == cc

    def stack(x, c, j):
        xp = x[c * c_len:(c + 1) * c_len, j * LANES:(j + 1) * LANES]
        return jnp.concatenate([jnp.where(head0, xp, 0.0), jnp.where(head0, 0.0, xp)], axis=0)

    units = [(c, j) for c in range(n_ch) for j in range(n_pair)]
    n_double = int(math.log2(c_len))

    a_s = [stack(a_t, c, j) for c, j in units]
    r_s = [stack(r_t, c, j) for c, j in units]
    v_b = [stack(v, c, j).astype(BF16) for c, j in units]
    gram = [_dot_nt(jnp.concatenate([a_s[i], r_s[i]], axis=0).astype(BF16),
                    jnp.concatenate([stack(b_t, c, j), stack(k_t, c, j)], axis=0).astype(BF16))
            for i, (c, j) in enumerate(units)]
    p = [jnp.where(strict, gm[:two_c, :two_c], 0.0) for gm in gram]
    m_rb = [jnp.where(incl, gm[two_c:, :two_c], 0.0).astype(BF16) for gm in gram]
    vv = [_dot(jnp.concatenate([jnp.where(strict, gm[:two_c, two_c:], 0.0),
                                jnp.where(incl, gm[two_c:, two_c:], 0.0)], axis=0).astype(BF16), vb)
          for gm, vb in zip(gram, v_b)]
    assert c_len == RWKV_HEAD and 2 * c_len == LANES
    half = LANES // 2
    lo_half = lax.broadcasted_iota(I32, (1, LANES), 1) < half
    keep = lax.broadcasted_iota(I32, (1, 2 * LANES), 1) >= half
    zeros_b = jnp.zeros((c_len, 2 * LANES), BF16)
    z = []
    for i in range(len(units)):
        for hp in range(2):
            rs = slice(hp * c_len, (hp + 1) * c_len)
            p_r, a_r, v_r = p[i][rs, :], a_s[i][rs, :], vv[i][rs, :]
            if hp == 0:
                z.append(jnp.concatenate([p_r + pltpu.roll(a_r, half, 1), v_r], axis=1))
            else:
                z.append(jnp.concatenate([pltpu.roll(p_r, half, 1) + a_r, pltpu.roll(v_r, half, 1)], axis=1))
    for it in range(n_double):
        lhs = [jnp.where(lo_half, zi[:, :LANES], 0.0).astype(BF16) for zi in z]
        rhs = [jnp.concatenate([zi.astype(BF16), zeros_b], axis=0) for zi in z]
        z = [_dot(li, ri) + jnp.where(keep, zi, 0.0) for li, ri, zi in zip(lhs, rhs, z)]
    x_b = []
    for i in range(len(units)):
        z0, z1 = z[2 * i], z[2 * i + 1]
        rows0 = jnp.concatenate([pltpu.roll(jnp.where(lo_half, 0.0, z0[:, :LANES]), half, 1),
                                 z0[:, LANES:]], axis=1)
        rows1 = jnp.concatenate([jnp.where(lo_half, 0.0, z1[:, :LANES]),
                                 pltpu.roll(z1[:, LANES:], half, 1)], axis=1)
        x_b.append(jnp.concatenate([rows0, rows1], axis=0).astype(BF16))
    mx = [_dot(mb, xb) for mb, xb in zip(m_rb, x_b)]
    bx = [_dot(stack(b_2, c, j).T.astype(BF16), x_b[i]) for i, (c, j) in enumerate(units)]
    kv = [_dot(stack(k_2, c, j).T.astype(BF16), v_b[i]) for i, (c, j) in enumerate(units)]

    y_rows = []
    st = [st_s[j] for j in range(n_pair)]
    for c in range(n_ch):
        p_c = jnp.exp(last_rows[c])
        y_pairs = []
        for j in range(n_pair):
            i = c * n_pair + j
            r_h = (r_s[i] + mx[i][:, :two_c]).astype(BF16)
            y_h = mx[i][:, two_c:] + vv[i][two_c:]
            g_m = (jnp.where(eye, p_c[:, j * LANES:(j + 1) * LANES], 0.0) + bx[i][:, :two_c]).astype(BF16)
            h_m = bx[i][:, two_c:] + kv[i]
            st_b = st[j].astype(BF16)
            y_s = _dot(r_h, st_b) + y_h
            st[j] = _dot(g_m, st_b) + h_m
            y_pairs.append(y_s[:c_len] + y_s[c_len:])
        y_rows.append(jnp.concatenate(y_pairs, axis=1))
    for j in range(n_pair):
        st_s[j] = st[j]
    y = jnp.concatenate(y_rows, axis=0)

    inv_n = 1.0 / RWKV_HEAD
    mu = head_sum(y) * inv_n
    dlt = y - mu
    var = head_sum(dlt * dlt) * inv_n
    yn = dlt * lax.rsqrt(var + LN_X_EPS) * lg_ref[...] + lb_ref[...]
    bonus = head_sum(r * k_mod * rk_ref[...]) * v
    o_ref[...] = (yn + bonus) * g


def _rwkv7(ub3, mu, w0, wlu, a0, alu, glu, kkw, ka, rk, lg, lb, bd, tri, c_len):
    b, s, cols = ub3.shape
    d_r = w0.shape[1]
    tr = tri.shape[0]
    const = lambda a: pl.BlockSpec(a.shape, lambda bi, ci: (0,) * a.ndim)
    params = (mu, w0, wlu, a0, alu, glu, kkw, ka, rk, lg, lb, bd, tri)
    return pl.pallas_call(
        functools.partial(_rwkv_body, c_len=c_len),
        grid=(b, s // tr),
        in_specs=[pl.BlockSpec((None, tr, cols), lambda bi, ci: (bi, ci, 0))]
                 + [const(a) for a in params],
        out_specs=pl.BlockSpec((None, tr, d_r), lambda bi, ci: (bi, ci, 0)),
        out_shape=jax.ShapeDtypeStruct((b, s, d_r), F32),
        scratch_shapes=[pltpu.VMEM((1, cols), F32),
                        pltpu.VMEM((d_r // LANES, LANES, LANES), F32)],
        compiler_params=pltpu.CompilerParams(
            dimension_semantics=("arbitrary", "arbitrary"), vmem_limit_bytes=VMEM_LIMIT),
        name="rwkv7",
    )(ub3, *params)


def _out_body(x_ref, oa_ref, ob_ref, wa_ref, wb_ref, g2_ref, wrh_ref, wrl_ref, br_ref, tril_ref,
              h_ref, t_ref, eid_ref, gate_ref, rank_ref, cnt_ref, run_s):
    tm = x_ref.shape[0]

    @pl.when(pl.program_id(0) == 0)
    def _():
        run_s[...] = jnp.zeros_like(run_s)

    h = x_ref[...] + _bdot(oa_ref[...], wa_ref[...]) + _bdot(ob_ref[...], wb_ref[...])
    h_ref[...] = h
    t = h * lax.rsqrt(jnp.mean(h * h, axis=-1, keepdims=True) + NORM_EPS) * g2_ref[...]
    _tile_rows_store(t_ref, t)

    t_hi, t_lo = _split(t)
    wrh = wrh_ref[...]
    lg = _dot(t_hi, wrh) + _dot(t_lo, wrh) + _dot(t_hi, wrl_ref[...]) + br_ref[...]

    lane_i = lax.broadcasted_iota(I32, (1, LANES), 1)
    lane = lane_i.astype(F32)
    big = float(LANES)

    def first_index(mask):
        return jnp.min(jnp.where(mask, lane, big), axis=-1, keepdims=True)

    def masked_softmax(mask):
        m = jnp.max(jnp.where(mask, lg, _NEG), axis=-1, keepdims=True)
        e = jnp.where(mask, jnp.exp(lg - m), 0.0)
        return e / jnp.sum(e, axis=-1, keepdims=True)

    is_g = lane_i < N_GROUPS
    gprob = masked_softmax(is_g)
    p_group = jnp.max(gprob, axis=-1, keepdims=True)
    gsel = first_index(is_g & (gprob == p_group))
    base = N_GROUPS + gsel * EXPERTS_PER_GROUP
    in_e = (lane >= base) & (lane < base + EXPERTS_PER_GROUP)
    eprob = masked_softmax(in_e)
    p1 = jnp.max(eprob, axis=-1, keepdims=True)
    i1 = first_index(in_e & (eprob == p1))
    rest = in_e & (lane != i1)
    p2 = jnp.max(jnp.where(rest, eprob, -1.0), axis=-1, keepdims=True)
    i2 = first_index(rest & (eprob == p2))
    den = p1 + p2
    g1 = p_group * p1 / den
    g2 = p_group * p2 / den
    e1 = i1 - N_GROUPS
    e2 = i2 - N_GROUPS

    oh1 = (lane == e1).astype(F32)
    oh2 = (lane == e2).astype(F32)
    tril = tril_ref[...]
    before1 = _dot(tril, oh1.astype(BF16))
    tot1 = jnp.sum(oh1, axis=0, keepdims=True)
    before2 = _dot(tril, oh2.astype(BF16)) + tot1
    run = run_s[...]
    rk1 = jnp.sum(oh1 * (before1 + run), axis=-1, keepdims=True)
    rk2 = jnp.sum(oh2 * (before2 + run), axis=-1, keepdims=True)
    run = run + tot1 + jnp.sum(oh2, axis=0, keepdims=True)
    run_s[...] = run
    cnt_ref[...] = run

    sel0 = lane_i == 0
    sel1 = lane_i == 1
    eid_ref[...] = jnp.where(sel0, e1, jnp.where(sel1, e2, 0.0)).astype(I32)
    rank_ref[...] = jnp.where(sel0, rk1, jnp.where(sel1, rk2, 0.0)).astype(I32)
    gate_ref[...] = jnp.where(sel0, g1, jnp.where(sel1, g2, 0.0))


def _out_proj_route(x2, oa, ob, wa, wb, g2, wrh, wrl, br, tril):
    n, d = x2.shape
    tm = tril.shape[0]
    da, db = oa.shape[1], ob.shape[1]
    row = lambda w: pl.BlockSpec((tm, w), lambda i: (i, 0))
    const = lambda a: pl.BlockSpec(a.shape, lambda i: (0,) * a.ndim)
    return pl.pallas_call(
        _out_body,
        grid=(n // tm,),
        in_specs=[row(d), row(da), row(db), const(wa), const(wb), const(g2), const(wrh), const(wrl),
                  const(br), const(tril)],
        out_specs=[row(d), pl.BlockSpec((tm * SUBLANES, LANES), lambda i: (i, 0)),
                   row(LANES), row(LANES), row(LANES), pl.BlockSpec((1, LANES), lambda i: (0, 0))],
        out_shape=[jax.ShapeDtypeStruct((n, d), F32), jax.ShapeDtypeStruct((n * SUBLANES, LANES), F32),
                   jax.ShapeDtypeStruct((n, LANES), I32), jax.ShapeDtypeStruct((n, LANES), F32),
                   jax.ShapeDtypeStruct((n, LANES), I32), jax.ShapeDtypeStruct((1, LANES), F32)],
        scratch_shapes=[pltpu.VMEM((1, LANES), F32)],
        compiler_params=pltpu.CompilerParams(
            dimension_semantics=("arbitrary",), vmem_limit_bytes=VMEM_LIMIT),
        name="out_proj_route",
    )(x2, oa, ob, wa, wb, g2, wrh, wrl, br, tril)


def _dispatch_body(d0_ref, d1_ref, t_ref, xs_ref, sem):
    tm = d0_ref.shape[0]

    def copy(i, d_ref):
        dst = pl.multiple_of(d_ref[i] * SUBLANES, SUBLANES)
        return pltpu.make_async_copy(t_ref.at[pl.ds(pl.multiple_of(i * SUBLANES, SUBLANES), SUBLANES)],
                                     xs_ref.at[pl.ds(dst, SUBLANES)], sem)

    def start(i, carry):
        copy(i, d0_ref).start(priority=0)
        copy(i, d1_ref).start(priority=1)
        return carry

    lax.fori_loop(0, tm, start, 0, unroll=8)
    for _ in range(2):
        pltpu.make_async_copy(t_ref, xs_ref.at[pl.ds(0, tm * SUBLANES)], sem).wait()


def _dispatch(d0, d1, t):
    n = d0.shape[0]
    tm = min(512, n)
    smem = pl.BlockSpec((tm,), lambda i: (i,), memory_space=pltpu.SMEM)
    return pl.pallas_call(
        _dispatch_body,
        grid=(n // tm,),
        in_specs=[smem, smem, pl.BlockSpec((tm * SUBLANES, LANES), lambda i: (i, 0))],
        out_specs=pl.BlockSpec(memory_space=pl.ANY),
        out_shape=jax.ShapeDtypeStruct((2 * n * SUBLANES, LANES), t.dtype),
        scratch_shapes=[pltpu.SemaphoreType.DMA],
        compiler_params=pltpu.CompilerParams(dimension_semantics=("arbitrary",)),
        name="moe_dispatch",
    )(d0, d1, t)


def _experts_body(ub_ref, ue_ref, gs_ref, nu_ref, x_ref, wg_ref, wu_ref, wd_ref, y_ref,
                  wg_s, wu_s, wd_s):
    w = pl.program_id(0)
    bm = x_ref.shape[0] // SUBLANES

    @pl.when(w < nu_ref[0])
    def _():
        e = ue_ref[w]
        blk = ub_ref[w]
        prev = jnp.maximum(w - 1, 0)

        @pl.when(jnp.logical_or(w == 0, ue_ref[prev] != e))
        def _():
            wg_s[...] = wg_ref[...].astype(BF16)
            wu_s[...] = wu_ref[...].astype(BF16)
            wd_s[...] = wd_ref[...].astype(BF16)

        rows = blk * bm + lax.broadcasted_iota(I32, (bm, 1), 0)
        valid = (rows >= gs_ref[e]) & (rows < gs_ref[e + 1])
        xb = _tile_rows_load(x_ref, bm).astype(BF16)
        gt = _dot(xb, wg_s[...])
        up = _dot(xb, wu_s[...])
        act = (gt / (1.0 + jnp.exp(-gt))) * up
        y = jnp.where(valid, _dot(act.astype(BF16), wd_s[...]), 0.0)
        first = jnp.logical_or(w == 0, ub_ref[prev] != blk)

        @pl.when(first)
        def _():
            _tile_rows_store(y_ref, y)

        @pl.when(jnp.logical_not(first))
        def _():
            _tile_rows_store(y_ref, _tile_rows_load(y_ref, bm) + y)


def _experts(unit_blk, unit_exp, gstart, n_units, xs, wg, wu, wd, bm):
    m = xs.shape[0] // SUBLANES
    d, de = wg.shape[1], wg.shape[2]
    max_units = unit_blk.shape[0]
    grid_spec = pltpu.PrefetchScalarGridSpec(
        num_scalar_prefetch=4,
        grid=(max_units,),
        in_specs=[
            pl.BlockSpec((bm * SUBLANES, LANES), lambda w, ub, ue, gs, nu: (ub[w], 0)),
            pl.BlockSpec((None, d, de), lambda w, ub, ue, gs, nu: (ue[w], 0, 0)),
            pl.BlockSpec((None, d, de), lambda w, ub, ue, gs, nu: (ue[w], 0, 0)),
            pl.BlockSpec((None, de, d), lambda w, ub, ue, gs, nu: (ue[w], 0, 0)),
        ],
        out_specs=pl.BlockSpec((bm * SUBLANES, LANES), lambda w, ub, ue, gs, nu: (ub[w], 0)),
        scratch_shapes=[pltpu.VMEM((d, de), BF16), pltpu.VMEM((d, de), BF16),
                        pltpu.VMEM((de, d), BF16)],
    )
    return pl.pallas_call(
        _experts_body,
        grid_spec=grid_spec,
        out_shape=jax.ShapeDtypeStruct((m * SUBLANES, LANES), F32),
        compiler_params=pltpu.CompilerParams(
            dimension_semantics=("arbitrary",), vmem_limit_bytes=VMEM_LIMIT),
        name="moe_experts",
    )(unit_blk, unit_exp, gstart, n_units, xs, wg, wu, wd)


def _combine_body(d0_ref, d1_ref, y_ref, h_ref, gate_ref, o_ref, b0, b1, sem):
    tm = d0_ref.shape[0]

    def copy(i, d_ref, buf):
        src = pl.multiple_of(d_ref[i] * SUBLANES, SUBLANES)
        return pltpu.make_async_copy(y_ref.at[pl.ds(src, SUBLANES)],
                                     buf.at[pl.ds(pl.multiple_of(i * SUBLANES, SUBLANES), SUBLANES)], sem)

    def start(i, carry):
        copy(i, d0_ref, b0).start(priority=0)
        copy(i, d1_ref, b1).start(priority=1)
        return carry

    lax.fori_loop(0, tm, start, 0, unroll=8)
    for buf in (b0, b1):
        pltpu.make_async_copy(y_ref.at[pl.ds(0, tm * SUBLANES)], buf, sem).wait()
    gate = gate_ref[...]
    o_ref[...] = (h_ref[...] + _tile_rows_load(b0, tm) * gate[:, 0:1]
                  + _tile_rows_load(b1, tm) * gate[:, 1:2])


def _combine(d0, d1, y, h, gate):
    n, d = h.shape
    tm = min(512, n)
    smem = pl.BlockSpec((tm,), lambda i: (i,), memory_space=pltpu.SMEM)
    return pl.pallas_call(
        _combine_body,
        grid=(n // tm,),
        in_specs=[smem, smem, pl.BlockSpec(memory_space=pl.ANY),
                  pl.BlockSpec((tm, d), lambda i: (i, 0)),
                  pl.BlockSpec((tm, LANES), lambda i: (i, 0))],
        out_specs=pl.BlockSpec((tm, d), lambda i: (i, 0)),
        out_shape=jax.ShapeDtypeStruct((n, d), F32),
        scratch_shapes=[pltpu.VMEM((tm * SUBLANES, LANES), F32), pltpu.VMEM((tm * SUBLANES, LANES), F32),
                        pltpu.SemaphoreType.DMA],
        compiler_params=pltpu.CompilerParams(
            dimension_semantics=("arbitrary",), vmem_limit_bytes=VMEM_LIMIT),
        name="moe_combine",
    )(d0, d1, y, h, gate)


def _rope_tables(seq):
    half = ROT_DIM // 2
    inv = ROPE_THETA ** (-jnp.arange(0, ROT_DIM, 2, dtype=F32) / ROT_DIM)
    ang = jnp.arange(seq, dtype=F32)[:, None] * inv[None, :]
    cos, sin = jnp.cos(ang), jnp.sin(ang)
    ones = jnp.ones((seq, DH_A - ROT_DIM), F32)
    zeros = jnp.zeros((seq, DH_A - half), F32)
    zeros_h = jnp.zeros((seq, half), F32)
    zeros_r = jnp.zeros((seq, DH_A - ROT_DIM), F32)
    cos_c = jnp.concatenate([cos, cos, ones], axis=1)
    sa_c = jnp.concatenate([-sin, zeros], axis=1)
    sb_c = jnp.concatenate([zeros_h, sin, zeros_r], axis=1)
    tile = lambda t: jnp.concatenate([t, t], axis=1)
    return tile(cos_c), tile(sa_c), tile(sb_c)


def _layer(x, l, lambda_init, ln1_g, w_in, q_norm_g, k_norm_g, lambda_q1, lambda_k1, lambda_q2,
           lambda_k2, subln_g, shift_mu, w0, w_lora_up, a0, a_lora_up, g_lora_up, k_k, k_a, r_k,
           ln_x_g, ln_x_b, w_out, ln2_g, w_router_group, b_router_group, w_router_expert,
           b_router_expert, w_exp_gate, w_exp_up, w_exp_down):
    b, s, d = x.shape
    n = b * s
    d_attn3 = 3 * (d // 2)
    d_attn = d // 2
    d_r = w0.shape[1]
    x2 = x.reshape(n, d)
    row = lambda a: a[l].reshape(1, -1)

    w_in_b = w_in[l].astype(BF16)
    ua, ub = _in_proj(x2, row(ln1_g), w_in_b[:, :d_attn3], w_in_b[:, d_attn3:])

    cosf, sina, sinb = _rope_tables(s)
    tile2 = lambda a: jnp.concatenate([a[l], a[l]]).reshape(1, -1)
    lvec = jnp.stack([lambda_q1[l], lambda_k1[l], lambda_q2[l], lambda_k2[l]])
    o_a = _diff_attn(ua.reshape(b, s, d_attn3), cosf, sina, sinb, tile2(q_norm_g), tile2(k_norm_g),
                     lvec, row(subln_g), lambda_init)

    dl = w_lora_up.shape[1]
    zeros_l = jnp.zeros((dl, d_r), F32)
    wlu = jnp.concatenate([w_lora_up[l], zeros_l], axis=0).astype(BF16)
    alu = jnp.concatenate([zeros_l, a_lora_up[l]], axis=0).astype(BF16)
    hid = jnp.arange(d_r) // RWKV_HEAD
    bd = (hid[:, None] == hid[None, :]).astype(BF16)
    tr = min(RWKV_ROWS, s)
    ti = jnp.arange(tr)
    tri = ((ti[:, None] >= ti[None, :]) & (ti[:, None] // CHUNK == ti[None, :] // CHUNK)).astype(BF16)
    o_b = _rwkv7(ub.reshape(b, s, -1), row(shift_mu), row(w0), wlu, row(a0), alu,
                 g_lora_up[l].astype(BF16), row(k_k), row(k_a), row(r_k), row(ln_x_g), row(ln_x_b),
                 bd, tri, CHUNK)

    w_out_b = w_out[l].astype(BF16)
    wr = jnp.concatenate([w_router_group[l], w_router_expert[l],
                          jnp.zeros((d, LANES - N_GROUPS - N_EXPERTS), F32)], axis=1)
    wr_hi = wr.astype(BF16)
    wr_lo = (wr - wr_hi.astype(F32)).astype(BF16)
    br = jnp.concatenate([b_router_group[l], b_router_expert[l],
                          jnp.zeros((LANES - N_GROUPS - N_EXPERTS,), F32)]).reshape(1, LANES)
    tm_r = min(512, n)
    tril = (jnp.arange(tm_r)[:, None] > jnp.arange(tm_r)[None, :]).astype(BF16)
    h, t, eid, gate, rank, cnt = _out_proj_route(
        x2, o_a.reshape(n, d_attn), o_b.reshape(n, d_r), w_out_b[:d_attn], w_out_b[d_attn:],
        row(ln2_g), wr_hi, wr_lo, br, tril)

    counts = cnt[0, :N_EXPERTS].astype(I32)
    gend = jnp.cumsum(counts)
    gstart = jnp.concatenate([jnp.zeros((1,), I32), gend]).astype(I32)
    onehot = (eid[:, :2, None] == jnp.arange(N_EXPERTS)[None, None, :])
    dest = jnp.sum(jnp.where(onehot, gstart[:N_EXPERTS], 0), axis=-1) + rank[:, :2]
    d0 = dest[:, 0].astype(I32)
    d1 = dest[:, 1].astype(I32)

    m = 2 * n
    bm = min(512, m)
    n_blk = m // bm
    max_units = n_blk + N_EXPERTS - 1
    first_blk = gstart[:N_EXPERTS] // bm
    last_blk = jnp.where(counts > 0, (gend - 1) // bm, first_blk)
    units_e = jnp.where(counts > 0, last_blk - first_blk + 1, 0)
    unit_end = jnp.cumsum(units_e)
    unit_start = unit_end - units_e
    n_units = unit_end[-1]
    w_ids = jnp.arange(max_units)
    u_exp = jnp.minimum(jnp.sum((unit_end[None, :] <= w_ids[:, None]).astype(I32), axis=1),
                        N_EXPERTS - 1)
    u_blk = first_blk[u_exp] + (w_ids - unit_start[u_exp])
    last = jnp.maximum(n_units - 1, 0)
    pad = w_ids >= n_units
    u_exp = jnp.where(pad, u_exp[last], u_exp).astype(I32)
    u_blk = jnp.where(pad, u_blk[last], u_blk).astype(I32)

    xs = _dispatch(d0, d1, t)
    y = _experts(u_blk, u_exp, gstart, n_units.reshape(1).astype(I32), xs,
                 w_exp_gate[l], w_exp_up[l], w_exp_down[l], bm)
    out = _combine(d0, d1, y, h, gate)
    return out.reshape(b, s, d)


def kernel(x, ln1_g, w_in, q_norm_g, k_norm_g, lambda_q1, lambda_k1, lambda_q2, lambda_k2, subln_g, shift_mu, w0, w_lora_up, a0, a_lora_up, g_lora_up, k_k, k_a, r_k, ln_x_g, ln_x_b, w_out, ln2_g, w_router_group, b_router_group, w_router_expert, b_router_expert, w_exp_gate, w_exp_up, w_exp_down):
    h = x
    for l in range(ln1_g.shape[0]):
        lambda_init = 0.8 - 0.6 * math.exp(-0.3 * l)
        h = _layer(h, l, lambda_init, ln1_g, w_in, q_norm_g, k_norm_g, lambda_q1, lambda_k1,
                   lambda_q2, lambda_k2, subln_g, shift_mu, w0, w_lora_up, a0, a_lora_up, g_lora_up,
                   k_k, k_a, r_k, ln_x_g, ln_x_b, w_out, ln2_g, w_router_group, b_router_group,
                   w_router_expert, b_router_expert, w_exp_gate, w_exp_up, w_exp_down)
    return h
```

```python
import functools
import math

import jax
import jax.numpy as jnp
from jax import lax
from jax.experimental import pallas as pl
from jax.experimental.pallas import tpu as pltpu

F32 = jnp.float32
BF16 = jnp.bfloat16
I32 = jnp.int32

CHUNK = 64
DH_A = 64
DV_A = 128
ROT_DIM = 16
ROPE_THETA = 500000.0
QK_NORM_EPS = 1e-6
SUBLN_EPS = 1e-5
RWKV_HEAD = 64
LN_X_EPS = 64e-5
RWKV_ROWS = 256
N_GROUPS = 4
EXPERTS_PER_GROUP = 8
N_EXPERTS = 32
NORM_EPS = 1e-6

LANES = 128
SUBLANES = 8
VMEM_LIMIT = 56 * 1024 * 1024

_NEG = -1e30


def _dot(a, b):
    return jnp.dot(a, b, preferred_element_type=F32)


def _dot_nt(a, b):
    return lax.dot_general(a, b, (((1,), (1,)), ((), ())), preferred_element_type=F32)


def _bdot(a, b):
    return _dot(a.astype(BF16), b.astype(BF16))


def _split(x):
    hi = x.astype(BF16)
    lo = (x - hi.astype(F32)).astype(BF16)
    return hi, lo


def _tile_rows_load(ref, rows):
    return jnp.concatenate([ref[pl.ds(j, rows, stride=SUBLANES), :] for j in range(SUBLANES)], axis=1)


def _tile_rows_store(ref, x):
    rows = x.shape[0]
    for j in range(SUBLANES):
        ref[pl.ds(j, rows, stride=SUBLANES), :] = x[:, j * LANES:(j + 1) * LANES]


def _in_proj_body(x_ref, g_ref, wa_ref, wb_ref, ua_ref, ub_ref):
    x = x_ref[...]
    ms = jnp.mean(x * x, axis=-1, keepdims=True)
    xn = (x * lax.rsqrt(ms + NORM_EPS) * g_ref[...]).astype(BF16)
    ua_ref[...] = _dot(xn, wa_ref[...])
    ub_ref[...] = _dot(xn, wb_ref[...])


def _in_proj(x2, g, wa, wb):
    n, d = x2.shape
    tm = min(512, n)
    na, nb = wa.shape[1], wb.shape[1]
    return pl.pallas_call(
        _in_proj_body,
        grid=(n // tm,),
        in_specs=[
            pl.BlockSpec((tm, d), lambda i: (i, 0)),
            pl.BlockSpec((1, d), lambda i: (0, 0)),
            pl.BlockSpec((d, na), lambda i: (0, 0)),
            pl.BlockSpec((d, nb), lambda i: (0, 0)),
        ],
        out_specs=[
            pl.BlockSpec((tm, na), lambda i: (i, 0)),
            pl.BlockSpec((tm, nb), lambda i: (i, 0)),
        ],
        out_shape=[jax.ShapeDtypeStruct((n, na), F32), jax.ShapeDtypeStruct((n, nb), F32)],
        compiler_params=pltpu.CompilerParams(
            dimension_semantics=("arbitrary",), vmem_limit_bytes=VMEM_LIMIT),
        name="in_proj",
    )(x2, g, wa, wb)


def _attn_body(q_ref, k_ref, v_ref, cos_ref, sa_ref, sb_ref, gq_ref, gk_ref, l_ref, sg_ref,
               o_ref, q0_s, q1_s, k_s, v_s, *, lambda_init, bq):
    s_len = q_ref.shape[0]
    lane = lax.broadcasted_iota(I32, (1, LANES), 1)
    lo = lane < DH_A

    def prep(t, g):
        t2 = t * t
        s_lo = jnp.sum(jnp.where(lo, t2, 0.0), axis=-1, keepdims=True)
        s_hi = jnp.sum(jnp.where(lo, 0.0, t2), axis=-1, keepdims=True)
        ms = jnp.where(lo, s_lo, s_hi) * (1.0 / DH_A)
        tn = t * lax.rsqrt(ms + QK_NORM_EPS) * g
        return (tn * cos_ref[...] + pltpu.roll(tn, LANES - ROT_DIM // 2, 1) * sa_ref[...]
                + pltpu.roll(tn, ROT_DIM // 2, 1) * sb_ref[...])

    q = prep(q_ref[...], gq_ref[...]) * (DH_A ** -0.5 * math.log2(math.e))
    q0_s[...] = jnp.where(lo, q, 0.0).astype(BF16)
    q1_s[...] = jnp.where(lo, 0.0, q).astype(BF16)
    k_s[...] = prep(k_ref[...], gk_ref[...]).astype(BF16)
    v_s[...] = v_ref[...].astype(BF16)

    l = l_ref[...]
    lam = (jnp.exp(jnp.sum(l[0:1] * l[1:2], axis=-1, keepdims=True))
           - jnp.exp(jnp.sum(l[2:3] * l[3:4], axis=-1, keepdims=True)) + lambda_init)

    chunk_shift = int(math.log2(CHUNK))
    rr = lax.broadcasted_iota(I32, (bq, bq), 0) >> chunk_shift
    cc = lax.broadcasted_iota(I32, (bq, bq), 1) >> chunk_shift
    diag_ok = cc <= rr

    diag_ok2 = jnp.concatenate([diag_ok, diag_ok], axis=0)

    for i in range(s_len // bq):
        r0 = i * bq
        qq = jnp.concatenate([q0_s[r0:r0 + bq, :], q1_s[r0:r0 + bq, :]], axis=0)
        m = l_run = acc = None
        for j in range(i + 1):
            c0 = j * bq
            s = _dot_nt(qq, k_s[c0:c0 + bq, :])
            if j == i:
                s = jnp.where(diag_ok2, s, _NEG)
            m_tile = jnp.max(s, axis=-1, keepdims=True)
            if j == 0:
                m = m_tile
                e = jnp.exp2(s - m)
                l_run = jnp.sum(e, axis=-1, keepdims=True)
                acc = _dot(e.astype(BF16), v_s[c0:c0 + bq, :])
            else:
                m_new = jnp.maximum(m, m_tile)
                alpha = jnp.exp2(m - m_new)
                e = jnp.exp2(s - m_new)
                l_run = l_run * alpha + jnp.sum(e, axis=-1, keepdims=True)
                acc = acc * alpha + _dot(e.astype(BF16), v_s[c0:c0 + bq, :])
                m = m_new
        w0 = 1.0 / l_run[:bq]
        w1 = lam / l_run[bq:]
        o = acc[:bq] * w0 - acc[bq:] * w1
        o = o * lax.rsqrt(jnp.mean(o * o, axis=-1, keepdims=True) + SUBLN_EPS)
        o_ref[r0:r0 + bq, :] = o * (sg_ref[...] * (1.0 - lambda_init))


def _diff_attn(ua3, cosf, sina, sinb, gq, gk, lvec, sg, lambda_init):
    b, s, _ = ua3.shape
    h_a = ua3.shape[2] // (3 * DV_A)
    bq = min(256, s)
    blk = lambda off: pl.BlockSpec((None, s, DV_A), lambda bi, hi: (bi, 0, off + hi))
    const = lambda shape: pl.BlockSpec(shape, lambda bi, hi: (0,) * len(shape))
    return pl.pallas_call(
        functools.partial(_attn_body, lambda_init=lambda_init, bq=bq),
        grid=(b, h_a),
        in_specs=[blk(0), blk(h_a), blk(2 * h_a),
                  const((s, LANES)), const((s, LANES)), const((s, LANES)),
                  const((1, LANES)), const((1, LANES)), const((4, DH_A)), const((1, LANES))],
        out_specs=pl.BlockSpec((None, s, DV_A), lambda bi, hi: (bi, 0, hi)),
        out_shape=jax.ShapeDtypeStruct((b, s, h_a * DV_A), F32),
        scratch_shapes=[pltpu.VMEM((s, LANES), BF16)] * 4,
        compiler_params=pltpu.CompilerParams(
            dimension_semantics=("arbitrary", "arbitrary"), vmem_limit_bytes=VMEM_LIMIT),
        name="diff_attn",
    )(ua3, ua3, ua3, cosf, sina, sinb, gq, gk, lvec, sg)


def _rwkv_body(u_ref, mu_ref, w0_ref, wlu_ref, a0_ref, alu_ref, glu_ref, kkw_ref, ka_ref, rk_ref,
               lg_ref, lb_ref, bd_ref, tri_ref, o_ref, prev_s, st_s, *, c_len):
    tr = u_ref.shape[0]
    d_r = o_ref.shape[1]
    n_pair = d_r // LANES
    n_ch = tr // c_len

    @pl.when(pl.program_id(1) == 0)
    def _():
        prev_s[...] = jnp.zeros_like(prev_s)
        st_s[...] = jnp.zeros_like(st_s)

    u = u_ref[...]
    row = lax.broadcasted_iota(I32, (tr, 1), 0)
    u_prev = jnp.where(row == 0, prev_s[...], pltpu.roll(u, 1, 0))
    prev_s[...] = u[tr - 1:tr, :]
    us = u + (u_prev - u) * mu_ref[...]

    r = us[:, 0:d_r]
    k = us[:, d_r:2 * d_r]
    v = us[:, 2 * d_r:3 * d_r]
    lo_in = us[:, 3 * d_r:3 * d_r + LANES]
    g_in = us[:, 3 * d_r + LANES:3 * d_r + 2 * LANES]

    bd = bd_ref[...]

    def head_sum(x, two_pass=False):
        hi, lo = _split(x)
        return _dot(hi, bd) + _dot(lo, bd) if two_pass else _dot(hi, bd)

    z = -(w0_ref[...] + _bdot(jnp.tanh(lo_in), wlu_ref[...]))
    softplus = jnp.maximum(z, 0.0) + jnp.log(1.0 + jnp.exp(-jnp.abs(z)))
    logw = -jnp.exp(-softplus - 0.5)
    a = 1.0 / (1.0 + jnp.exp(-(a0_ref[...] + _bdot(lo_in, alu_ref[...]))))
    g = _bdot(1.0 / (1.0 + jnp.exp(-g_in)), glu_ref[...])
    kk_raw = k * kkw_ref[...]
    kk = kk_raw / jnp.maximum(jnp.sqrt(head_sum(kk_raw * kk_raw, two_pass=True)), 1e-12)
    k_mod = k * (1.0 + (a - 1.0) * ka_ref[...])

    tri = tri_ref[...]
    w_hi, w_lo = _split(logw)
    cum = _dot(tri, w_hi) + _dot(tri, w_lo)
    last_rows = [cum[(c + 1) * c_len - 1:(c + 1) * c_len, :] for c in range(n_ch)]
    cum_last = jnp.concatenate([jnp.broadcast_to(lr, (c_len, d_r)) for lr in last_rows], axis=0)
    e_neg = jnp.exp(-cum)
    e_rem = jnp.exp(cum_last - cum)
    a_t = -kk * jnp.exp(cum - logw)
    r_t = r * jnp.exp(cum)
    b_raw = kk * a
    b_t = b_raw * e_neg
    k_t = k_mod * e_neg
    b_2 = b_raw * e_rem
    k_2 = k_mod * e_rem

    lane = lax.broadcasted_iota(I32, (1, LANES), 1)
    head0 = lane < RWKV_HEAD
    two_c = 2 * c_len
    rr = lax.broadcasted_iota(I32, (two_c, two_c), 0)
    cc = lax.broadcasted_iota(I32, (two_c, two_c), 1)
    t_i = rr & (c_len - 1)
    s_i = cc & (c_len - 1)
    strict = t_i > s_i
    incl = t_i >= s_i
    eye = rr == cc

    def stack(x, c, j):
        xp = x[c * c_len:(c + 1) * c_len, j * LANES:(j + 1) * LANES]
        return jnp.concatenate([jnp.where(head0, xp, 0.0), jnp.where(head0, 0.0, xp)], axis=0)

    units = [(c, j) for c in range(n_ch) for j in range(n_pair)]
    n_double = int(math.log2(c_len))

    a_s = [stack(a_t, c, j) for c, j in units]
    r_s = [stack(r_t, c, j) for c, j in units]
    v_b = [stack(v, c, j).astype(BF16) for c, j in units]
    gram = [_dot_nt(jnp.concatenate([a_s[i], r_s[i]], axis=0).astype(BF16),
                    jnp.concatenate([stack(b_t, c, j), stack(k_t, c, j)], axis=0).astype(BF16))
            for i, (c, j) in enumerate(units)]
    p = [jnp.where(strict, gm[:two_c, :two_c], 0.0) for gm in gram]
    m_rb = [jnp.where(incl, gm[two_c:, :two_c], 0.0).astype(BF16) for gm in gram]
    m_rk = [jnp.where(incl, gm[two_c:, two_c:], 0.0).astype(BF16) for gm in gram]
    vv = [_dot(jnp.where(strict, gm[:two_c, two_c:], 0.0).astype(BF16), vb)
          for gm, vb in zip(gram, v_b)]
    assert c_len == RWKV_HEAD and 2 * c_len == LANES
    half = LANES // 2
    lo_half = lax.broadcasted_iota(I32, (1, LANES), 1) < half
    keep = lax.broadcasted_iota(I32, (1, 2 * LANES), 1) >= half
    zeros_b = jnp.zeros((c_len, 2 * LANES), BF16)
    z = []
    for i in range(len(units)):
        for hp in range(2):
            rs = slice(hp * c_len, (hp + 1) * c_len)
            p_r, a_r, v_r = p[i][rs, :], a_s[i][rs, :], vv[i][rs, :]
            if hp == 0:
                z.append(jnp.concatenate([p_r + pltpu.roll(a_r, half, 1), v_r], axis=1))
            else:
                z.append(jnp.concatenate([pltpu.roll(p_r, half, 1) + a_r, pltpu.roll(v_r, half, 1)], axis=1))
    for it in range(n_double):
        lhs = [jnp.where(lo_half, zi[:, :LANES], 0.0).astype(BF16) for zi in z]
        rhs = [jnp.concatenate([zi.astype(BF16), zeros_b], axis=0) for zi in z]
        z = [_dot(li, ri) + jnp.where(keep, zi, 0.0) for li, ri, zi in zip(lhs, rhs, z)]
    x_b = []
    for i in range(len(units)):
        z0, z1 = z[2 * i], z[2 * i + 1]
        rows0 = jnp.concatenate([pltpu.roll(jnp.where(lo_half, 0.0, z0[:, :LANES]), half, 1),
                                 z0[:, LANES:]], axis=1)
        rows1 = jnp.concatenate([jnp.where(lo_half, 0.0, z1[:, :LANES]),
                                 pltpu.roll(z1[:, LANES:], half, 1)], axis=1)
        x_b.append(jnp.concatenate([rows0, rows1], axis=0).astype(BF16))
    zeros_v = jnp.zeros((two_c, two_c), BF16)
    rhs2 = [jnp.concatenate([x_b[i], jnp.concatenate([zeros_v, v_b[i]], axis=1)], axis=0)
            for i in range(len(units))]
    mx = [_dot(jnp.concatenate([m_rb[i], m_rk[i]], axis=1), rhs2[i]) for i in range(len(units))]
    bx = [_dot(jnp.concatenate([stack(b_2, c, j).T, stack(k_2, c, j).T], axis=1).astype(BF16), rhs2[i])
          for i, (c, j) in enumerate(units)]

    y_rows = []
    st = [st_s[j] for j in range(n_pair)]
    for c in range(n_ch):
        p_c = jnp.exp(last_rows[c])
        y_pairs = []
        for j in range(n_pair):
            i = c * n_pair + j
            r_h = (r_s[i] + mx[i][:, :two_c]).astype(BF16)
            y_h = mx[i][:, two_c:]
            g_m = (jnp.where(eye, p_c[:, j * LANES:(j + 1) * LANES], 0.0) + bx[i][:, :two_c]).astype(BF16)
            h_m = bx[i][:, two_c:]
            st_b = st[j].astype(BF16)
            y_s = _dot(r_h, st_b) + y_h
            st[j] = _dot(g_m, st_b) + h_m
            y_pairs.append(y_s[:c_len] + y_s[c_len:])
        y_rows.append(jnp.concatenate(y_pairs, axis=1))
    for j in range(n_pair):
        st_s[j] = st[j]
    y = jnp.concatenate(y_rows, axis=0)

    inv_n = 1.0 / RWKV_HEAD
    mu = head_sum(y) * inv_n
    dlt = y - mu
    var = head_sum(dlt * dlt) * inv_n
    yn = dlt * lax.rsqrt(var + LN_X_EPS) * lg_ref[...] + lb_ref[...]
    bonus = head_sum(r * k_mod * rk_ref[...]) * v
    o_ref[...] = (yn + bonus) * g


def _rwkv7(ub3, mu, w0, wlu, a0, alu, glu, kkw, ka, rk, lg, lb, bd, tri, c_len):
    b, s, cols = ub3.shape
    d_r = w0.shape[1]
    tr = tri.shape[0]
    const = lambda a: pl.BlockSpec(a.shape, lambda bi, ci: (0,) * a.ndim)
    params = (mu, w0, wlu, a0, alu, glu, kkw, ka, rk, lg, lb, bd, tri)
    return pl.pallas_call(
        functools.partial(_rwkv_body, c_len=c_len),
        grid=(b, s // tr),
        in_specs=[pl.BlockSpec((None, tr, cols), lambda bi, ci: (bi, ci, 0))]
                 + [const(a) for a in params],
        out_specs=pl.BlockSpec((None, tr, d_r), lambda bi, ci: (bi, ci, 0)),
        out_shape=jax.ShapeDtypeStruct((b, s, d_r), F32),
        scratch_shapes=[pltpu.VMEM((1, cols), F32),
                        pltpu.VMEM((d_r // LANES, LANES, LANES), F32)],
        compiler_params=pltpu.CompilerParams(
            dimension_semantics=("arbitrary", "arbitrary"), vmem_limit_bytes=VMEM_LIMIT),
        name="rwkv7",
    )(ub3, *params)


def _out_body(x_ref, oa_ref, ob_ref, wa_ref, wb_ref, g2_ref, wrh_ref, wrl_ref, br_ref, tril_ref,
              h_ref, t_ref, eid_ref, gate_ref, rank_ref, cnt_ref, run_s):
    tm = x_ref.shape[0]

    @pl.when(pl.program_id(0) == 0)
    def _():
        run_s[...] = jnp.zeros_like(run_s)

    h = x_ref[...] + _bdot(oa_ref[...], wa_ref[...]) + _bdot(ob_ref[...], wb_ref[...])
    h_ref[...] = h
    t = h * lax.rsqrt(jnp.mean(h * h, axis=-1, keepdims=True) + NORM_EPS) * g2_ref[...]
    _tile_rows_store(t_ref, t)

    t_hi, t_lo = _split(t)
    wrh = wrh_ref[...]
    lg = _dot(t_hi, wrh) + _dot(t_lo, wrh) + _dot(t_hi, wrl_ref[...]) + br_ref[...]

    lane_i = lax.broadcasted_iota(I32, (1, LANES), 1)
    lane = lane_i.astype(F32)
    big = float(LANES)

    def first_index(mask):
        return jnp.min(jnp.where(mask, lane, big), axis=-1, keepdims=True)

    def masked_softmax(mask):
        m = jnp.max(jnp.where(mask, lg, _NEG), axis=-1, keepdims=True)
        e = jnp.where(mask, jnp.exp(lg - m), 0.0)
        return e / jnp.sum(e, axis=-1, keepdims=True)

    is_g = lane_i < N_GROUPS
    gprob = masked_softmax(is_g)
    p_group = jnp.max(gprob, axis=-1, keepdims=True)
    gsel = first_index(is_g & (gprob == p_group))
    base = N_GROUPS + gsel * EXPERTS_PER_GROUP
    in_e = (lane >= base) & (lane < base + EXPERTS_PER_GROUP)
    eprob = masked_softmax(in_e)
    p1 = jnp.max(eprob, axis=-1, keepdims=True)
    i1 = first_index(in_e & (eprob == p1))
    rest = in_e & (lane != i1)
    p2 = jnp.max(jnp.where(rest, eprob, -1.0), axis=-1, keepdims=True)
    i2 = first_index(rest & (eprob == p2))
    den = p1 + p2
    g1 = p_group * p1 / den
    g2 = p_group * p2 / den
    e1 = i1 - N_GROUPS
    e2 = i2 - N_GROUPS

    oh1 = (lane == e1).astype(F32)
    oh2 = (lane == e2).astype(F32)
    tril = tril_ref[...]
    before1 = _dot(tril, oh1.astype(BF16))
    tot1 = jnp.sum(oh1, axis=0, keepdims=True)
    before2 = _dot(tril, oh2.astype(BF16)) + tot1
    run = run_s[...]
    rk1 = jnp.sum(oh1 * (before1 + run), axis=-1, keepdims=True)
    rk2 = jnp.sum(oh2 * (before2 + run), axis=-1, keepdims=True)
    run = run + tot1 + jnp.sum(oh2, axis=0, keepdims=True)
    run_s[...] = run
    cnt_ref[...] = run

    sel0 = lane_i == 0
    sel1 = lane_i == 1
    eid_ref[...] = jnp.where(sel0, e1, jnp.where(sel1, e2, 0.0)).astype(I32)
    rank_ref[...] = jnp.where(sel0, rk1, jnp.where(sel1, rk2, 0.0)).astype(I32)
    gate_ref[...] = jnp.where(sel0, g1, jnp.where(sel1, g2, 0.0))


def _out_proj_route(x2, oa, ob, wa, wb, g2, wrh, wrl, br, tril):
    n, d = x2.shape
    tm = tril.shape[0]
    da, db = oa.shape[1], ob.shape[1]
    row = lambda w: pl.BlockSpec((tm, w), lambda i: (i, 0))
    const = lambda a: pl.BlockSpec(a.shape, lambda i: (0,) * a.ndim)
    return pl.pallas_call(
        _out_body,
        grid=(n // tm,),
        in_specs=[row(d), row(da), row(db), const(wa), const(wb), const(g2), const(wrh), const(wrl),
                  const(br), const(tril)],
        out_specs=[row(d), pl.BlockSpec((tm * SUBLANES, LANES), lambda i: (i, 0)),
                   row(LANES), row(LANES), row(LANES), pl.BlockSpec((1, LANES), lambda i: (0, 0))],
        out_shape=[jax.ShapeDtypeStruct((n, d), F32), jax.ShapeDtypeStruct((n * SUBLANES, LANES), F32),
                   jax.ShapeDtypeStruct((n, LANES), I32), jax.ShapeDtypeStruct((n, LANES), F32),
                   jax.ShapeDtypeStruct((n, LANES), I32), jax.ShapeDtypeStruct((1, LANES), F32)],
        scratch_shapes=[pltpu.VMEM((1, LANES), F32)],
        compiler_params=pltpu.CompilerParams(
            dimension_semantics=("arbitrary",), vmem_limit_bytes=VMEM_LIMIT),
        name="out_proj_route",
    )(x2, oa, ob, wa, wb, g2, wrh, wrl, br, tril)


def _dispatch_body(d0_ref, d1_ref, t_ref, xs_ref, sem):
    tm = d0_ref.shape[0]

    def copy(i, d_ref):
        dst = pl.multiple_of(d_ref[i] * SUBLANES, SUBLANES)
        return pltpu.make_async_copy(t_ref.at[pl.ds(pl.multiple_of(i * SUBLANES, SUBLANES), SUBLANES)],
                                     xs_ref.at[pl.ds(dst, SUBLANES)], sem)

    def start(i, carry):
        copy(i, d0_ref).start(priority=0)
        copy(i, d1_ref).start(priority=1)
        return carry

    lax.fori_loop(0, tm, start, 0, unroll=8)
    for _ in range(2):
        pltpu.make_async_copy(t_ref, xs_ref.at[pl.ds(0, tm * SUBLANES)], sem).wait()


def _dispatch(d0, d1, t):
    n = d0.shape[0]
    tm = min(512, n)
    smem = pl.BlockSpec((tm,), lambda i: (i,), memory_space=pltpu.SMEM)
    return pl.pallas_call(
        _dispatch_body,
        grid=(n // tm,),
        in_specs=[smem, smem, pl.BlockSpec((tm * SUBLANES, LANES), lambda i: (i, 0))],
        out_specs=pl.BlockSpec(memory_space=pl.ANY),
        out_shape=jax.ShapeDtypeStruct((2 * n * SUBLANES, LANES), t.dtype),
        scratch_shapes=[pltpu.SemaphoreType.DMA],
        compiler_params=pltpu.CompilerParams(dimension_semantics=("arbitrary",)),
        name="moe_dispatch",
    )(d0, d1, t)


def _experts_body(ub_ref, ue_ref, gs_ref, nu_ref, x_ref, wg_ref, wu_ref, wd_ref, y_ref,
                  wg_s, wu_s, wd_s):
    w = pl.program_id(0)
    bm = x_ref.shape[0] // SUBLANES

    @pl.when(w < nu_ref[0])
    def _():
        e = ue_ref[w]
        blk = ub_ref[w]
        prev = jnp.maximum(w - 1, 0)

        @pl.when(jnp.logical_or(w == 0, ue_ref[prev] != e))
        def _():
            wg_s[...] = wg_ref[...].astype(BF16)
            wu_s[...] = wu_ref[...].astype(BF16)
            wd_s[...] = wd_ref[...].astype(BF16)

        rows = blk * bm + lax.broadcasted_iota(I32, (bm, 1), 0)
        valid = (rows >= gs_ref[e]) & (rows < gs_ref[e + 1])
        xb = _tile_rows_load(x_ref, bm).astype(BF16)
        gt = _dot(xb, wg_s[...])
        up = _dot(xb, wu_s[...])
        act = (gt / (1.0 + jnp.exp(-gt))) * up
        y = jnp.where(valid, _dot(act.astype(BF16), wd_s[...]), 0.0)
        first = jnp.logical_or(w == 0, ub_ref[prev] != blk)

        @pl.when(first)
        def _():
            _tile_rows_store(y_ref, y)

        @pl.when(jnp.logical_not(first))
        def _():
            _tile_rows_store(y_ref, _tile_rows_load(y_ref, bm) + y)


def _experts(unit_blk, unit_exp, gstart, n_units, xs, wg, wu, wd, bm):
    m = xs.shape[0] // SUBLANES
    d, de = wg.shape[1], wg.shape[2]
    max_units = unit_blk.shape[0]
    grid_spec = pltpu.PrefetchScalarGridSpec(
        num_scalar_prefetch=4,
        grid=(max_units,),
        in_specs=[
            pl.BlockSpec((bm * SUBLANES, LANES), lambda w, ub, ue, gs, nu: (ub[w], 0)),
            pl.BlockSpec((None, d, de), lambda w, ub, ue, gs, nu: (ue[w], 0, 0)),
            pl.BlockSpec((None, d, de), lambda w, ub, ue, gs, nu: (ue[w], 0, 0)),
            pl.BlockSpec((None, de, d), lambda w, ub, ue, gs, nu: (ue[w], 0, 0)),
        ],
        out_specs=pl.BlockSpec((bm * SUBLANES, LANES), lambda w, ub, ue, gs, nu: (ub[w], 0)),
        scratch_shapes=[pltpu.VMEM((d, de), BF16), pltpu.VMEM((d, de), BF16),
                        pltpu.VMEM((de, d), BF16)],
    )
    return pl.pallas_call(
        _experts_body,
        grid_spec=grid_spec,
        out_shape=jax.ShapeDtypeStruct((m * SUBLANES, LANES), F32),
        compiler_params=pltpu.CompilerParams(
            dimension_semantics=("arbitrary",), vmem_limit_bytes=VMEM_LIMIT),
        name="moe_experts",
    )(unit_blk, unit_exp, gstart, n_units, xs, wg, wu, wd)


def _combine_body(d0_ref, d1_ref, d0n_ref, d1n_ref, y_ref, h_ref, gate_ref, o_ref, buf, sem):
    tm = d0_ref.shape[0]
    tile = pl.program_id(0)
    n_tiles = pl.num_programs(0)

    def gather(da_ref, db_ref, slot):
        def copy(i, d_ref, k):
            src = pl.multiple_of(d_ref[i] * SUBLANES, SUBLANES)
            dst = pl.multiple_of(i * SUBLANES, SUBLANES)
            return pltpu.make_async_copy(y_ref.at[pl.ds(src, SUBLANES)],
                                         buf.at[slot, k, pl.ds(dst, SUBLANES)], sem.at[slot])

        def start(i, carry):
            copy(i, da_ref, 0).start(priority=0)
            copy(i, db_ref, 1).start(priority=1)
            return carry

        lax.fori_loop(0, tm, start, 0, unroll=8)

    slot = tile % 2

    @pl.when(tile == 0)
    def _():
        gather(d0_ref, d1_ref, 0)

    @pl.when(tile + 1 < n_tiles)
    def _():
        gather(d0n_ref, d1n_ref, 1 - slot)

    for k in range(2):
        pltpu.make_async_copy(y_ref.at[pl.ds(0, tm * SUBLANES)], buf.at[slot, k], sem.at[slot]).wait()
    gate = gate_ref[...]
    o_ref[...] = (h_ref[...] + _tile_rows_load(buf.at[slot, 0], tm) * gate[:, 0:1]
                  + _tile_rows_load(buf.at[slot, 1], tm) * gate[:, 1:2])


def _combine(d0, d1, y, h, gate):
    n, d = h.shape
    tm = min(512, n)
    last = n // tm - 1
    smem = pl.BlockSpec((tm,), lambda i: (i,), memory_space=pltpu.SMEM)
    smem_next = pl.BlockSpec((tm,), lambda i: (jnp.minimum(i + 1, last),), memory_space=pltpu.SMEM)
    return pl.pallas_call(
        _combine_body,
        grid=(n // tm,),
        in_specs=[smem, smem, smem_next, smem_next, pl.BlockSpec(memory_space=pl.ANY),
                  pl.BlockSpec((tm, d), lambda i: (i, 0)),
                  pl.BlockSpec((tm, LANES), lambda i: (i, 0))],
        out_specs=pl.BlockSpec((tm, d), lambda i: (i, 0)),
        out_shape=jax.ShapeDtypeStruct((n, d), F32),
        scratch_shapes=[pltpu.VMEM((2, 2, tm * SUBLANES, LANES), F32), pltpu.SemaphoreType.DMA((2,))],
        compiler_params=pltpu.CompilerParams(
            dimension_semantics=("arbitrary",), vmem_limit_bytes=VMEM_LIMIT),
        name="moe_combine",
    )(d0, d1, d0, d1, y, h, gate)


def _rope_tables(seq):
    half = ROT_DIM // 2
    inv = ROPE_THETA ** (-jnp.arange(0, ROT_DIM, 2, dtype=F32) / ROT_DIM)
    ang = jnp.arange(seq, dtype=F32)[:, None] * inv[None, :]
    cos, sin = jnp.cos(ang), jnp.sin(ang)
    ones = jnp.ones((seq, DH_A - ROT_DIM), F32)
    zeros = jnp.zeros((seq, DH_A - half), F32)
    zeros_h = jnp.zeros((seq, half), F32)
    zeros_r = jnp.zeros((seq, DH_A - ROT_DIM), F32)
    cos_c = jnp.concatenate([cos, cos, ones], axis=1)
    sa_c = jnp.concatenate([-sin, zeros], axis=1)
    sb_c = jnp.concatenate([zeros_h, sin, zeros_r], axis=1)
    tile = lambda t: jnp.concatenate([t, t], axis=1)
    return tile(cos_c), tile(sa_c), tile(sb_c)


def _layer(x, l, lambda_init, ln1_g, w_in, q_norm_g, k_norm_g, lambda_q1, lambda_k1, lambda_q2,
           lambda_k2, subln_g, shift_mu, w0, w_lora_up, a0, a_lora_up, g_lora_up, k_k, k_a, r_k,
           ln_x_g, ln_x_b, w_out, ln2_g, w_router_group, b_router_group, w_router_expert,
           b_router_expert, w_exp_gate, w_exp_up, w_exp_down):
    b, s, d = x.shape
    n = b * s
    d_attn3 = 3 * (d // 2)
    d_attn = d // 2
    d_r = w0.shape[1]
    x2 = x.reshape(n, d)
    row = lambda a: a[l].reshape(1, -1)

    w_in_b = w_in[l].astype(BF16)
    ua, ub = _in_proj(x2, row(ln1_g), w_in_b[:, :d_attn3], w_in_b[:, d_attn3:])

    cosf, sina, sinb = _rope_tables(s)
    tile2 = lambda a: jnp.concatenate([a[l], a[l]]).reshape(1, -1)
    lvec = jnp.stack([lambda_q1[l], lambda_k1[l], lambda_q2[l], lambda_k2[l]])
    o_a = _diff_attn(ua.reshape(b, s, d_attn3), cosf, sina, sinb, tile2(q_norm_g), tile2(k_norm_g),
                     lvec, row(subln_g), lambda_init)

    dl = w_lora_up.shape[1]
    zeros_l = jnp.zeros((dl, d_r), F32)
    wlu = jnp.concatenate([w_lora_up[l], zeros_l], axis=0).astype(BF16)
    alu = jnp.concatenate([zeros_l, a_lora_up[l]], axis=0).astype(BF16)
    hid = jnp.arange(d_r) // RWKV_HEAD
    bd = (hid[:, None] == hid[None, :]).astype(BF16)
    tr = min(RWKV_ROWS, s)
    ti = jnp.arange(tr)
    tri = ((ti[:, None] >= ti[None, :]) & (ti[:, None] // CHUNK == ti[None, :] // CHUNK)).astype(BF16)
    o_b = _rwkv7(ub.reshape(b, s, -1), row(shift_mu), row(w0), wlu, row(a0), alu,
                 g_lora_up[l].astype(BF16), row(k_k), row(k_a), row(r_k), row(ln_x_g), row(ln_x_b),
                 bd, tri, CHUNK)

    w_out_b = w_out[l].astype(BF16)
    wr = jnp.concatenate([w_router_group[l], w_router_expert[l],
                          jnp.zeros((d, LANES - N_GROUPS - N_EXPERTS), F32)], axis=1)
    wr_hi = wr.astype(BF16)
    wr_lo = (wr - wr_hi.astype(F32)).astype(BF16)
    br = jnp.concatenate([b_router_group[l], b_router_expert[l],
                          jnp.zeros((LANES - N_GROUPS - N_EXPERTS,), F32)]).reshape(1, LANES)
    tm_r = min(512, n)
    tril = (jnp.arange(tm_r)[:, None] > jnp.arange(tm_r)[None, :]).astype(BF16)
    h, t, eid, gate, rank, cnt = _out_proj_route(
        x2, o_a.reshape(n, d_attn), o_b.reshape(n, d_r), w_out_b[:d_attn], w_out_b[d_attn:],
        row(ln2_g), wr_hi, wr_lo, br, tril)

    counts = cnt[0, :N_EXPERTS].astype(I32)
    gend = jnp.cumsum(counts)
    gstart = jnp.concatenate([jnp.zeros((1,), I32), gend]).astype(I32)
    onehot = (eid[:, :2, None] == jnp.arange(N_EXPERTS)[None, None, :])
    dest = jnp.sum(jnp.where(onehot, gstart[:N_EXPERTS], 0), axis=-1) + rank[:, :2]
    d0 = dest[:, 0].astype(I32)
    d1 = dest[:, 1].astype(I32)

    m = 2 * n
    bm = min(512, m)
    n_blk = m // bm
    max_units = n_blk + N_EXPERTS - 1
    first_blk = gstart[:N_EXPERTS] // bm
    last_blk = jnp.where(counts > 0, (gend - 1) // bm, first_blk)
    units_e = jnp.where(counts > 0, last_blk - first_blk + 1, 0)
    unit_end = jnp.cumsum(units_e)
    unit_start = unit_end - units_e
    n_units = unit_end[-1]
    w_ids = jnp.arange(max_units)
    u_exp = jnp.minimum(jnp.sum((unit_end[None, :] <= w_ids[:, None]).astype(I32), axis=1),
                        N_EXPERTS - 1)
    u_blk = first_blk[u_exp] + (w_ids - unit_start[u_exp])
    last = jnp.maximum(n_units - 1, 0)
    pad = w_ids >= n_units
    u_exp = jnp.where(pad, u_exp[last], u_exp).astype(I32)
    u_blk = jnp.where(pad, u_blk[last], u_blk).astype(I32)

    xs = _dispatch(d0, d1, t)
    y = _experts(u_blk, u_exp, gstart, n_units.reshape(1).astype(I32), xs,
                 w_exp_gate[l], w_exp_up[l], w_exp_down[l], bm)
    out = _combine(d0, d1, y, h, gate)
    return out.reshape(b, s, d)


def kernel(x, ln1_g, w_in, q_norm_g, k_norm_g, lambda_q1, lambda_k1, lambda_q2, lambda_k2, subln_g, shift_mu, w0, w_lora_up, a0, a_lora_up, g_lora_up, k_k, k_a, r_k, ln_x_g, ln_x_b, w_out, ln2_g, w_router_group, b_router_group, w_router_expert, b_router_expert, w_exp_gate, w_exp_up, w_exp_down):
    h = x
    for l in range(ln1_g.shape[0]):
        lambda_init = 0.8 - 0.6 * math.exp(-0.3 * l)
        h = _layer(h, l, lambda_init, ln1_g, w_in, q_norm_g, k_norm_g, lambda_q1, lambda_k1,
                   lambda_q2, lambda_k2, subln_g, shift_mu, w0, w_lora_up, a0, a_lora_up, g_lora_up,
                   k_k, k_a, r_k, ln_x_g, ln_x_b, w_out, ln2_g, w_router_group, b_router_group,
                   w_router_expert, b_router_expert, w_exp_gate, w_exp_up, w_exp_down)
    return h
```

```python
import functools
import math

import jax
import jax.numpy as jnp
from jax import lax
from jax.experimental import pallas as pl
from jax.experimental.pallas import tpu as pltpu

F32 = jnp.float32
BF16 = jnp.bfloat16
I32 = jnp.int32

CHUNK = 64
DH_A = 64
DV_A = 128
ROT_DIM = 16
ROPE_THETA = 500000.0
QK_NORM_EPS = 1e-6
SUBLN_EPS = 1e-5
RWKV_HEAD = 64
LN_X_EPS = 64e-5
RWKV_ROWS = 256
N_GROUPS = 4
EXPERTS_PER_GROUP = 8
N_EXPERTS = 32
NORM_EPS = 1e-6

LANES = 128
SUBLANES = 8
VMEM_LIMIT = 56 * 1024 * 1024

_NEG = -1e30


def _dot(a, b):
    return jnp.dot(a, b, preferred_element_type=F32)


def _dot_nt(a, b):
    return lax.dot_general(a, b, (((1,), (1,)), ((), ())), preferred_element_type=F32)


def _bdot(a, b):
    return _dot(a.astype(BF16), b.astype(BF16))


def _split(x):
    hi = x.astype(BF16)
    lo = (x - hi.astype(F32)).astype(BF16)
    return hi, lo


def _tile_rows_load(ref, rows):
    return jnp.concatenate([ref[pl.ds(j, rows, stride=SUBLANES), :] for j in range(SUBLANES)], axis=1)


def _tile_rows_store(ref, x):
    rows = x.shape[0]
    for j in range(SUBLANES):
        ref[pl.ds(j, rows, stride=SUBLANES), :] = x[:, j * LANES:(j + 1) * LANES]


def _in_proj_body(x_ref, g_ref, wa_ref, wb_ref, ua_ref, ub_ref):
    x = x_ref[...]
    ms = jnp.mean(x * x, axis=-1, keepdims=True)
    xn = (x * lax.rsqrt(ms + NORM_EPS) * g_ref[...]).astype(BF16)
    ua_ref[...] = _dot(xn, wa_ref[...])
    ub_ref[...] = _dot(xn, wb_ref[...])


def _in_proj(x2, g, wa, wb):
    n, d = x2.shape
    tm = min(512, n)
    na, nb = wa.shape[1], wb.shape[1]
    return pl.pallas_call(
        _in_proj_body,
        grid=(n // tm,),
        in_specs=[
            pl.BlockSpec((tm, d), lambda i: (i, 0)),
            pl.BlockSpec((1, d), lambda i: (0, 0)),
            pl.BlockSpec((d, na), lambda i: (0, 0)),
            pl.BlockSpec((d, nb), lambda i: (0, 0)),
        ],
        out_specs=[
            pl.BlockSpec((tm, na), lambda i: (i, 0)),
            pl.BlockSpec((tm, nb), lambda i: (i, 0)),
        ],
        out_shape=[jax.ShapeDtypeStruct((n, na), F32), jax.ShapeDtypeStruct((n, nb), F32)],
        compiler_params=pltpu.CompilerParams(
            dimension_semantics=("arbitrary",), vmem_limit_bytes=VMEM_LIMIT),
        name="in_proj",
    )(x2, g, wa, wb)


def _attn_body(q_ref, k_ref, v_ref, cos_ref, sa_ref, sb_ref, gq_ref, gk_ref, l_ref, sg_ref,
               o_ref, q0_s, q1_s, k_s, v_s, *, lambda_init, bq):
    s_len = q_ref.shape[0]
    lane = lax.broadcasted_iota(I32, (1, LANES), 1)
    lo = lane < DH_A

    def prep(t, g):
        t2 = t * t
        s_lo = jnp.sum(jnp.where(lo, t2, 0.0), axis=-1, keepdims=True)
        s_hi = jnp.sum(jnp.where(lo, 0.0, t2), axis=-1, keepdims=True)
        ms = jnp.where(lo, s_lo, s_hi) * (1.0 / DH_A)
        tn = t * lax.rsqrt(ms + QK_NORM_EPS) * g
        return (tn * cos_ref[...] + pltpu.roll(tn, LANES - ROT_DIM // 2, 1) * sa_ref[...]
                + pltpu.roll(tn, ROT_DIM // 2, 1) * sb_ref[...])

    q = prep(q_ref[...], gq_ref[...]) * (DH_A ** -0.5 * math.log2(math.e))
    q0_s[...] = jnp.where(lo, q, 0.0).astype(BF16)
    q1_s[...] = jnp.where(lo, 0.0, q).astype(BF16)
    k_s[...] = prep(k_ref[...], gk_ref[...]).astype(BF16)
    v_s[...] = v_ref[...].astype(BF16)

    l = l_ref[...]
    lam = (jnp.exp(jnp.sum(l[0:1] * l[1:2], axis=-1, keepdims=True))
           - jnp.exp(jnp.sum(l[2:3] * l[3:4], axis=-1, keepdims=True)) + lambda_init)

    chunk_shift = int(math.log2(CHUNK))
    rr = lax.broadcasted_iota(I32, (bq, bq), 0) >> chunk_shift
    cc = lax.broadcasted_iota(I32, (bq, bq), 1) >> chunk_shift
    diag_ok = cc <= rr

    diag_ok2 = jnp.concatenate([diag_ok, diag_ok], axis=0)

    for i in range(s_len // bq):
        r0 = i * bq
        qq = jnp.concatenate([q0_s[r0:r0 + bq, :], q1_s[r0:r0 + bq, :]], axis=0)
        m = l_run = acc = None
        for j in range(i + 1):
            c0 = j * bq
            s = _dot_nt(qq, k_s[c0:c0 + bq, :])
            if j == i:
                s = jnp.where(diag_ok2, s, _NEG)
            m_tile = jnp.max(s, axis=-1, keepdims=True)
            if j == 0:
                m = m_tile
                e = jnp.exp2(s - m)
                l_run = jnp.sum(e, axis=-1, keepdims=True)
                acc = _dot(e.astype(BF16), v_s[c0:c0 + bq, :])
            else:
                m_new = jnp.maximum(m, m_tile)
                alpha = jnp.exp2(m - m_new)
                e = jnp.exp2(s - m_new)
                l_run = l_run * alpha + jnp.sum(e, axis=-1, keepdims=True)
                acc = acc * alpha + _dot(e.astype(BF16), v_s[c0:c0 + bq, :])
                m = m_new
        w0 = 1.0 / l_run[:bq]
        w1 = lam / l_run[bq:]
        o = acc[:bq] * w0 - acc[bq:] * w1
        o = o * lax.rsqrt(jnp.mean(o * o, axis=-1, keepdims=True) + SUBLN_EPS)
        o_ref[r0:r0 + bq, :] = o * (sg_ref[...] * (1.0 - lambda_init))


def _diff_attn(ua3, cosf, sina, sinb, gq, gk, lvec, sg, lambda_init):
    b, s, _ = ua3.shape
    h_a = ua3.shape[2] // (3 * DV_A)
    bq = min(256, s)
    blk = lambda off: pl.BlockSpec((None, s, DV_A), lambda bi, hi: (bi, 0, off + hi))
    const = lambda shape: pl.BlockSpec(shape, lambda bi, hi: (0,) * len(shape))
    return pl.pallas_call(
        functools.partial(_attn_body, lambda_init=lambda_init, bq=bq),
        grid=(b, h_a),
        in_specs=[blk(0), blk(h_a), blk(2 * h_a),
                  const((s, LANES)), const((s, LANES)), const((s, LANES)),
                  const((1, LANES)), const((1, LANES)), const((4, DH_A)), const((1, LANES))],
        out_specs=pl.BlockSpec((None, s, DV_A), lambda bi, hi: (bi, 0, hi)),
        out_shape=jax.ShapeDtypeStruct((b, s, h_a * DV_A), F32),
        scratch_shapes=[pltpu.VMEM((s, LANES), BF16)] * 4,
        compiler_params=pltpu.CompilerParams(
            dimension_semantics=("arbitrary", "arbitrary"), vmem_limit_bytes=VMEM_LIMIT),
        name="diff_attn",
    )(ua3, ua3, ua3, cosf, sina, sinb, gq, gk, lvec, sg)


def _rwkv_body(u_ref, mu_ref, w0_ref, wlu_ref, a0_ref, alu_ref, glu_ref, kkw_ref, ka_ref, rk_ref,
               lg_ref, lb_ref, bd_ref, tri_ref, o_ref, prev_s, st_s, *, c_len):
    tr = u_ref.shape[0]
    d_r = o_ref.shape[1]
    n_pair = d_r // LANES
    n_ch = tr // c_len

    @pl.when(pl.program_id(1) == 0)
    def _():
        prev_s[...] = jnp.zeros_like(prev_s)
        st_s[...] = jnp.zeros_like(st_s)

    u = u_ref[...]
    row = lax.broadcasted_iota(I32, (tr, 1), 0)
    u_prev = jnp.where(row == 0, prev_s[...], pltpu.roll(u, 1, 0))
    prev_s[...] = u[tr - 1:tr, :]
    us = u + (u_prev - u) * mu_ref[...]

    r = us[:, 0:d_r]
    k = us[:, d_r:2 * d_r]
    v = us[:, 2 * d_r:3 * d_r]
    lo_in = us[:, 3 * d_r:3 * d_r + LANES]
    g_in = us[:, 3 * d_r + LANES:3 * d_r + 2 * LANES]

    bd = bd_ref[...]

    def head_sum(x, two_pass=False):
        hi, lo = _split(x)
        return _dot(hi, bd) + _dot(lo, bd) if two_pass else _dot(hi, bd)

    z = -(w0_ref[...] + _bdot(jnp.tanh(lo_in), wlu_ref[...]))
    softplus = jnp.maximum(z, 0.0) + jnp.log(1.0 + jnp.exp(-jnp.abs(z)))
    logw = -jnp.exp(-softplus - 0.5)
    a = 1.0 / (1.0 + jnp.exp(-(a0_ref[...] + _bdot(lo_in, alu_ref[...]))))
    g = _bdot(1.0 / (1.0 + jnp.exp(-g_in)), glu_ref[...])
    kk_raw = k * kkw_ref[...]
    kk = kk_raw / jnp.maximum(jnp.sqrt(head_sum(kk_raw * kk_raw, two_pass=True)), 1e-12)
    k_mod = k * (1.0 + (a - 1.0) * ka_ref[...])

    tri = tri_ref[...]
    w_hi, w_lo = _split(logw)
    cum = _dot(tri, w_hi) + _dot(tri, w_lo)
    last_rows = [cum[(c + 1) * c_len - 1:(c + 1) * c_len, :] for c in range(n_ch)]
    cum_last = jnp.concatenate([jnp.broadcast_to(lr, (c_len, d_r)) for lr in last_rows], axis=0)
    e_neg = jnp.exp(-cum)
    e_rem = jnp.exp(cum_last - cum)
    a_t = -kk * jnp.exp(cum - logw)
    r_t = r * jnp.exp(cum)
    b_raw = kk * a
    b_t = b_raw * e_neg
    k_t = k_mod * e_neg
    b_2 = b_raw * e_rem
    k_2 = k_mod * e_rem

    lane = lax.broadcasted_iota(I32, (1, LANES), 1)
    head0 = lane < RWKV_HEAD
    two_c = 2 * c_len
    rr = lax.broadcasted_iota(I32, (two_c, two_c), 0)
    cc = lax.broadcasted_iota(I32, (two_c, two_c), 1)
    t_i = rr & (c_len - 1)
    s_i = cc & (c_len - 1)
    strict = t_i > s_i
    incl = t_i >= s_i
    eye = rr == cc

    def stack(x, c, j):
        xp = x[c * c_len:(c + 1) * c_len, j * LANES:(j + 1) * LANES]
        return jnp.concatenate([jnp.where(head0, xp, 0.0), jnp.where(head0, 0.0, xp)], axis=0)

    units = [(c, j) for c in range(n_ch) for j in range(n_pair)]
    n_double = int(math.log2(c_len))

    a_s = [stack(a_t, c, j) for c, j in units]
    r_s = [stack(r_t, c, j) for c, j in units]
    v_b = [stack(v, c, j).astype(BF16) for c, j in units]
    gram = [_dot_nt(jnp.concatenate([a_s[i], r_s[i]], axis=0).astype(BF16),
                    jnp.concatenate([stack(b_t, c, j), stack(k_t, c, j)], axis=0).astype(BF16))
            for i, (c, j) in enumerate(units)]
    p = [jnp.where(strict, gm[:two_c, :two_c], 0.0) for gm in gram]
    m_rb = [jnp.where(incl, gm[two_c:, :two_c], 0.0).astype(BF16) for gm in gram]
    m_rk = [jnp.where(incl, gm[two_c:, two_c:], 0.0).astype(BF16) for gm in gram]
    vv = [_dot(jnp.where(strict, gm[:two_c, two_c:], 0.0).astype(BF16), vb)
          for gm, vb in zip(gram, v_b)]
    assert c_len == RWKV_HEAD and 2 * c_len == LANES
    half = LANES // 2
    lo_half = lax.broadcasted_iota(I32, (1, LANES), 1) < half
    keep = lax.broadcasted_iota(I32, (1, 2 * LANES), 1) >= half
    zeros_b = jnp.zeros((c_len, 2 * LANES), BF16)
    z = []
    for i in range(len(units)):
        for hp in range(2):
            rs = slice(hp * c_len, (hp + 1) * c_len)
            p_r, a_r, v_r = p[i][rs, :], a_s[i][rs, :], vv[i][rs, :]
            if hp == 0:
                z.append(jnp.concatenate([p_r + pltpu.roll(a_r, half, 1), v_r], axis=1))
            else:
                z.append(jnp.concatenate([pltpu.roll(p_r, half, 1) + a_r, pltpu.roll(v_r, half, 1)], axis=1))
    for it in range(n_double):
        lhs = [jnp.where(lo_half, zi[:, :LANES], 0.0).astype(BF16) for zi in z]
        rhs = [jnp.concatenate([zi.astype(BF16), zeros_b], axis=0) for zi in z]
        z = [_dot(li, ri) + jnp.where(keep, zi, 0.0) for li, ri, zi in zip(lhs, rhs, z)]
    x_b = []
    for i in range(len(units)):
        z0, z1 = z[2 * i], z[2 * i + 1]
        rows0 = jnp.concatenate([pltpu.roll(jnp.where(lo_half, 0.0, z0[:, :LANES]), half, 1),
                                 z0[:, LANES:]], axis=1)
        rows1 = jnp.concatenate([jnp.where(lo_half, 0.0, z1[:, :LANES]),
                                 pltpu.roll(z1[:, LANES:], half, 1)], axis=1)
        x_b.append(jnp.concatenate([rows0, rows1], axis=0).astype(BF16))
    zeros_v = jnp.zeros((two_c, two_c), BF16)
    rhs2 = [jnp.concatenate([x_b[i], jnp.concatenate([zeros_v, v_b[i]], axis=1)], axis=0)
            for i in range(len(units))]
    mx = [_dot(jnp.concatenate([m_rb[i], m_rk[i]], axis=1), rhs2[i]) for i in range(len(units))]
    bx = [_dot(jnp.concatenate([stack(b_2, c, j).T, stack(k_2, c, j).T], axis=1).astype(BF16), rhs2[i])
          for i, (c, j) in enumerate(units)]

    y_rows = []
    st = [st_s[j] for j in range(n_pair)]
    for c in range(n_ch):
        p_c = jnp.exp(last_rows[c])
        y_pairs = []
        for j in range(n_pair):
            i = c * n_pair + j
            r_h = (r_s[i] + mx[i][:, :two_c]).astype(BF16)
            y_h = mx[i][:, two_c:]
            g_m = (jnp.where(eye, p_c[:, j * LANES:(j + 1) * LANES], 0.0) + bx[i][:, :two_c]).astype(BF16)
            h_m = bx[i][:, two_c:]
            st_b = st[j].astype(BF16)
            y_s = _dot(r_h, st_b) + y_h
            st[j] = _dot(g_m, st_b) + h_m
            y_pairs.append(y_s[:c_len] + y_s[c_len:])
        y_rows.append(jnp.concatenate(y_pairs, axis=1))
    for j in range(n_pair):
        st_s[j] = st[j]
    y = jnp.concatenate(y_rows, axis=0)

    inv_n = 1.0 / RWKV_HEAD
    mu = head_sum(y) * inv_n
    dlt = y - mu
    var = head_sum(dlt * dlt) * inv_n
    yn = dlt * lax.rsqrt(var + LN_X_EPS) * lg_ref[...] + lb_ref[...]
    bonus = head_sum(r * k_mod * rk_ref[...]) * v
    o_ref[...] = (yn + bonus) * g


def _rwkv7(ub3, mu, w0, wlu, a0, alu, glu, kkw, ka, rk, lg, lb, bd, tri, c_len):
    b, s, cols = ub3.shape
    d_r = w0.shape[1]
    tr = tri.shape[0]
    const = lambda a: pl.BlockSpec(a.shape, lambda bi, ci: (0,) * a.ndim)
    params = (mu, w0, wlu, a0, alu, glu, kkw, ka, rk, lg, lb, bd, tri)
    return pl.pallas_call(
        functools.partial(_rwkv_body, c_len=c_len),
        grid=(b, s // tr),
        in_specs=[pl.BlockSpec((None, tr, cols), lambda bi, ci: (bi, ci, 0))]
                 + [const(a) for a in params],
        out_specs=pl.BlockSpec((None, tr, d_r), lambda bi, ci: (bi, ci, 0)),
        out_shape=jax.ShapeDtypeStruct((b, s, d_r), F32),
        scratch_shapes=[pltpu.VMEM((1, cols), F32),
                        pltpu.VMEM((d_r // LANES, LANES, LANES), F32)],
        compiler_params=pltpu.CompilerParams(
            dimension_semantics=("arbitrary", "arbitrary"), vmem_limit_bytes=VMEM_LIMIT),
        name="rwkv7",
    )(ub3, *params)


def _out_body(x_ref, oa_ref, ob_ref, wa_ref, wb_ref, g2_ref, wr_ref, br_ref, tril_ref,
              h_ref, t_ref, eid_ref, gate_ref, rank_ref, cnt_ref, run_s):
    tm = x_ref.shape[0]

    @pl.when(pl.program_id(0) == 0)
    def _():
        run_s[...] = jnp.zeros_like(run_s)

    h = x_ref[...] + _bdot(oa_ref[...], wa_ref[...]) + _bdot(ob_ref[...], wb_ref[...])
    h_ref[...] = h
    t = h * lax.rsqrt(jnp.mean(h * h, axis=-1, keepdims=True) + NORM_EPS) * g2_ref[...]
    _tile_rows_store(t_ref, t)

    t_hi, t_lo = _split(t)
    wr = wr_ref[...]
    hh = _dot(t_hi, wr)
    lg = hh[:, :LANES] + _dot(t_lo, wr[:, :LANES]) + hh[:, LANES:] + br_ref[...]

    lane_i = lax.broadcasted_iota(I32, (1, LANES), 1)
    lane = lane_i.astype(F32)
    big = float(LANES)

    def first_index(mask):
        return jnp.min(jnp.where(mask, lane, big), axis=-1, keepdims=True)

    def masked_softmax(mask):
        m = jnp.max(jnp.where(mask, lg, _NEG), axis=-1, keepdims=True)
        e = jnp.where(mask, jnp.exp(lg - m), 0.0)
        return e / jnp.sum(e, axis=-1, keepdims=True)

    is_g = lane_i < N_GROUPS
    gprob = masked_softmax(is_g)
    p_group = jnp.max(gprob, axis=-1, keepdims=True)
    gsel = first_index(is_g & (gprob == p_group))
    base = N_GROUPS + gsel * EXPERTS_PER_GROUP
    in_e = (lane >= base) & (lane < base + EXPERTS_PER_GROUP)
    eprob = masked_softmax(in_e)
    p1 = jnp.max(eprob, axis=-1, keepdims=True)
    i1 = first_index(in_e & (eprob == p1))
    rest = in_e & (lane != i1)
    p2 = jnp.max(jnp.where(rest, eprob, -1.0), axis=-1, keepdims=True)
    i2 = first_index(rest & (eprob == p2))
    den = p1 + p2
    g1 = p_group * p1 / den
    g2 = p_group * p2 / den
    e1 = i1 - N_GROUPS
    e2 = i2 - N_GROUPS

    oh1 = (lane == e1).astype(F32)
    oh2 = (lane == e2).astype(F32)
    before = _dot(tril_ref[...], jnp.concatenate([oh1, oh2], axis=1).astype(BF16))
    before1 = before[:, :LANES]
    tot1 = jnp.sum(oh1, axis=0, keepdims=True)
    before2 = before[:, LANES:] + tot1
    run = run_s[...]
    rk1 = jnp.sum(oh1 * (before1 + run), axis=-1, keepdims=True)
    rk2 = jnp.sum(oh2 * (before2 + run), axis=-1, keepdims=True)
    run = run + tot1 + jnp.sum(oh2, axis=0, keepdims=True)
    run_s[...] = run
    cnt_ref[...] = run

    sel0 = lane_i == 0
    sel1 = lane_i == 1
    eid_ref[...] = jnp.where(sel0, e1, jnp.where(sel1, e2, 0.0)).astype(I32)
    rank_ref[...] = jnp.where(sel0, rk1, jnp.where(sel1, rk2, 0.0)).astype(I32)
    gate_ref[...] = jnp.where(sel0, g1, jnp.where(sel1, g2, 0.0))


def _out_proj_route(x2, oa, ob, wa, wb, g2, wr, br, tril):
    n, d = x2.shape
    tm = tril.shape[0]
    da, db = oa.shape[1], ob.shape[1]
    row = lambda w: pl.BlockSpec((tm, w), lambda i: (i, 0))
    const = lambda a: pl.BlockSpec(a.shape, lambda i: (0,) * a.ndim)
    return pl.pallas_call(
        _out_body,
        grid=(n // tm,),
        in_specs=[row(d), row(da), row(db), const(wa), const(wb), const(g2), const(wr),
                  const(br), const(tril)],
        out_specs=[row(d), pl.BlockSpec((tm * SUBLANES, LANES), lambda i: (i, 0)),
                   row(LANES), row(LANES), row(LANES), pl.BlockSpec((1, LANES), lambda i: (0, 0))],
        out_shape=[jax.ShapeDtypeStruct((n, d), F32), jax.ShapeDtypeStruct((n * SUBLANES, LANES), F32),
                   jax.ShapeDtypeStruct((n, LANES), I32), jax.ShapeDtypeStruct((n, LANES), F32),
                   jax.ShapeDtypeStruct((n, LANES), I32), jax.ShapeDtypeStruct((1, LANES), F32)],
        scratch_shapes=[pltpu.VMEM((1, LANES), F32)],
        compiler_params=pltpu.CompilerParams(
            dimension_semantics=("arbitrary",), vmem_limit_bytes=VMEM_LIMIT),
        name="out_proj_route",
    )(x2, oa, ob, wa, wb, g2, wr, br, tril)


def _dispatch_body(d0_ref, d1_ref, t_ref, xs_ref, tbuf, sem_in, sem_out):
    tm = d0_ref.shape[0]
    rows = tm * SUBLANES
    tile = pl.program_id(0)
    n_tiles = pl.num_programs(0)
    slot = tile % 2

    def load(idx, s):
        src = pl.multiple_of(idx * rows, SUBLANES)
        return pltpu.make_async_copy(t_ref.at[pl.ds(src, rows)], tbuf.at[s], sem_in.at[s])

    def wait_rows(s):
        for _ in range(2):
            pltpu.make_async_copy(tbuf.at[s], xs_ref.at[pl.ds(0, rows)], sem_out.at[s]).wait()

    @pl.when(tile == 0)
    def _():
        load(0, 0).start()

    @pl.when(tile >= 1)
    def _():
        wait_rows(1 - slot)

    @pl.when(tile + 1 < n_tiles)
    def _():
        load(tile + 1, 1 - slot).start()

    load(tile, slot).wait()

    def copy(i, d_ref):
        dst = pl.multiple_of(d_ref[i] * SUBLANES, SUBLANES)
        return pltpu.make_async_copy(tbuf.at[slot, pl.ds(pl.multiple_of(i * SUBLANES, SUBLANES), SUBLANES)],
                                     xs_ref.at[pl.ds(dst, SUBLANES)], sem_out.at[slot])

    def start(i, carry):
        copy(i, d0_ref).start(priority=0)
        copy(i, d1_ref).start(priority=1)
        return carry

    lax.fori_loop(0, tm, start, 0, unroll=8)

    @pl.when(tile == n_tiles - 1)
    def _():
        wait_rows(slot)


def _dispatch(d0, d1, t):
    n = d0.shape[0]
    tm = min(512, n)
    smem = pl.BlockSpec((tm,), lambda i: (i,), memory_space=pltpu.SMEM)
    return pl.pallas_call(
        _dispatch_body,
        grid=(n // tm,),
        in_specs=[smem, smem, pl.BlockSpec(memory_space=pl.ANY)],
        out_specs=pl.BlockSpec(memory_space=pl.ANY),
        out_shape=jax.ShapeDtypeStruct((2 * n * SUBLANES, LANES), t.dtype),
        scratch_shapes=[pltpu.VMEM((2, tm * SUBLANES, LANES), F32), pltpu.SemaphoreType.DMA((2,)),
                        pltpu.SemaphoreType.DMA((2,))],
        compiler_params=pltpu.CompilerParams(dimension_semantics=("arbitrary",)),
        name="moe_dispatch",
    )(d0, d1, t)


def _experts_body(ub_ref, ue_ref, gs_ref, nu_ref, x_ref, wg_ref, wu_ref, wd_ref, y_ref,
                  wg_s, wu_s, wd_s):
    w = pl.program_id(0)
    bm = x_ref.shape[0] // SUBLANES

    @pl.when(w < nu_ref[0])
    def _():
        e = ue_ref[w]
        blk = ub_ref[w]
        prev = jnp.maximum(w - 1, 0)

        @pl.when(jnp.logical_or(w == 0, ue_ref[prev] != e))
        def _():
            wg_s[...] = wg_ref[...].astype(BF16)
            wu_s[...] = wu_ref[...].astype(BF16)
            wd_s[...] = wd_ref[...].astype(BF16)

        rows = blk * bm + lax.broadcasted_iota(I32, (bm, 1), 0)
        valid = (rows >= gs_ref[e]) & (rows < gs_ref[e + 1])
        xb = _tile_rows_load(x_ref, bm).astype(BF16)
        gt = _dot(xb, wg_s[...])
        up = _dot(xb, wu_s[...])
        act = (gt / (1.0 + jnp.exp(-gt))) * up
        y = jnp.where(valid, _dot(act.astype(BF16), wd_s[...]), 0.0)
        first = jnp.logical_or(w == 0, ub_ref[prev] != blk)

        @pl.when(first)
        def _():
            _tile_rows_store(y_ref, y)

        @pl.when(jnp.logical_not(first))
        def _():
            _tile_rows_store(y_ref, _tile_rows_load(y_ref, bm) + y)


def _experts(unit_blk, unit_exp, gstart, n_units, xs, wg, wu, wd, bm):
    m = xs.shape[0] // SUBLANES
    d, de = wg.shape[1], wg.shape[2]
    max_units = unit_blk.shape[0]
    grid_spec = pltpu.PrefetchScalarGridSpec(
        num_scalar_prefetch=4,
        grid=(max_units,),
        in_specs=[
            pl.BlockSpec((bm * SUBLANES, LANES), lambda w, ub, ue, gs, nu: (ub[w], 0)),
            pl.BlockSpec((None, d, de), lambda w, ub, ue, gs, nu: (ue[w], 0, 0)),
            pl.BlockSpec((None, d, de), lambda w, ub, ue, gs, nu: (ue[w], 0, 0)),
            pl.BlockSpec((None, de, d), lambda w, ub, ue, gs, nu: (ue[w], 0, 0)),
        ],
        out_specs=pl.BlockSpec((bm * SUBLANES, LANES), lambda w, ub, ue, gs, nu: (ub[w], 0)),
        scratch_shapes=[pltpu.VMEM((d, de), BF16), pltpu.VMEM((d, de), BF16),
                        pltpu.VMEM((de, d), BF16)],
    )
    return pl.pallas_call(
        _experts_body,
        grid_spec=grid_spec,
        out_shape=jax.ShapeDtypeStruct((m * SUBLANES, LANES), F32),
        compiler_params=pltpu.CompilerParams(
            dimension_semantics=("arbitrary",), vmem_limit_bytes=VMEM_LIMIT),
        name="moe_experts",
    )(unit_blk, unit_exp, gstart, n_units, xs, wg, wu, wd)


def _combine_body(d0_ref, d1_ref, d0n_ref, d1n_ref, y_ref, h_ref, gate_ref, o_ref, buf, sem):
    tm = d0_ref.shape[0]
    tile = pl.program_id(0)
    n_tiles = pl.num_programs(0)

    def gather(da_ref, db_ref, slot):
        def copy(i, d_ref, k):
            src = pl.multiple_of(d_ref[i] * SUBLANES, SUBLANES)
            dst = pl.multiple_of(i * SUBLANES, SUBLANES)
            return pltpu.make_async_copy(y_ref.at[pl.ds(src, SUBLANES)],
                                         buf.at[slot, k, pl.ds(dst, SUBLANES)], sem.at[slot])

        def start(i, carry):
            copy(i, da_ref, 0).start(priority=0)
            copy(i, db_ref, 1).start(priority=1)
            return carry

        lax.fori_loop(0, tm, start, 0, unroll=8)

    slot = tile % 2

    @pl.when(tile == 0)
    def _():
        gather(d0_ref, d1_ref, 0)

    @pl.when(tile + 1 < n_tiles)
    def _():
        gather(d0n_ref, d1n_ref, 1 - slot)

    for k in range(2):
        pltpu.make_async_copy(y_ref.at[pl.ds(0, tm * SUBLANES)], buf.at[slot, k], sem.at[slot]).wait()
    gate = gate_ref[...]
    o_ref[...] = (h_ref[...] + _tile_rows_load(buf.at[slot, 0], tm) * gate[:, 0:1]
                  + _tile_rows_load(buf.at[slot, 1], tm) * gate[:, 1:2])


def _combine(d0, d1, y, h, gate):
    n, d = h.shape
    tm = min(512, n)
    last = n // tm - 1
    smem = pl.BlockSpec((tm,), lambda i: (i,), memory_space=pltpu.SMEM)
    smem_next = pl.BlockSpec((tm,), lambda i: (jnp.minimum(i + 1, last),), memory_space=pltpu.SMEM)
    return pl.pallas_call(
        _combine_body,
        grid=(n // tm,),
        in_specs=[smem, smem, smem_next, smem_next, pl.BlockSpec(memory_space=pl.ANY),
                  pl.BlockSpec((tm, d), lambda i: (i, 0)),
                  pl.BlockSpec((tm, LANES), lambda i: (i, 0))],
        out_specs=pl.BlockSpec((tm, d), lambda i: (i, 0)),
        out_shape=jax.ShapeDtypeStruct((n, d), F32),
        scratch_shapes=[pltpu.VMEM((2, 2, tm * SUBLANES, LANES), F32), pltpu.SemaphoreType.DMA((2,))],
        compiler_params=pltpu.CompilerParams(
            dimension_semantics=("arbitrary",), vmem_limit_bytes=VMEM_LIMIT),
        name="moe_combine",
    )(d0, d1, d0, d1, y, h, gate)


def _rope_tables(seq):
    half = ROT_DIM // 2
    inv = ROPE_THETA ** (-jnp.arange(0, ROT_DIM, 2, dtype=F32) / ROT_DIM)
    ang = jnp.arange(seq, dtype=F32)[:, None] * inv[None, :]
    cos, sin = jnp.cos(ang), jnp.sin(ang)
    ones = jnp.ones((seq, DH_A - ROT_DIM), F32)
    zeros = jnp.zeros((seq, DH_A - half), F32)
    zeros_h = jnp.zeros((seq, half), F32)
    zeros_r = jnp.zeros((seq, DH_A - ROT_DIM), F32)
    cos_c = jnp.concatenate([cos, cos, ones], axis=1)
    sa_c = jnp.concatenate([-sin, zeros], axis=1)
    sb_c = jnp.concatenate([zeros_h, sin, zeros_r], axis=1)
    tile = lambda t: jnp.concatenate([t, t], axis=1)
    return tile(cos_c), tile(sa_c), tile(sb_c)


def _layer(x, l, lambda_init, ln1_g, w_in, q_norm_g, k_norm_g, lambda_q1, lambda_k1, lambda_q2,
           lambda_k2, subln_g, shift_mu, w0, w_lora_up, a0, a_lora_up, g_lora_up, k_k, k_a, r_k,
           ln_x_g, ln_x_b, w_out, ln2_g, w_router_group, b_router_group, w_router_expert,
           b_router_expert, w_exp_gate, w_exp_up, w_exp_down):
    b, s, d = x.shape
    n = b * s
    d_attn3 = 3 * (d // 2)
    d_attn = d // 2
    d_r = w0.shape[1]
    x2 = x.reshape(n, d)
    row = lambda a: a[l].reshape(1, -1)

    w_in_b = w_in[l].astype(BF16)
    ua, ub = _in_proj(x2, row(ln1_g), w_in_b[:, :d_attn3], w_in_b[:, d_attn3:])

    cosf, sina, sinb = _rope_tables(s)
    tile2 = lambda a: jnp.concatenate([a[l], a[l]]).reshape(1, -1)
    lvec = jnp.stack([lambda_q1[l], lambda_k1[l], lambda_q2[l], lambda_k2[l]])
    o_a = _diff_attn(ua.reshape(b, s, d_attn3), cosf, sina, sinb, tile2(q_norm_g), tile2(k_norm_g),
                     lvec, row(subln_g), lambda_init)

    dl = w_lora_up.shape[1]
    zeros_l = jnp.zeros((dl, d_r), F32)
    wlu = jnp.concatenate([w_lora_up[l], zeros_l], axis=0).astype(BF16)
    alu = jnp.concatenate([zeros_l, a_lora_up[l]], axis=0).astype(BF16)
    hid = jnp.arange(d_r) // RWKV_HEAD
    bd = (hid[:, None] == hid[None, :]).astype(BF16)
    tr = min(RWKV_ROWS, s)
    ti = jnp.arange(tr)
    tri = ((ti[:, None] >= ti[None, :]) & (ti[:, None] // CHUNK == ti[None, :] // CHUNK)).astype(BF16)
    o_b = _rwkv7(ub.reshape(b, s, -1), row(shift_mu), row(w0), wlu, row(a0), alu,
                 g_lora_up[l].astype(BF16), row(k_k), row(k_a), row(r_k), row(ln_x_g), row(ln_x_b),
                 bd, tri, CHUNK)

    w_out_b = w_out[l].astype(BF16)
    wr = jnp.concatenate([w_router_group[l], w_router_expert[l],
                          jnp.zeros((d, LANES - N_GROUPS - N_EXPERTS), F32)], axis=1)
    wr_hi = wr.astype(BF16)
    wr_cat = jnp.concatenate([wr_hi, (wr - wr_hi.astype(F32)).astype(BF16)], axis=1)
    br = jnp.concatenate([b_router_group[l], b_router_expert[l],
                          jnp.zeros((LANES - N_GROUPS - N_EXPERTS,), F32)]).reshape(1, LANES)
    tm_r = min(512, n)
    tril = (jnp.arange(tm_r)[:, None] > jnp.arange(tm_r)[None, :]).astype(BF16)
    h, t, eid, gate, rank, cnt = _out_proj_route(
        x2, o_a.reshape(n, d_attn), o_b.reshape(n, d_r), w_out_b[:d_attn], w_out_b[d_attn:],
        row(ln2_g), wr_cat, br, tril)

    counts = cnt[0, :N_EXPERTS].astype(I32)
    gend = jnp.cumsum(counts)
    gstart = jnp.concatenate([jnp.zeros((1,), I32), gend]).astype(I32)
    onehot = (eid[:, :2, None] == jnp.arange(N_EXPERTS)[None, None, :])
    dest = jnp.sum(jnp.where(onehot, gstart[:N_EXPERTS], 0), axis=-1) + rank[:, :2]
    d0 = dest[:, 0].astype(I32)
    d1 = dest[:, 1].astype(I32)

    m = 2 * n
    bm = min(512, m)
    n_blk = m // bm
    max_units = n_blk + N_EXPERTS - 1
    first_blk = gstart[:N_EXPERTS] // bm
    last_blk = jnp.where(counts > 0, (gend - 1) // bm, first_blk)
    units_e = jnp.where(counts > 0, last_blk - first_blk + 1, 0)
    unit_end = jnp.cumsum(units_e)
    unit_start = unit_end - units_e
    n_units = unit_end[-1]
    w_ids = jnp.arange(max_units)
    u_exp = jnp.minimum(jnp.sum((unit_end[None, :] <= w_ids[:, None]).astype(I32), axis=1),
                        N_EXPERTS - 1)
    u_blk = first_blk[u_exp] + (w_ids - unit_start[u_exp])
    last = jnp.maximum(n_units - 1, 0)
    pad = w_ids >= n_units
    u_exp = jnp.where(pad, u_exp[last], u_exp).astype(I32)
    u_blk = jnp.where(pad, u_blk[last], u_blk).astype(I32)

    xs = _dispatch(d0, d1, t)
    y = _experts(u_blk, u_exp, gstart, n_units.reshape(1).astype(I32), xs,
                 w_exp_gate[l], w_exp_up[l], w_exp_down[l], bm)
    out = _combine(d0, d1, y, h, gate)
    return out.reshape(b, s, d)


def kernel(x, ln1_g, w_in, q_norm_g, k_norm_g, lambda_q1, lambda_k1, lambda_q2, lambda_k2, subln_g, shift_mu, w0, w_lora_up, a0, a_lora_up, g_lora_up, k_k, k_a, r_k, ln_x_g, ln_x_b, w_out, ln2_g, w_router_group, b_router_group, w_router_expert, b_router_expert, w_exp_gate, w_exp_up, w_exp_down):
    h = x
    for l in range(ln1_g.shape[0]):
        lambda_init = 0.8 - 0.6 * math.exp(-0.3 * l)
        h = _layer(h, l, lambda_init, ln1_g, w_in, q_norm_g, k_norm_g, lambda_q1, lambda_k1,
                   lambda_q2, lambda_k2, subln_g, shift_mu, w0, w_lora_up, a0, a_lora_up, g_lora_up,
                   k_k, k_a, r_k, ln_x_g, ln_x_b, w_out, ln2_g, w_router_group, b_router_group,
                   w_router_expert, b_router_expert, w_exp_gate, w_exp_up, w_exp_down)
    return h
```

```python
import functools
import math

import jax
import jax.numpy as jnp
from jax import lax
from jax.experimental import pallas as pl
from jax.experimental.pallas import tpu as pltpu

F32 = jnp.float32
BF16 = jnp.bfloat16
I32 = jnp.int32

CHUNK = 64
DH_A = 64
DV_A = 128
ROT_DIM = 16
ROPE_THETA = 500000.0
QK_NORM_EPS = 1e-6
SUBLN_EPS = 1e-5
RWKV_HEAD = 64
LN_X_EPS = 64e-5
RWKV_ROWS = 256
N_GROUPS = 4
EXPERTS_PER_GROUP = 8
N_EXPERTS = 32
EXPERT_ROWS = 512
NORM_EPS = 1e-6

LANES = 128
SUBLANES = 8
VMEM_LIMIT = 56 * 1024 * 1024

_NEG = -1e30


def _dot(a, b):
    return jnp.dot(a, b, preferred_element_type=F32)


def _dot_nt(a, b):
    return lax.dot_general(a, b, (((1,), (1,)), ((), ())), preferred_element_type=F32)


def _bdot(a, b):
    return _dot(a.astype(BF16), b.astype(BF16))


def _split(x):
    hi = x.astype(BF16)
    lo = (x - hi.astype(F32)).astype(BF16)
    return hi, lo


def _tile_rows_load(ref, rows):
    return jnp.concatenate([ref[pl.ds(j, rows, stride=SUBLANES), :] for j in range(SUBLANES)], axis=1)


def _tile_rows_store(ref, x):
    rows = x.shape[0]
    for j in range(SUBLANES):
        ref[pl.ds(j, rows, stride=SUBLANES), :] = x[:, j * LANES:(j + 1) * LANES]


def _in_proj_body(x_ref, g_ref, wa_ref, wb_ref, ua_ref, ub_ref):
    x = x_ref[...]
    ms = jnp.mean(x * x, axis=-1, keepdims=True)
    xn = (x * lax.rsqrt(ms + NORM_EPS) * g_ref[...]).astype(BF16)
    ua_ref[...] = _dot(xn, wa_ref[...])
    ub_ref[...] = _dot(xn, wb_ref[...])


def _in_proj(x2, g, wa, wb):
    n, d = x2.shape
    tm = min(512, n)
    na, nb = wa.shape[1], wb.shape[1]
    return pl.pallas_call(
        _in_proj_body,
        grid=(n // tm,),
        in_specs=[
            pl.BlockSpec((tm, d), lambda i: (i, 0)),
            pl.BlockSpec((1, d), lambda i: (0, 0)),
            pl.BlockSpec((d, na), lambda i: (0, 0)),
            pl.BlockSpec((d, nb), lambda i: (0, 0)),
        ],
        out_specs=[
            pl.BlockSpec((tm, na), lambda i: (i, 0)),
            pl.BlockSpec((tm, nb), lambda i: (i, 0)),
        ],
        out_shape=[jax.ShapeDtypeStruct((n, na), F32), jax.ShapeDtypeStruct((n, nb), F32)],
        compiler_params=pltpu.CompilerParams(
            dimension_semantics=("arbitrary",), vmem_limit_bytes=VMEM_LIMIT),
        name="in_proj",
    )(x2, g, wa, wb)


def _attn_body(q_ref, k_ref, v_ref, cos_ref, sa_ref, sb_ref, gq_ref, gk_ref, l_ref, sg_ref,
               o_ref, q0_s, q1_s, k_s, v_s, *, lambda_init, bq):
    s_len = q_ref.shape[0]
    lane = lax.broadcasted_iota(I32, (1, LANES), 1)
    lo = lane < DH_A

    def prep(t, g):
        t2 = t * t
        s_lo = jnp.sum(jnp.where(lo, t2, 0.0), axis=-1, keepdims=True)
        s_hi = jnp.sum(jnp.where(lo, 0.0, t2), axis=-1, keepdims=True)
        ms = jnp.where(lo, s_lo, s_hi) * (1.0 / DH_A)
        tn = t * lax.rsqrt(ms + QK_NORM_EPS) * g
        return (tn * cos_ref[...] + pltpu.roll(tn, LANES - ROT_DIM // 2, 1) * sa_ref[...]
                + pltpu.roll(tn, ROT_DIM // 2, 1) * sb_ref[...])

    q = prep(q_ref[...], gq_ref[...]) * (DH_A ** -0.5 * math.log2(math.e))
    q0_s[...] = jnp.where(lo, q, 0.0).astype(BF16)
    q1_s[...] = jnp.where(lo, 0.0, q).astype(BF16)
    k_s[...] = prep(k_ref[...], gk_ref[...]).astype(BF16)
    v_s[...] = v_ref[...].astype(BF16)

    l = l_ref[...]
    lam = (jnp.exp(jnp.sum(l[0:1] * l[1:2], axis=-1, keepdims=True))
           - jnp.exp(jnp.sum(l[2:3] * l[3:4], axis=-1, keepdims=True)) + lambda_init)

    chunk_shift = int(math.log2(CHUNK))
    rr = lax.broadcasted_iota(I32, (bq, bq), 0) >> chunk_shift
    cc = lax.broadcasted_iota(I32, (bq, bq), 1) >> chunk_shift
    diag_ok = cc <= rr

    diag_ok2 = jnp.concatenate([diag_ok, diag_ok], axis=0)

    for i in range(s_len // bq):
        r0 = i * bq
        qq = jnp.concatenate([q0_s[r0:r0 + bq, :], q1_s[r0:r0 + bq, :]], axis=0)
        m = l_run = acc = None
        for j in range(i + 1):
            c0 = j * bq
            s = _dot_nt(qq, k_s[c0:c0 + bq, :])
            if j == i:
                s = jnp.where(diag_ok2, s, _NEG)
            m_tile = jnp.max(s, axis=-1, keepdims=True)
            if j == 0:
                m = m_tile
                e = jnp.exp2(s - m)
                l_run = jnp.sum(e, axis=-1, keepdims=True)
                acc = _dot(e.astype(BF16), v_s[c0:c0 + bq, :])
            else:
                m_new = jnp.maximum(m, m_tile)
                alpha = jnp.exp2(m - m_new)
                e = jnp.exp2(s - m_new)
                l_run = l_run * alpha + jnp.sum(e, axis=-1, keepdims=True)
                acc = acc * alpha + _dot(e.astype(BF16), v_s[c0:c0 + bq, :])
                m = m_new
        w0 = 1.0 / l_run[:bq]
        w1 = lam / l_run[bq:]
        o = acc[:bq] * w0 - acc[bq:] * w1
        o = o * lax.rsqrt(jnp.mean(o * o, axis=-1, keepdims=True) + SUBLN_EPS)
        o_ref[r0:r0 + bq, :] = o * (sg_ref[...] * (1.0 - lambda_init))


def _diff_attn(ua3, cosf, sina, sinb, gq, gk, lvec, sg, lambda_init):
    b, s, _ = ua3.shape
    h_a = ua3.shape[2] // (3 * DV_A)
    bq = min(256, s)
    blk = lambda off: pl.BlockSpec((None, s, DV_A), lambda bi, hi: (bi, 0, off + hi))
    const = lambda shape: pl.BlockSpec(shape, lambda bi, hi: (0,) * len(shape))
    return pl.pallas_call(
        functools.partial(_attn_body, lambda_init=lambda_init, bq=bq),
        grid=(b, h_a),
        in_specs=[blk(0), blk(h_a), blk(2 * h_a),
                  const((s, LANES)), const((s, LANES)), const((s, LANES)),
                  const((1, LANES)), const((1, LANES)), const((4, DH_A)), const((1, LANES))],
        out_specs=pl.BlockSpec((None, s, DV_A), lambda bi, hi: (bi, 0, hi)),
        out_shape=jax.ShapeDtypeStruct((b, s, h_a * DV_A), F32),
        scratch_shapes=[pltpu.VMEM((s, LANES), BF16)] * 4,
        compiler_params=pltpu.CompilerParams(
            dimension_semantics=("arbitrary", "arbitrary"), vmem_limit_bytes=VMEM_LIMIT),
        name="diff_attn",
    )(ua3, ua3, ua3, cosf, sina, sinb, gq, gk, lvec, sg)


def _rwkv_body(u_ref, mu_ref, w0_ref, wlu_ref, a0_ref, alu_ref, glu_ref, kkw_ref, ka_ref, rk_ref,
               lg_ref, lb_ref, bd_ref, tri_ref, o_ref, prev_s, st_s, *, c_len):
    tr = u_ref.shape[0]
    d_r = o_ref.shape[1]
    n_pair = d_r // LANES
    n_ch = tr // c_len

    @pl.when(pl.program_id(1) == 0)
    def _():
        prev_s[...] = jnp.zeros_like(prev_s)
        st_s[...] = jnp.zeros_like(st_s)

    u = u_ref[...]
    row = lax.broadcasted_iota(I32, (tr, 1), 0)
    u_prev = jnp.where(row == 0, prev_s[...], pltpu.roll(u, 1, 0))
    prev_s[...] = u[tr - 1:tr, :]
    us = u + (u_prev - u) * mu_ref[...]

    r = us[:, 0:d_r]
    k = us[:, d_r:2 * d_r]
    v = us[:, 2 * d_r:3 * d_r]
    lo_in = us[:, 3 * d_r:3 * d_r + LANES]
    g_in = us[:, 3 * d_r + LANES:3 * d_r + 2 * LANES]

    bd = bd_ref[...]

    def head_sum(x, two_pass=False):
        hi, lo = _split(x)
        return _dot(hi, bd) + _dot(lo, bd) if two_pass else _dot(hi, bd)

    z = -(w0_ref[...] + _bdot(jnp.tanh(lo_in), wlu_ref[...]))
    softplus = jnp.maximum(z, 0.0) + jnp.log(1.0 + jnp.exp(-jnp.abs(z)))
    logw = -jnp.exp(-softplus - 0.5)
    a = 1.0 / (1.0 + jnp.exp(-(a0_ref[...] + _bdot(lo_in, alu_ref[...]))))
    g = _bdot(1.0 / (1.0 + jnp.exp(-g_in)), glu_ref[...])
    kk_raw = k * kkw_ref[...]
    kk = kk_raw / jnp.maximum(jnp.sqrt(head_sum(kk_raw * kk_raw, two_pass=True)), 1e-12)
    k_mod = k * (1.0 + (a - 1.0) * ka_ref[...])

    tri = tri_ref[...]
    w_hi, w_lo = _split(logw)
    cum = _dot(tri, w_hi) + _dot(tri, w_lo)
    last_rows = [cum[(c + 1) * c_len - 1:(c + 1) * c_len, :] for c in range(n_ch)]
    cum_last = jnp.concatenate([jnp.broadcast_to(lr, (c_len, d_r)) for lr in last_rows], axis=0)
    e_neg = jnp.exp(-cum)
    e_rem = jnp.exp(cum_last - cum)
    a_t = -kk * jnp.exp(cum - logw)
    r_t = r * jnp.exp(cum)
    b_raw = kk * a
    b_t = b_raw * e_neg
    k_t = k_mod * e_neg
    b_2 = b_raw * e_rem
    k_2 = k_mod * e_rem

    lane = lax.broadcasted_iota(I32, (1, LANES), 1)
    head0 = lane < RWKV_HEAD
    two_c = 2 * c_len
    rr = lax.broadcasted_iota(I32, (two_c, two_c), 0)
    cc = lax.broadcasted_iota(I32, (two_c, two_c), 1)
    t_i = rr & (c_len - 1)
    s_i = cc & (c_len - 1)
    strict = t_i > s_i
    incl = t_i >= s_i
    eye = rr == cc

    def stack(x, c, j):
        xp = x[c * c_len:(c + 1) * c_len, j * LANES:(j + 1) * LANES]
        return jnp.concatenate([jnp.where(head0, xp, 0.0), jnp.where(head0, 0.0, xp)], axis=0)

    units = [(c, j) for c in range(n_ch) for j in range(n_pair)]
    n_double = int(math.log2(c_len))

    a_s = [stack(a_t, c, j) for c, j in units]
    r_s = [stack(r_t, c, j) for c, j in units]
    v_b = [stack(v, c, j).astype(BF16) for c, j in units]
    gram = [_dot_nt(jnp.concatenate([a_s[i], r_s[i]], axis=0).astype(BF16),
                    jnp.concatenate([stack(b_t, c, j), stack(k_t, c, j)], axis=0).astype(BF16))
            for i, (c, j) in enumerate(units)]
    p = [jnp.where(strict, gm[:two_c, :two_c], 0.0) for gm in gram]
    m_rb = [jnp.where(incl, gm[two_c:, :two_c], 0.0).astype(BF16) for gm in gram]
    m_rk = [jnp.where(incl, gm[two_c:, two_c:], 0.0).astype(BF16) for gm in gram]
    vv = [_dot(jnp.where(strict, gm[:two_c, two_c:], 0.0).astype(BF16), vb)
          for gm, vb in zip(gram, v_b)]
    assert c_len == RWKV_HEAD and 2 * c_len == LANES
    half = LANES // 2
    lo_half = lax.broadcasted_iota(I32, (1, LANES), 1) < half
    keep = lax.broadcasted_iota(I32, (1, 2 * LANES), 1) >= half
    zeros_b = jnp.zeros((c_len, 2 * LANES), BF16)
    z = []
    for i in range(len(units)):
        for hp in range(2):
            rs = slice(hp * c_len, (hp + 1) * c_len)
            p_r, a_r, v_r = p[i][rs, :], a_s[i][rs, :], vv[i][rs, :]
            if hp == 0:
                z.append(jnp.concatenate([p_r + pltpu.roll(a_r, half, 1), v_r], axis=1))
            else:
                z.append(jnp.concatenate([pltpu.roll(p_r, half, 1) + a_r, pltpu.roll(v_r, half, 1)], axis=1))
    for it in range(n_double):
        lhs = [jnp.where(lo_half, zi[:, :LANES], 0.0).astype(BF16) for zi in z]
        rhs = [jnp.concatenate([zi.astype(BF16), zeros_b], axis=0) for zi in z]
        z = [_dot(li, ri) + jnp.where(keep, zi, 0.0) for li, ri, zi in zip(lhs, rhs, z)]
    x_b = []
    for i in range(len(units)):
        z0, z1 = z[2 * i], z[2 * i + 1]
        rows0 = jnp.concatenate([pltpu.roll(jnp.where(lo_half, 0.0, z0[:, :LANES]), half, 1),
                                 z0[:, LANES:]], axis=1)
        rows1 = jnp.concatenate([jnp.where(lo_half, 0.0, z1[:, :LANES]),
                                 pltpu.roll(z1[:, LANES:], half, 1)], axis=1)
        x_b.append(jnp.concatenate([rows0, rows1], axis=0).astype(BF16))
    zeros_v = jnp.zeros((two_c, two_c), BF16)
    rhs2 = [jnp.concatenate([x_b[i], jnp.concatenate([zeros_v, v_b[i]], axis=1)], axis=0)
            for i in range(len(units))]
    mx = [_dot(jnp.concatenate([m_rb[i], m_rk[i]], axis=1), rhs2[i]) for i in range(len(units))]
    bx = [_dot(jnp.concatenate([stack(b_2, c, j).T, stack(k_2, c, j).T], axis=1).astype(BF16), rhs2[i])
          for i, (c, j) in enumerate(units)]

    y_rows = []
    st = [st_s[j] for j in range(n_pair)]
    for c in range(n_ch):
        p_c = jnp.exp(last_rows[c])
        y_pairs = []
        for j in range(n_pair):
            i = c * n_pair + j
            r_h = (r_s[i] + mx[i][:, :two_c]).astype(BF16)
            y_h = mx[i][:, two_c:]
            g_m = (jnp.where(eye, p_c[:, j * LANES:(j + 1) * LANES], 0.0) + bx[i][:, :two_c]).astype(BF16)
            h_m = bx[i][:, two_c:]
            st_b = st[j].astype(BF16)
            y_s = _dot(r_h, st_b) + y_h
            st[j] = _dot(g_m, st_b) + h_m
            y_pairs.append(y_s[:c_len] + y_s[c_len:])
        y_rows.append(jnp.concatenate(y_pairs, axis=1))
    for j in range(n_pair):
        st_s[j] = st[j]
    y = jnp.concatenate(y_rows, axis=0)

    inv_n = 1.0 / RWKV_HEAD
    mu = head_sum(y) * inv_n
    dlt = y - mu
    var = head_sum(dlt * dlt) * inv_n
    yn = dlt * lax.rsqrt(var + LN_X_EPS) * lg_ref[...] + lb_ref[...]
    bonus = head_sum(r * k_mod * rk_ref[...]) * v
    o_ref[...] = (yn + bonus) * g


def _rwkv7(ub3, mu, w0, wlu, a0, alu, glu, kkw, ka, rk, lg, lb, bd, tri, c_len):
    b, s, cols = ub3.shape
    d_r = w0.shape[1]
    tr = tri.shape[0]
    const = lambda a: pl.BlockSpec(a.shape, lambda bi, ci: (0,) * a.ndim)
    params = (mu, w0, wlu, a0, alu, glu, kkw, ka, rk, lg, lb, bd, tri)
    return pl.pallas_call(
        functools.partial(_rwkv_body, c_len=c_len),
        grid=(b, s // tr),
        in_specs=[pl.BlockSpec((None, tr, cols), lambda bi, ci: (bi, ci, 0))]
                 + [const(a) for a in params],
        out_specs=pl.BlockSpec((None, tr, d_r), lambda bi, ci: (bi, ci, 0)),
        out_shape=jax.ShapeDtypeStruct((b, s, d_r), F32),
        scratch_shapes=[pltpu.VMEM((1, cols), F32),
                        pltpu.VMEM((d_r // LANES, LANES, LANES), F32)],
        compiler_params=pltpu.CompilerParams(
            dimension_semantics=("arbitrary", "arbitrary"), vmem_limit_bytes=VMEM_LIMIT),
        name="rwkv7",
    )(ub3, *params)


def _out_body(x_ref, oa_ref, ob_ref, wa_ref, wb_ref, g2_ref, wr_ref, br_ref, tril_ref,
              h_ref, t_ref, eid_ref, gate_ref, rank_ref, cnt_ref, run_s):
    tm = x_ref.shape[0]

    @pl.when(pl.program_id(0) == 0)
    def _():
        run_s[...] = jnp.zeros_like(run_s)

    h = x_ref[...] + _bdot(oa_ref[...], wa_ref[...]) + _bdot(ob_ref[...], wb_ref[...])
    h_ref[...] = h
    t = h * lax.rsqrt(jnp.mean(h * h, axis=-1, keepdims=True) + NORM_EPS) * g2_ref[...]
    _tile_rows_store(t_ref, t)

    t_hi, t_lo = _split(t)
    wr = wr_ref[...]
    hh = _dot(t_hi, wr)
    lg = hh[:, :LANES] + _dot(t_lo, wr[:, :LANES]) + hh[:, LANES:] + br_ref[...]

    lane_i = lax.broadcasted_iota(I32, (1, LANES), 1)
    lane = lane_i.astype(F32)
    big = float(LANES)

    def first_index(mask):
        return jnp.min(jnp.where(mask, lane, big), axis=-1, keepdims=True)

    def masked_softmax(mask):
        m = jnp.max(jnp.where(mask, lg, _NEG), axis=-1, keepdims=True)
        e = jnp.where(mask, jnp.exp(lg - m), 0.0)
        return e / jnp.sum(e, axis=-1, keepdims=True)

    is_g = lane_i < N_GROUPS
    gprob = masked_softmax(is_g)
    p_group = jnp.max(gprob, axis=-1, keepdims=True)
    gsel = first_index(is_g & (gprob == p_group))
    base = N_GROUPS + gsel * EXPERTS_PER_GROUP
    in_e = (lane >= base) & (lane < base + EXPERTS_PER_GROUP)
    eprob = masked_softmax(in_e)
    p1 = jnp.max(eprob, axis=-1, keepdims=True)
    i1 = first_index(in_e & (eprob == p1))
    rest = in_e & (lane != i1)
    p2 = jnp.max(jnp.where(rest, eprob, -1.0), axis=-1, keepdims=True)
    i2 = first_index(rest & (eprob == p2))
    den = p1 + p2
    g1 = p_group * p1 / den
    g2 = p_group * p2 / den
    e1 = i1 - N_GROUPS
    e2 = i2 - N_GROUPS

    oh1 = (lane == e1).astype(F32)
    oh2 = (lane == e2).astype(F32)
    before = _dot(tril_ref[...], jnp.concatenate([oh1, oh2], axis=1).astype(BF16))
    before1 = before[:, :LANES]
    tot1 = jnp.sum(oh1, axis=0, keepdims=True)
    before2 = before[:, LANES:] + tot1
    run = run_s[...]
    rk1 = jnp.sum(oh1 * (before1 + run), axis=-1, keepdims=True)
    rk2 = jnp.sum(oh2 * (before2 + run), axis=-1, keepdims=True)
    run = run + tot1 + jnp.sum(oh2, axis=0, keepdims=True)
    run_s[...] = run
    cnt_ref[...] = run

    sel0 = lane_i == 0
    sel1 = lane_i == 1
    eid_ref[...] = jnp.where(sel0, e1, jnp.where(sel1, e2, 0.0)).astype(I32)
    rank_ref[...] = jnp.where(sel0, rk1, jnp.where(sel1, rk2, 0.0)).astype(I32)
    gate_ref[...] = jnp.where(sel0, g1, jnp.where(sel1, g2, 0.0))


def _out_proj_route(x2, oa, ob, wa, wb, g2, wr, br, tril):
    n, d = x2.shape
    tm = tril.shape[0]
    da, db = oa.shape[1], ob.shape[1]
    row = lambda w: pl.BlockSpec((tm, w), lambda i: (i, 0))
    const = lambda a: pl.BlockSpec(a.shape, lambda i: (0,) * a.ndim)
    return pl.pallas_call(
        _out_body,
        grid=(n // tm,),
        in_specs=[row(d), row(da), row(db), const(wa), const(wb), const(g2), const(wr),
                  const(br), const(tril)],
        out_specs=[row(d), pl.BlockSpec((tm * SUBLANES, LANES), lambda i: (i, 0)),
                   row(LANES), row(LANES), row(LANES), pl.BlockSpec((1, LANES), lambda i: (0, 0))],
        out_shape=[jax.ShapeDtypeStruct((n, d), F32), jax.ShapeDtypeStruct((n * SUBLANES, LANES), F32),
                   jax.ShapeDtypeStruct((n, LANES), I32), jax.ShapeDtypeStruct((n, LANES), F32),
                   jax.ShapeDtypeStruct((n, LANES), I32), jax.ShapeDtypeStruct((1, LANES), F32)],
        scratch_shapes=[pltpu.VMEM((1, LANES), F32)],
        compiler_params=pltpu.CompilerParams(
            dimension_semantics=("arbitrary",), vmem_limit_bytes=VMEM_LIMIT),
        name="out_proj_route",
    )(x2, oa, ob, wa, wb, g2, wr, br, tril)


def _zero_fill_copies(fill_ref, zbuf, xs_ref, sem, op):
    bm = zbuf.shape[0] // SUBLANES
    for e in range(N_EXPERTS):
        start = fill_ref[e]
        length = fill_ref[N_EXPERTS + e]
        for k in range(int(math.log2(bm))):
            size = (1 << k) * SUBLANES
            above = (length >> (k + 1)) << (k + 1)

            @pl.when(((length >> k) & 1) == 1)
            def _(start=start, above=above, size=size):
                dst = pl.multiple_of((start + above) * SUBLANES, SUBLANES)
                op(pltpu.make_async_copy(zbuf.at[pl.ds(0, size)], xs_ref.at[pl.ds(dst, size)], sem))

    tail_start = fill_ref[2 * N_EXPERTS]

    def tail(c, carry):
        dst = pl.multiple_of((tail_start + c * bm) * SUBLANES, SUBLANES)
        op(pltpu.make_async_copy(zbuf, xs_ref.at[pl.ds(dst, bm * SUBLANES)], sem))
        return carry

    lax.fori_loop(0, fill_ref[2 * N_EXPERTS + 1], tail, 0)


def _dispatch_body(d0_ref, d1_ref, fill_ref, t_ref, xs_ref, tbuf, zbuf, sem_in, sem_out, sem_fill):
    tm = d0_ref.shape[0]
    rows = tm * SUBLANES
    tile = pl.program_id(0)
    n_tiles = pl.num_programs(0)
    slot = tile % 2

    @pl.when(tile == 0)
    def _():
        zbuf[...] = jnp.zeros_like(zbuf)
        _zero_fill_copies(fill_ref, zbuf, xs_ref, sem_fill, lambda cp: cp.start())

    def load(idx, s):
        src = pl.multiple_of(idx * rows, SUBLANES)
        return pltpu.make_async_copy(t_ref.at[pl.ds(src, rows)], tbuf.at[s], sem_in.at[s])

    def wait_rows(s):
        for _ in range(2):
            pltpu.make_async_copy(tbuf.at[s], xs_ref.at[pl.ds(0, rows)], sem_out.at[s]).wait()

    @pl.when(tile == 0)
    def _():
        load(0, 0).start()

    @pl.when(tile >= 1)
    def _():
        wait_rows(1 - slot)

    @pl.when(tile + 1 < n_tiles)
    def _():
        load(tile + 1, 1 - slot).start()

    load(tile, slot).wait()

    def copy(i, d_ref):
        dst = pl.multiple_of(d_ref[i] * SUBLANES, SUBLANES)
        return pltpu.make_async_copy(tbuf.at[slot, pl.ds(pl.multiple_of(i * SUBLANES, SUBLANES), SUBLANES)],
                                     xs_ref.at[pl.ds(dst, SUBLANES)], sem_out.at[slot])

    def start(i, carry):
        copy(i, d0_ref).start(priority=0)
        copy(i, d1_ref).start(priority=1)
        return carry

    lax.fori_loop(0, tm, start, 0, unroll=8)

    @pl.when(tile == n_tiles - 1)
    def _():
        wait_rows(slot)
        _zero_fill_copies(fill_ref, zbuf, xs_ref, sem_fill, lambda cp: cp.wait())


def _dispatch(d0, d1, fill, t, m_rows, bm):
    n = d0.shape[0]
    tm = min(512, n)
    smem = pl.BlockSpec((tm,), lambda i: (i,), memory_space=pltpu.SMEM)
    return pl.pallas_call(
        _dispatch_body,
        grid=(n // tm,),
        in_specs=[smem, smem, pl.BlockSpec(memory_space=pltpu.SMEM), pl.BlockSpec(memory_space=pl.ANY)],
        out_specs=pl.BlockSpec(memory_space=pl.ANY),
        out_shape=jax.ShapeDtypeStruct((m_rows * SUBLANES, LANES), t.dtype),
        scratch_shapes=[pltpu.VMEM((2, tm * SUBLANES, LANES), F32), pltpu.VMEM((bm * SUBLANES, LANES), F32),
                        pltpu.SemaphoreType.DMA((2,)), pltpu.SemaphoreType.DMA((2,)),
                        pltpu.SemaphoreType.DMA],
        compiler_params=pltpu.CompilerParams(dimension_semantics=("arbitrary",)),
        name="moe_dispatch",
    )(d0, d1, fill, t)


def _experts_body(be_ref, nu_ref, x_ref, wg_ref, wu_ref, wd_ref, y_ref, wg_s, wu_s, wd_s):
    w = pl.program_id(0)
    bm = x_ref.shape[0] // SUBLANES

    @pl.when(w < nu_ref[0])
    def _():
        @pl.when(jnp.logical_or(w == 0, be_ref[jnp.maximum(w - 1, 0)] != be_ref[w]))
        def _():
            wg_s[...] = wg_ref[...].astype(BF16)
            wu_s[...] = wu_ref[...].astype(BF16)
            wd_s[...] = wd_ref[...].astype(BF16)

        xb = _tile_rows_load(x_ref, bm).astype(BF16)
        gt = _dot(xb, wg_s[...])
        up = _dot(xb, wu_s[...])
        act = (gt / (1.0 + jnp.exp(-gt))) * up
        _tile_rows_store(y_ref, _dot(act.astype(BF16), wd_s[...]))

    @pl.when(w >= nu_ref[0])
    def _():
        y_ref[...] = jnp.zeros_like(y_ref)


def _experts(blk_exp, n_used, xs, wg, wu, wd, bm):
    m = xs.shape[0] // SUBLANES
    d, de = wg.shape[1], wg.shape[2]
    grid_spec = pltpu.PrefetchScalarGridSpec(
        num_scalar_prefetch=2,
        grid=(m // bm,),
        in_specs=[
            pl.BlockSpec((bm * SUBLANES, LANES),
                         lambda w, be, nu: (jnp.minimum(w, jnp.maximum(nu[0] - 1, 0)), 0)),
            pl.BlockSpec((None, d, de), lambda w, be, nu: (be[w], 0, 0)),
            pl.BlockSpec((None, d, de), lambda w, be, nu: (be[w], 0, 0)),
            pl.BlockSpec((None, de, d), lambda w, be, nu: (be[w], 0, 0)),
        ],
        out_specs=pl.BlockSpec((bm * SUBLANES, LANES), lambda w, be, nu: (w, 0)),
        scratch_shapes=[pltpu.VMEM((d, de), BF16), pltpu.VMEM((d, de), BF16),
                        pltpu.VMEM((de, d), BF16)],
    )
    return pl.pallas_call(
        _experts_body,
        grid_spec=grid_spec,
        out_shape=jax.ShapeDtypeStruct((m * SUBLANES, LANES), F32),
        compiler_params=pltpu.CompilerParams(
            dimension_semantics=("arbitrary",), vmem_limit_bytes=VMEM_LIMIT),
        name="moe_experts",
    )(blk_exp, n_used, xs, wg, wu, wd)


def _combine_body(d0_ref, d1_ref, d0n_ref, d1n_ref, y_ref, h_ref, gate_ref, o_ref, buf, sem):
    tm = d0_ref.shape[0]
    tile = pl.program_id(0)
    n_tiles = pl.num_programs(0)

    def gather(da_ref, db_ref, slot):
        def copy(i, d_ref, k):
            src = pl.multiple_of(d_ref[i] * SUBLANES, SUBLANES)
            dst = pl.multiple_of(i * SUBLANES, SUBLANES)
            return pltpu.make_async_copy(y_ref.at[pl.ds(src, SUBLANES)],
                                         buf.at[slot, k, pl.ds(dst, SUBLANES)], sem.at[slot])

        def start(i, carry):
            copy(i, da_ref, 0).start(priority=0)
            copy(i, db_ref, 1).start(priority=1)
            return carry

        lax.fori_loop(0, tm, start, 0, unroll=8)

    slot = tile % 2

    @pl.when(tile == 0)
    def _():
        gather(d0_ref, d1_ref, 0)

    @pl.when(tile + 1 < n_tiles)
    def _():
        gather(d0n_ref, d1n_ref, 1 - slot)

    for k in range(2):
        pltpu.make_async_copy(y_ref.at[pl.ds(0, tm * SUBLANES)], buf.at[slot, k], sem.at[slot]).wait()
    gate = gate_ref[...]
    o_ref[...] = (h_ref[...] + _tile_rows_load(buf.at[slot, 0], tm) * gate[:, 0:1]
                  + _tile_rows_load(buf.at[slot, 1], tm) * gate[:, 1:2])


def _combine(d0, d1, y, h, gate):
    n, d = h.shape
    tm = min(512, n)
    last = n // tm - 1
    smem = pl.BlockSpec((tm,), lambda i: (i,), memory_space=pltpu.SMEM)
    smem_next = pl.BlockSpec((tm,), lambda i: (jnp.minimum(i + 1, last),), memory_space=pltpu.SMEM)
    return pl.pallas_call(
        _combine_body,
        grid=(n // tm,),
        in_specs=[smem, smem, smem_next, smem_next, pl.BlockSpec(memory_space=pl.ANY),
                  pl.BlockSpec((tm, d), lambda i: (i, 0)),
                  pl.BlockSpec((tm, LANES), lambda i: (i, 0))],
        out_specs=pl.BlockSpec((tm, d), lambda i: (i, 0)),
        out_shape=jax.ShapeDtypeStruct((n, d), F32),
        scratch_shapes=[pltpu.VMEM((2, 2, tm * SUBLANES, LANES), F32), pltpu.SemaphoreType.DMA((2,))],
        compiler_params=pltpu.CompilerParams(
            dimension_semantics=("arbitrary",), vmem_limit_bytes=VMEM_LIMIT),
        name="moe_combine",
    )(d0, d1, d0, d1, y, h, gate)


def _rope_tables(seq):
    half = ROT_DIM // 2
    inv = ROPE_THETA ** (-jnp.arange(0, ROT_DIM, 2, dtype=F32) / ROT_DIM)
    ang = jnp.arange(seq, dtype=F32)[:, None] * inv[None, :]
    cos, sin = jnp.cos(ang), jnp.sin(ang)
    ones = jnp.ones((seq, DH_A - ROT_DIM), F32)
    zeros = jnp.zeros((seq, DH_A - half), F32)
    zeros_h = jnp.zeros((seq, half), F32)
    zeros_r = jnp.zeros((seq, DH_A - ROT_DIM), F32)
    cos_c = jnp.concatenate([cos, cos, ones], axis=1)
    sa_c = jnp.concatenate([-sin, zeros], axis=1)
    sb_c = jnp.concatenate([zeros_h, sin, zeros_r], axis=1)
    tile = lambda t: jnp.concatenate([t, t], axis=1)
    return tile(cos_c), tile(sa_c), tile(sb_c)


def _layer(x, l, lambda_init, ln1_g, w_in, q_norm_g, k_norm_g, lambda_q1, lambda_k1, lambda_q2,
           lambda_k2, subln_g, shift_mu, w0, w_lora_up, a0, a_lora_up, g_lora_up, k_k, k_a, r_k,
           ln_x_g, ln_x_b, w_out, ln2_g, w_router_group, b_router_group, w_router_expert,
           b_router_expert, w_exp_gate, w_exp_up, w_exp_down):
    b, s, d = x.shape
    n = b * s
    d_attn3 = 3 * (d // 2)
    d_attn = d // 2
    d_r = w0.shape[1]
    x2 = x.reshape(n, d)
    row = lambda a: a[l].reshape(1, -1)

    w_in_b = w_in[l].astype(BF16)
    ua, ub = _in_proj(x2, row(ln1_g), w_in_b[:, :d_attn3], w_in_b[:, d_attn3:])

    cosf, sina, sinb = _rope_tables(s)
    tile2 = lambda a: jnp.concatenate([a[l], a[l]]).reshape(1, -1)
    lvec = jnp.stack([lambda_q1[l], lambda_k1[l], lambda_q2[l], lambda_k2[l]])
    o_a = _diff_attn(ua.reshape(b, s, d_attn3), cosf, sina, sinb, tile2(q_norm_g), tile2(k_norm_g),
                     lvec, row(subln_g), lambda_init)

    dl = w_lora_up.shape[1]
    zeros_l = jnp.zeros((dl, d_r), F32)
    wlu = jnp.concatenate([w_lora_up[l], zeros_l], axis=0).astype(BF16)
    alu = jnp.concatenate([zeros_l, a_lora_up[l]], axis=0).astype(BF16)
    hid = jnp.arange(d_r) // RWKV_HEAD
    bd = (hid[:, None] == hid[None, :]).astype(BF16)
    tr = min(RWKV_ROWS, s)
    ti = jnp.arange(tr)
    tri = ((ti[:, None] >= ti[None, :]) & (ti[:, None] // CHUNK == ti[None, :] // CHUNK)).astype(BF16)
    o_b = _rwkv7(ub.reshape(b, s, -1), row(shift_mu), row(w0), wlu, row(a0), alu,
                 g_lora_up[l].astype(BF16), row(k_k), row(k_a), row(r_k), row(ln_x_g), row(ln_x_b),
                 bd, tri, CHUNK)

    w_out_b = w_out[l].astype(BF16)
    wr = jnp.concatenate([w_router_group[l], w_router_expert[l],
                          jnp.zeros((d, LANES - N_GROUPS - N_EXPERTS), F32)], axis=1)
    wr_hi = wr.astype(BF16)
    wr_cat = jnp.concatenate([wr_hi, (wr - wr_hi.astype(F32)).astype(BF16)], axis=1)
    br = jnp.concatenate([b_router_group[l], b_router_expert[l],
                          jnp.zeros((LANES - N_GROUPS - N_EXPERTS,), F32)]).reshape(1, LANES)
    tm_r = min(512, n)
    tril = (jnp.arange(tm_r)[:, None] > jnp.arange(tm_r)[None, :]).astype(BF16)
    h, t, eid, gate, rank, cnt = _out_proj_route(
        x2, o_a.reshape(n, d_attn), o_b.reshape(n, d_r), w_out_b[:d_attn], w_out_b[d_attn:],
        row(ln2_g), wr_cat, br, tril)

    bm = min(EXPERT_ROWS, 2 * n)
    counts = cnt[0, :N_EXPERTS].astype(I32)
    padded = (counts + bm - 1) // bm * bm
    gend = jnp.cumsum(padded)
    gstart = gend - padded
    onehot = (eid[:, :2, None] == jnp.arange(N_EXPERTS)[None, None, :])
    dest = jnp.sum(jnp.where(onehot, gstart, 0), axis=-1) + rank[:, :2]
    d0 = dest[:, 0].astype(I32)
    d1 = dest[:, 1].astype(I32)

    n_blk = 2 * n // bm + N_EXPERTS
    used_blk = gend[-1] // bm
    blk_ids = jnp.arange(n_blk)
    blk_exp = jnp.minimum(jnp.sum((gend[None, :] <= (blk_ids * bm)[:, None]).astype(I32), axis=1),
                          N_EXPERTS - 1)
    blk_exp = jnp.where(blk_ids < used_blk, blk_exp, blk_exp[jnp.maximum(used_blk - 1, 0)]).astype(I32)
    fill = jnp.concatenate([gstart + counts, padded - counts,
                            jnp.stack([gend[-1], n_blk - used_blk])]).astype(I32)

    xs = _dispatch(d0, d1, fill, t, n_blk * bm, bm)
    y = _experts(blk_exp, used_blk.reshape(1).astype(I32), xs,
                 w_exp_gate[l], w_exp_up[l], w_exp_down[l], bm)
    out = _combine(d0, d1, y, h, gate)
    return out.reshape(b, s, d)


def kernel(x, ln1_g, w_in, q_norm_g, k_norm_g, lambda_q1, lambda_k1, lambda_q2, lambda_k2, subln_g, shift_mu, w0, w_lora_up, a0, a_lora_up, g_lora_up, k_k, k_a, r_k, ln_x_g, ln_x_b, w_out, ln2_g, w_router_group, b_router_group, w_router_expert, b_router_expert, w_exp_gate, w_exp_up, w_exp_down):
    h = x
    for l in range(ln1_g.shape[0]):
        lambda_init = 0.8 - 0.6 * math.exp(-0.3 * l)
        h = _layer(h, l, lambda_init, ln1_g, w_in, q_norm_g, k_norm_g, lambda_q1, lambda_k1,
                   lambda_q2, lambda_k2, subln_g, shift_mu, w0, w_lora_up, a0, a_lora_up, g_lora_up,
                   k_k, k_a, r_k, ln_x_g, ln_x_b, w_out, ln2_g, w_router_group, b_router_group,
                   w_router_expert, b_router_expert, w_exp_gate, w_exp_up, w_exp_down)
    return h
```
